```python
import math
import jax, jax.numpy as jnp
from jax import lax
import numpy as np

D_MODEL = 1024
BATCH = 4
SEQ = 8192
DEPTH = 2

CHUNK = 64
Q_BLOCK = 128
D_MIX = D_MODEL
LRU_WIDTH = D_MIX // 2
LRU_BLOCKS = 8
LRU_BLOCK_W = LRU_WIDTH // LRU_BLOCKS
CONV_W = 4
RG_C = 8.0
ATTN_WIDTH = D_MIX - LRU_WIDTH
DIFF_HEADS = 4
DIFF_VDIM = ATTN_WIDTH // DIFF_HEADS
DIFF_DH = DIFF_VDIM // 2
ROPE_THETA = 10000.0
D_IN = 2 * LRU_WIDTH + 3 * ATTN_WIDTH
D_FF = 3 * D_MODEL
N_EXPERTS = 8
TOP_K = 2
N_DENSE = (DEPTH + 1) // 2
N_MOE = DEPTH // 2
EPS = 1e-6
MAX_OFFSET_CHUNKS = 64

kernel_name = 'hybrid_rglru_diffattn_moe_adaln'


def rmsnorm(x, g):
    x32 = x.astype(jnp.float32)
    y = x32 * lax.rsqrt(jnp.mean(x32 * x32, axis=-1, keepdims=True) + EPS)
    return y.astype(x.dtype) * g


def rope5(x, pos):
    inv = ROPE_THETA ** (-jnp.arange(0, DIFF_DH, 2, dtype=jnp.float32) / DIFF_DH)
    ang = pos.astype(jnp.float32)[..., None] * inv
    ang = jnp.concatenate([ang, ang], axis=-1)[:, :, None, None, :]
    cos, sin = jnp.cos(ang), jnp.sin(ang)
    x1, x2 = jnp.split(x, 2, axis=-1)
    rot = jnp.concatenate([-x2, x1], axis=-1)
    return (x * cos + rot * sin).astype(x.dtype)


def causal_conv(x, w, b):
    rhs = w[:, None, :]
    y = lax.conv_general_dilated(x, rhs, window_strides=(1,), padding=[(CONV_W - 1, 0)],
                                 dimension_numbers=('NWC', 'WIO', 'NWC'),
                                 feature_group_count=x.shape[-1])
    return y + b


def _lin_rec(e1, e2):
    a1, b1 = e1
    a2, b2 = e2
    return a1 * a2, a2 * b1 + b2


def rg_lru(x, wa, ba, wx, bx, lam):
    B, S, W = x.shape
    xb = x.reshape(B, S, LRU_BLOCKS, LRU_BLOCK_W)
    r = jax.nn.sigmoid(jnp.einsum('bsnh,nhk->bsnk', xb, wa).reshape(B, S, W) + ba)
    i = jax.nn.sigmoid(jnp.einsum('bsnh,nhk->bsnk', xb, wx).reshape(B, S, W) + bx)
    log_a = -RG_C * r.astype(jnp.float32) * jax.nn.softplus(-lam.astype(jnp.float32))
    a = jnp.exp(log_a)
    mult = jnp.sqrt(-jnp.expm1(2.0 * log_a))
    bt = mult * (i * x).astype(jnp.float32)
    _, h = lax.associative_scan(_lin_rec, (a, bt), axis=1)
    return h.astype(x.dtype)


def diff_attention(q, k, v, pos, lq1, lk1, lq2, lk2, subln_g, lambda_init):
    B, S, _ = q.shape
    q = rope5(q.reshape(B, S, DIFF_HEADS, 2, DIFF_DH), pos)
    k = rope5(k.reshape(B, S, DIFF_HEADS, 2, DIFF_DH), pos)
    v = v.reshape(B, S, DIFF_HEADS, DIFF_VDIM)
    lam = (jnp.exp(jnp.sum(lq1.astype(jnp.float32) * lk1.astype(jnp.float32)))
           - jnp.exp(jnp.sum(lq2.astype(jnp.float32) * lk2.astype(jnp.float32)))
           + lambda_init)
    chunk = pos // CHUNK
    nqb = S // Q_BLOCK
    qs = q.reshape(B, nqb, Q_BLOCK, DIFF_HEADS, 2, DIFF_DH).transpose(1, 0, 3, 4, 2, 5)
    qc = chunk.reshape(B, nqb, Q_BLOCK).transpose(1, 0, 2)
    kt = k.transpose(0, 2, 3, 1, 4)
    vt = v.transpose(0, 2, 1, 3)
    scale = DIFF_DH ** -0.5
    neg = jnp.finfo(jnp.float32).min

    def attend(args):
        qb, qcb = args
        s = jnp.einsum('bhcqd,bhckd->bhcqk', qb, kt).astype(jnp.float32) * scale
        mask = (chunk[:, None, :] <= qcb[:, :, None])[:, None, None]
        p = jax.nn.softmax(jnp.where(mask, s, neg), axis=-1)
        pd = p[:, :, 0] - lam * p[:, :, 1]
        return jnp.einsum('bhqk,bhkd->bhqd', pd.astype(vt.dtype), vt)

    o = lax.map(attend, (qs, qc))
    o = o.transpose(1, 0, 3, 2, 4).reshape(B, S, DIFF_HEADS, DIFF_VDIM)
    o = rmsnorm(o, subln_g) * (1.0 - lambda_init)
    return o.reshape(B, S, ATTN_WIDTH)


def swiglu(h, wg, wu, wd):
    return (jax.nn.silu(h @ wg) * (h @ wu)) @ wd


def moe(h, router, wg, wu, wd):
    logits = (h @ router).astype(jnp.float32)
    vals, idx = lax.top_k(logits, TOP_K)
    w = jax.nn.softmax(vals, axis=-1)
    gates = jnp.sum(jax.nn.one_hot(idx, N_EXPERTS, dtype=jnp.float32) * w[..., None], axis=-2)
    y = jnp.zeros_like(h)
    for e in range(N_EXPERTS):
        y = y + gates[..., e:e + 1].astype(h.dtype) * swiglu(h, wg[e], wu[e], wd[e])
    return y


def setup_inputs(seed: int = 0) -> dict:
    key = jax.random.key(seed)
    ks = iter(jax.random.split(key, 40))
    f32 = jnp.float32

    def nrm(shape, scale):
        return jax.random.normal(next(ks), shape, f32) * scale

    a0 = jax.random.uniform(next(ks), (DEPTH, LRU_WIDTH), f32, 0.9, 0.999)
    s0 = a0 ** (1.0 / RG_C)
    lru_lambda = jnp.log(s0) - jnp.log1p(-s0)
    offsets = jax.random.randint(next(ks), (BATCH, 1), 0, MAX_OFFSET_CHUNKS) * CHUNK
    positions = (offsets + jnp.arange(SEQ, dtype=jnp.int32)[None, :]).astype(jnp.int32)
    return {
        'x': nrm((BATCH, SEQ, D_MODEL), 1.0),
        'c': nrm((BATCH, D_MODEL), 1.0),
        'positions': positions,
        'ada_w': nrm((DEPTH, D_MODEL, 6 * D_MODEL), 0.5 * D_MODEL ** -0.5),
        'ada_b': nrm((DEPTH, 6 * D_MODEL), 0.02),
        'ln1_g': 1.0 + nrm((DEPTH, D_MODEL), 0.02),
        'ln2_g': 1.0 + nrm((DEPTH, D_MODEL), 0.02),
        'w_in': nrm((DEPTH, D_MODEL, D_IN), D_MODEL ** -0.5),
        'conv_w': nrm((DEPTH, CONV_W, LRU_WIDTH), CONV_W ** -0.5),
        'conv_b': nrm((DEPTH, LRU_WIDTH), 0.02),
        'gate_a_w': nrm((DEPTH, LRU_BLOCKS, LRU_BLOCK_W, LRU_BLOCK_W), LRU_BLOCK_W ** -0.5),
        'gate_a_b': nrm((DEPTH, LRU_WIDTH), 0.02),
        'gate_x_w': nrm((DEPTH, LRU_BLOCKS, LRU_BLOCK_W, LRU_BLOCK_W), LRU_BLOCK_W ** -0.5),
        'gate_x_b': nrm((DEPTH, LRU_WIDTH), 0.02),
        'lru_lambda': lru_lambda,
        'lam_q1': nrm((DEPTH, DIFF_DH), 0.1),
        'lam_k1': nrm((DEPTH, DIFF_DH), 0.1),
        'lam_q2': nrm((DEPTH, DIFF_DH), 0.1),
        'lam_k2': nrm((DEPTH, DIFF_DH), 0.1),
        'subln_g': 1.0 + nrm((DEPTH, DIFF_VDIM), 0.02),
        'w_out': nrm((DEPTH, D_MIX, D_MODEL), D_MIX ** -0.5),
        'ffn_w_gate': nrm((N_DENSE, D_MODEL, D_FF), D_MODEL ** -0.5),
        'ffn_w_up': nrm((N_DENSE, D_MODEL, D_FF), D_MODEL ** -0.5),
        'ffn_w_down': nrm((N_DENSE, D_FF, D_MODEL), D_FF ** -0.5),
        'moe_router': nrm((N_MOE, D_MODEL, N_EXPERTS), D_MODEL ** -0.5),
        'moe_w_gate': nrm((N_MOE, N_EXPERTS, D_MODEL, D_FF), D_MODEL ** -0.5),
        'moe_w_up': nrm((N_MOE, N_EXPERTS, D_MODEL, D_FF), D_MODEL ** -0.5),
        'moe_w_down': nrm((N_MOE, N_EXPERTS, D_FF, D_MODEL), D_FF ** -0.5),
        'final_g': 1.0 + nrm((D_MODEL,), 0.02),
    }


def reference(x, c, positions, ada_w, ada_b, ln1_g, ln2_g, w_in, conv_w, conv_b,
              gate_a_w, gate_a_b, gate_x_w, gate_x_b, lru_lambda, lam_q1, lam_k1,
              lam_q2, lam_k2, subln_g, w_out, ffn_w_gate, ffn_w_up, ffn_w_down,
              moe_router, moe_w_gate, moe_w_up, moe_w_down, final_g):
    splits = [LRU_WIDTH, 2 * LRU_WIDTH, 2 * LRU_WIDTH + ATTN_WIDTH, 2 * LRU_WIDTH + 2 * ATTN_WIDTH]
    for l in range(DEPTH):
        lambda_init = 0.8 - 0.6 * math.exp(-0.3 * l)
        mod = jax.nn.silu(c) @ ada_w[l] + ada_b[l]
        sh1, sc1, g1, sh2, sc2, g2 = jnp.split(mod[:, None, :], 6, axis=-1)
        h = rmsnorm(x, ln1_g[l]) * (1.0 + sc1) + sh1
        z = h @ w_in[l]
        xr, yr, q, k, v = jnp.split(z, splits, axis=-1)
        lru = rg_lru(causal_conv(xr, conv_w[l], conv_b[l]), gate_a_w[l], gate_a_b[l],
                     gate_x_w[l], gate_x_b[l], lru_lambda[l]) * jax.nn.gelu(yr)
        att = diff_attention(q, k, v, positions, lam_q1[l], lam_k1[l], lam_q2[l], lam_k2[l],
                             subln_g[l], lambda_init)
        x = x + g1 * (jnp.concatenate([lru, att], axis=-1) @ w_out[l])
        h = rmsnorm(x, ln2_g[l]) * (1.0 + sc2) + sh2
        if l % 2 == 0:
            j = l // 2
            f = swiglu(h, ffn_w_gate[j], ffn_w_up[j], ffn_w_down[j])
        else:
            j = l // 2
            f = moe(h, moe_router[j], moe_w_gate[j], moe_w_up[j], moe_w_down[j])
        x = x + g2 * f
    return rmsnorm(x, final_g)
```

```python
import functools
import math

import jax
import jax.numpy as jnp
from jax import lax
from jax.experimental import pallas as pl
from jax.experimental.pallas import tpu as pltpu

F32 = jnp.float32
BF16 = jnp.bfloat16
HIGHEST = lax.Precision.HIGHEST

CHUNK = 64
LRU_BLOCKS = 8
CONV_W = 4
RG_C = 8.0
DIFF_HEADS = 4
ROPE_THETA = 10000.0
N_EXPERTS = 8
EPS = 1e-6
LANES = 128
SUBLANES = 8
VMEM_LIMIT = 56 * 1024 * 1024
MASK_VALUE = -0.5 * float(jnp.finfo(jnp.float32).max)


def _cparams(sem):
    return pltpu.CompilerParams(dimension_semantics=sem, vmem_limit_bytes=VMEM_LIMIT)


def _rmsnorm(x, g):
    return x * lax.rsqrt(jnp.mean(x * x, axis=-1, keepdims=True) + EPS) * g


def _mod_kernel(c_ref, w_ref, b_ref, o_ref):
    c = c_ref[...]
    s = c * jax.nn.sigmoid(c)
    o_ref[...] = jnp.dot(s, w_ref[...], precision=HIGHEST,
                         preferred_element_type=F32) + b_ref[...]


def _modulation(c, ada_w, ada_b, tn=1024):
    depth, d, n = ada_w.shape
    bsz = c.shape[0]
    rows = -(-bsz // SUBLANES) * SUBLANES
    c_pad = jnp.zeros((rows, d), F32).at[:bsz].set(c)
    out = pl.pallas_call(
        _mod_kernel,
        grid=(depth, n // tn),
        in_specs=[
            pl.BlockSpec((rows, d), lambda l, j: (0, 0)),
            pl.BlockSpec((None, d, tn), lambda l, j: (l, 0, j)),
            pl.BlockSpec((None, 1, tn), lambda l, j: (l, 0, j)),
        ],
        out_specs=pl.BlockSpec((None, rows, tn), lambda l, j: (l, 0, j)),
        out_shape=jax.ShapeDtypeStruct((depth, rows, n), F32),
        compiler_params=_cparams(("arbitrary", "arbitrary")),
        name="adaln_mod",
    )(c_pad, ada_w, ada_b.reshape(depth, 1, n))
    return out[:, :bsz].reshape(depth, bsz, 6, 1, d)


def _rope_table_kernel(pos_ref, inv_ref, cos_ref, sin_ref):
    ang = pos_ref[...].astype(F32) * inv_ref[...]
    cos_ref[...] = jnp.cos(ang)
    sin_ref[...] = jnp.sin(ang)


def _rope_tables(positions, dh):
    n_freq = dh // 2
    per_row = LANES // n_freq
    tok = positions.size
    rows = tok // per_row
    inv = ROPE_THETA ** (-jnp.arange(0, dh, 2, dtype=F32) / dh)
    pos_x = jnp.repeat(positions.reshape(-1), n_freq).reshape(rows, LANES)
    inv_x = jnp.tile(inv, per_row).reshape(1, LANES)
    tr = min(rows, 1024)
    cos, sin = pl.pallas_call(
        _rope_table_kernel,
        grid=(rows // tr,),
        in_specs=[pl.BlockSpec((tr, LANES), lambda i: (i, 0)),
                  pl.BlockSpec((1, LANES), lambda i: (0, 0))],
        out_specs=[pl.BlockSpec((tr, LANES), lambda i: (i, 0))] * 2,
        out_shape=[jax.ShapeDtypeStruct((rows, LANES), F32)] * 2,
        compiler_params=_cparams(("arbitrary",)),
        name="rope_tables",
    )(pos_x, inv_x)
    cos = cos.reshape(tok, n_freq)
    sin = sin.reshape(tok, n_freq)
    reps = LANES // dh
    cos_t = jnp.tile(jnp.concatenate([cos, cos], axis=-1), (1, reps))
    sin_t = jnp.tile(jnp.concatenate([-sin, sin], axis=-1), (1, reps))
    return cos_t, sin_t


def _inproj_kernel(x_ref, sc_ref, sh_ref, g_ref, w_ref, cos_ref, sin_ref,
                   xy_ref, q_ref, k_ref, v_ref, *, lru2, attn_w, dh):
    h = _rmsnorm(x_ref[...], g_ref[...]) * (1.0 + sc_ref[...]) + sh_ref[...]
    hb = h.astype(BF16)
    xy_ref[...] = jnp.dot(hb, w_ref[:, :lru2], preferred_element_type=F32)

    reps = attn_w // LANES
    cos = jnp.tile(cos_ref[...], (1, reps))
    sin = jnp.tile(sin_ref[...], (1, reps))
    lane = lax.broadcasted_iota(jnp.int32, cos.shape, 1)
    first_half = (lane % dh) < (dh // 2)

    def rope(t):
        fwd = pltpu.roll(t, attn_w - dh // 2, axis=1)
        bwd = pltpu.roll(t, dh // 2, axis=1)
        return t * cos + jnp.where(first_half, fwd, bwd) * sin

    q = jnp.dot(hb, w_ref[:, lru2:lru2 + attn_w], preferred_element_type=F32)
    q_ref[...] = (rope(q) * (dh ** -0.5)).astype(BF16)
    k = jnp.dot(hb, w_ref[:, lru2 + attn_w:lru2 + 2 * attn_w], preferred_element_type=F32)
    k_ref[...] = rope(k).astype(BF16)
    v = jnp.dot(hb, w_ref[:, lru2 + 2 * attn_w:], preferred_element_type=F32)
    v_ref[...] = v.astype(BF16)


def _inproj(x, mod_l, ln_g, w_in_b, cos_t, sin_t, *, lru_w, attn_w, dh, tm=512):
    bsz, seq, d = x.shape
    nt = seq // tm
    d_in = w_in_b.shape[1]
    lru2 = 2 * lru_w
    row = lambda k: pl.BlockSpec((None, None, 1, d), lambda b, i: (b, k, 0, 0))
    tok = lambda w: pl.BlockSpec((None, tm, w), lambda b, i: (b, i, 0))
    return pl.pallas_call(
        functools.partial(_inproj_kernel, lru2=lru2, attn_w=attn_w, dh=dh),
        grid=(bsz, nt),
        in_specs=[
            tok(d), row(1), row(0),
            pl.BlockSpec((1, d), lambda b, i: (0, 0)),
            pl.BlockSpec((d, d_in), lambda b, i: (0, 0)),
            pl.BlockSpec((tm, LANES), lambda b, i: (b * nt + i, 0)),
            pl.BlockSpec((tm, LANES), lambda b, i: (b * nt + i, 0)),
        ],
        out_specs=[tok(lru2), tok(attn_w), tok(attn_w), tok(attn_w)],
        out_shape=[jax.ShapeDtypeStruct((bsz, seq, lru2), F32),
                   jax.ShapeDtypeStruct((bsz, seq, attn_w), BF16),
                   jax.ShapeDtypeStruct((bsz, seq, attn_w), BF16),
                   jax.ShapeDtypeStruct((bsz, seq, attn_w), BF16)],
        compiler_params=_cparams(("arbitrary", "arbitrary")),
        name="inproj",
    )(x, mod_l, mod_l, ln_g.reshape(1, d), w_in_b, cos_t, sin_t)


def _gelu_tanh(x):
    return 0.5 * x * (1.0 + jnp.tanh(math.sqrt(2.0 / math.pi) * (x + 0.044715 * (x * x * x))))


def _lru_kernel(xy_ref, cw_ref, cb_ref, wg_ref, bg_ref, lam_ref, o_ref,
                xpad_ref, h_ref, *, t, w):
    @pl.when(pl.program_id(1) == 0)
    def _():
        xpad_ref[0:SUBLANES, :] = jnp.zeros((SUBLANES, w), F32)
        h_ref[...] = jnp.zeros_like(h_ref)

    xpad_ref[SUBLANES:SUBLANES + t, :] = xy_ref[:, :w]
    u = cb_ref[...]
    for j in range(CONV_W):
        off = SUBLANES - (CONV_W - 1) + j
        u = u + cw_ref[j:j + 1, :] * xpad_ref[off:off + t, :]
    xpad_ref[0:SUBLANES, :] = xpad_ref[t:t + SUBLANES, :]

    gates = jnp.dot(u.astype(BF16), wg_ref[...], preferred_element_type=F32) + bg_ref[...]
    r = jax.nn.sigmoid(gates[:, :w])
    ig = jax.nn.sigmoid(gates[:, w:])
    neg_lam = -lam_ref[...]
    softplus = jnp.maximum(neg_lam, 0.0) + jnp.log1p(jnp.exp(-jnp.abs(neg_lam)))
    log_a = (-RG_C) * r * softplus
    a = jnp.exp(log_a)
    bt = jnp.sqrt(1.0 - a * a) * (ig * u)

    row = lax.broadcasted_iota(jnp.int32, (t, w), 0)
    shift = 1
    while shift < t:
        keep = row >= shift
        a_prev = jnp.where(keep, pltpu.roll(a, shift, axis=0), 1.0)
        b_prev = jnp.where(keep, pltpu.roll(bt, shift, axis=0), 0.0)
        bt = a * b_prev + bt
        a = a * a_prev
        shift *= 2
    hs = a * h_ref[...] + bt
    h_ref[...] = hs[t - 1:t, :]
    o_ref[...] = (hs * _gelu_tanh(xy_ref[:, w:])).astype(BF16)


def _block_diag(wb):
    n, bw, _ = wb.shape
    eye = jnp.eye(n, dtype=wb.dtype)
    return jnp.einsum('nhk,nm->nhmk', wb, eye).reshape(n * bw, n * bw)


def _lru(xy, conv_w, conv_b, wa, ba, wx, bx, lam, *, t=256):
    bsz, seq, w2 = xy.shape
    w = w2 // 2
    wg = jnp.concatenate([_block_diag(wa), _block_diag(wx)], axis=1).astype(BF16)
    bg = jnp.concatenate([ba, bx]).reshape(1, 2 * w)
    const = lambda shape: pl.BlockSpec(shape, lambda b, i: (0,) * len(shape))
    return pl.pallas_call(
        functools.partial(_lru_kernel, t=t, w=w),
        grid=(bsz, seq // t),
        in_specs=[
            pl.BlockSpec((None, t, w2), lambda b, i: (b, i, 0)),
            const((CONV_W, w)), const((1, w)), const((w, 2 * w)), const((1, 2 * w)),
            const((1, w)),
        ],
        out_specs=pl.BlockSpec((None, t, w), lambda b, i: (b, i, 0)),
        out_shape=jax.ShapeDtypeStruct((bsz, seq, w), BF16),
        scratch_shapes=[pltpu.VMEM((t + SUBLANES, w), F32), pltpu.VMEM((1, w), F32)],
        compiler_params=_cparams(("arbitrary", "arbitrary")),
        name="rglru",
    )(xy, conv_w, conv_b.reshape(1, w), wg, bg, lam.reshape(1, w))


def _attn_kernel(qmin_ref, qmax_ref, kmin_ref, kmax_ref,
                 q_ref, k_ref, v_ref, cq_ref, ck_ref, lq1_ref, lk1_ref, lq2_ref, lk2_ref,
                 g_ref, o_ref, m_ref, l_ref, acc_ref, *, tq, tk, nk, dh, lambda_init):
    b = pl.program_id(0)
    i = pl.program_id(2)
    m_ref[...] = jnp.full(m_ref.shape, -jnp.inf, F32)
    l_ref[...] = jnp.zeros(l_ref.shape, F32)
    acc_ref[...] = jnp.zeros(acc_ref.shape, F32)

    q = q_ref[...]
    lane = lax.broadcasted_iota(jnp.int32, q.shape, 1)
    qc = (jnp.where(lane < dh, q, jnp.zeros_like(q)), jnp.where(lane >= dh, q, jnp.zeros_like(q)))
    q_lo = qmin_ref[b, i]
    q_hi = qmax_ref[b, i]

    def process(j, masked):
        start = pl.multiple_of(j * tk, tk)
        kb = k_ref[pl.ds(start, tk), :]
        vb = v_ref[pl.ds(start, tk), :]
        if masked:
            ck = ck_ref[:, pl.ds(start, tk)]
            visible = ck <= jnp.tile(cq_ref[...], (1, tk // LANES))
        for c in range(2):
            s = lax.dot_general(qc[c], kb, (((1,), (1,)), ((), ())),
                                preferred_element_type=F32)
            if masked:
                s = jnp.where(visible, s, MASK_VALUE)
            m_prev = m_ref[c]
            m_new = jnp.maximum(m_prev, jnp.max(s, axis=-1, keepdims=True))
            alpha = jnp.exp(m_prev - m_new)
            p = jnp.exp(s - m_new)
            l_ref[c] = alpha * l_ref[c] + jnp.sum(p, axis=-1, keepdims=True)
            acc_ref[c] = alpha * acc_ref[c] + jnp.dot(p.astype(BF16), vb,
                                                      preferred_element_type=F32)
            m_ref[c] = m_new

    def body(j, carry):
        k_lo = kmin_ref[b, j]
        k_hi = kmax_ref[b, j]
        needed = k_lo <= q_hi
        needs_mask = k_hi > q_lo

        @pl.when(jnp.logical_and(needed, needs_mask))
        def _():
            process(j, True)

        @pl.when(jnp.logical_and(needed, jnp.logical_not(needs_mask)))
        def _():
            process(j, False)

        return carry

    lax.fori_loop(0, nk, body, 0)

    lam = (jnp.exp(jnp.sum(lq1_ref[...] * lk1_ref[...], keepdims=True))
           - jnp.exp(jnp.sum(lq2_ref[...] * lk2_ref[...], keepdims=True)) + lambda_init)
    o = acc_ref[0] / l_ref[0] - lam * (acc_ref[1] / l_ref[1])
    o_ref[...] = (_rmsnorm(o, g_ref[...]) * (1.0 - lambda_init)).astype(o_ref.dtype)


def _attention(q, k, v, positions, lq1, lk1, lq2, lk2, subln_g, lambda_init, *,
               dh, tq=512, tk=512):
    bsz, seq, aw = q.shape
    vd = 2 * dh
    heads = aw // vd
    nq, nk = seq // tq, seq // tk
    chunk = positions // CHUNK
    qmin = chunk.reshape(bsz, nq, tq).min(-1)
    qmax = chunk.reshape(bsz, nq, tq).max(-1)
    kmin = chunk.reshape(bsz, nk, tk).min(-1)
    kmax = chunk.reshape(bsz, nk, tk).max(-1)
    cq = jnp.broadcast_to(chunk[:, :, None], (bsz, seq, LANES))
    ck = chunk.reshape(bsz, 1, seq)
    vec = lambda n: pl.BlockSpec((1, n), lambda b, h, i, *_: (0, 0))
    grid_spec = pltpu.PrefetchScalarGridSpec(
        num_scalar_prefetch=4,
        grid=(bsz, heads, nq),
        in_specs=[
            pl.BlockSpec((None, tq, vd), lambda b, h, i, *_: (b, i, h)),
            pl.BlockSpec((None, seq, vd), lambda b, h, i, *_: (b, 0, h)),
            pl.BlockSpec((None, seq, vd), lambda b, h, i, *_: (b, 0, h)),
            pl.BlockSpec((None, tq, LANES), lambda b, h, i, *_: (b, i, 0)),
            pl.BlockSpec((None, 1, seq), lambda b, h, i, *_: (b, 0, 0)),
            vec(dh), vec(dh), vec(dh), vec(dh), vec(vd),
        ],
        out_specs=pl.BlockSpec((None, tq, vd), lambda b, h, i, *_: (b, i, h)),
        scratch_shapes=[pltpu.VMEM((2, tq, 1), F32), pltpu.VMEM((2, tq, 1), F32),
                        pltpu.VMEM((2, tq, vd), F32)],
    )
    return pl.pallas_call(
        functools.partial(_attn_kernel, tq=tq, tk=tk, nk=nk, dh=dh, lambda_init=lambda_init),
        grid_spec=grid_spec,
        out_shape=jax.ShapeDtypeStruct((bsz, seq, aw), BF16),
        compiler_params=_cparams(("arbitrary", "arbitrary", "arbitrary")),
        name="diff_attn",
    )(qmin, qmax, kmin, kmax, q, k, v, cq, ck,
      lq1.reshape(1, dh), lk1.reshape(1, dh), lq2.reshape(1, dh), lk2.reshape(1, dh),
      subln_g.reshape(1, vd))


def _outproj_kernel(*refs, w, with_router):
    if with_router:
        (lru_ref, att_ref, x_ref, g1_ref, sc_ref, sh_ref, ln_ref, wo_ref, rt_ref,
         x1_ref, h2_ref, gates_ref) = refs
    else:
        (lru_ref, att_ref, x_ref, g1_ref, sc_ref, sh_ref, ln_ref, wo_ref,
         x1_ref, h2_ref) = refs
    y = (jnp.dot(lru_ref[...], wo_ref[:w, :], preferred_element_type=F32)
         + jnp.dot(att_ref[...], wo_ref[w:, :], preferred_element_type=F32))
    x1 = x_ref[...] + g1_ref[...] * y
    x1_ref[...] = x1
    h2 = _rmsnorm(x1, ln_ref[...]) * (1.0 + sc_ref[...]) + sh_ref[...]
    h2_ref[...] = h2.astype(BF16)
    if with_router:
        logits = jnp.dot(h2, rt_ref[...], precision=HIGHEST, preferred_element_type=F32)
        lane = lax.broadcasted_iota(jnp.int32, logits.shape, 1)
        lg = jnp.where(lane < N_EXPERTS, logits, -jnp.inf)
        m1 = jnp.max(lg, axis=-1, keepdims=True)
        i1 = jnp.min(jnp.where(lg == m1, lane, LANES), axis=-1, keepdims=True)
        lg2 = jnp.where(lane == i1, -jnp.inf, lg)
        m2 = jnp.max(lg2, axis=-1, keepdims=True)
        i2 = jnp.min(jnp.where(lg2 == m2, lane, LANES), axis=-1, keepdims=True)
        e2 = jnp.exp(m2 - m1)
        w1 = 1.0 / (1.0 + e2)
        w2 = e2 / (1.0 + e2)
        gates_ref[...] = jnp.where(lane == i1, w1, jnp.where(lane == i2, w2, 0.0))


def _outproj(lru, att, x, mod_l, ln_g, w_out_b, router=None, *, tm=512):
    bsz, seq, d = x.shape
    w = lru.shape[-1]
    with_router = router is not None
    row = lambda k: pl.BlockSpec((None, None, 1, d), lambda b, i: (b, k, 0, 0))
    tok = lambda n: pl.BlockSpec((None, tm, n), lambda b, i: (b, i, 0))
    in_specs = [tok(w), tok(w), tok(d), row(2), row(4), row(3),
                pl.BlockSpec((1, d), lambda b, i: (0, 0)),
                pl.BlockSpec((d, d), lambda b, i: (0, 0))]
    args = [lru, att, x, mod_l, mod_l, mod_l, ln_g.reshape(1, d), w_out_b]
    out_specs = [tok(d), tok(d)]
    out_shape = [jax.ShapeDtypeStruct((bsz, seq, d), F32),
                 jax.ShapeDtypeStruct((bsz, seq, d), BF16)]
    if with_router:
        rt = jnp.zeros((d, LANES), F32).at[:, :N_EXPERTS].set(router)
        in_specs.append(pl.BlockSpec((d, LANES), lambda b, i: (0, 0)))
        args.append(rt)
        out_specs.append(tok(LANES))
        out_shape.append(jax.ShapeDtypeStruct((bsz, seq, LANES), F32))
    return pl.pallas_call(
        functools.partial(_outproj_kernel, w=w, with_router=with_router),
        grid=(bsz, seq // tm),
        in_specs=in_specs, out_specs=out_specs, out_shape=out_shape,
        compiler_params=_cparams(("arbitrary", "arbitrary")),
        name="outproj_router" if with_router else "outproj",
    )(*args)


def _ffn(h2, wg, wu, wd, x1, mod_l, gates=None, final_g=None, *, tm=1024, tf=512):
    bsz, seq, d = x1.shape
    n_exp, _, ff = wg.shape
    final_norm = final_g is not None
    tok = lambda n: pl.BlockSpec((None, tm, n), lambda b, i, e, j: (b, i, 0))
    in_specs = [tok(d),
                pl.BlockSpec((None, d, tf), lambda b, i, e, j: (e, 0, j)),
                pl.BlockSpec((None, d, tf), lambda b, i, e, j: (e, 0, j)),
                pl.BlockSpec((None, tf, d), lambda b, i, e, j: (e, j, 0)),
                tok(d),
                pl.BlockSpec((None, None, 1, d), lambda b, i, e, j: (b, 5, 0, 0))]
    args = [h2, wg, wu, wd, x1, mod_l]
    if n_exp > 1:
        in_specs.append(tok(LANES))
        args.append(gates)
    if final_norm:
        in_specs.append(pl.BlockSpec((1, d), lambda b, i, e, j: (0, 0)))
        args.append(final_g.reshape(1, d))

    return pl.pallas_call(
        functools.partial(_ffn_kernel, n_exp=n_exp, final_norm=final_norm),
        grid=(bsz, seq // tm, n_exp, ff // tf),
        in_specs=in_specs,
        out_specs=tok(d),
        out_shape=jax.ShapeDtypeStruct((bsz, seq, d), F32),
        scratch_shapes=[pltpu.VMEM((tm, d), F32)],
        compiler_params=_cparams(("arbitrary",) * 4),
        name="moe_ffn" if n_exp > 1 else "dense_ffn",
    )(*args)


def _ffn_kernel(*refs, n_exp, final_norm):
    refs = list(refs)
    h_ref, wg_ref, wu_ref, wd_ref, x1_ref, g2_ref = refs[:6]
    rest = refs[6:]
    gates_ref = rest.pop(0) if n_exp > 1 else None
    fg_ref = rest.pop(0) if final_norm else None
    o_ref, acc_ref = rest
    e = pl.program_id(2)
    j = pl.program_id(3)

    @pl.when(jnp.logical_and(e == 0, j == 0))
    def _():
        acc_ref[...] = jnp.zeros_like(acc_ref)

    h = h_ref[...]
    act = (jax.nn.silu(jnp.dot(h, wg_ref[...], preferred_element_type=F32))
           * jnp.dot(h, wu_ref[...], preferred_element_type=F32))
    y = jnp.dot(act.astype(BF16), wd_ref[...], preferred_element_type=F32)
    if n_exp > 1:
        gates = gates_ref[...]
        lane = lax.broadcasted_iota(jnp.int32, gates.shape, 1)
        y = y * jnp.sum(jnp.where(lane == e, gates, 0.0), axis=-1, keepdims=True)
    acc_ref[...] += y

    @pl.when(jnp.logical_and(e == n_exp - 1, j == pl.num_programs(3) - 1))
    def _():
        out = x1_ref[...] + g2_ref[...] * acc_ref[...]
        if final_norm:
            out = _rmsnorm(out, fg_ref[...])
        o_ref[...] = out


def kernel(x, c, positions, ada_w, ada_b, ln1_g, ln2_g, w_in, conv_w, conv_b, gate_a_w, gate_a_b, gate_x_w, gate_x_b, lru_lambda, lam_q1, lam_k1, lam_q2, lam_k2, subln_g, w_out, ffn_w_gate, ffn_w_up, ffn_w_down, moe_router, moe_w_gate, moe_w_up, moe_w_down, final_g):
    depth = ada_w.shape[0]
    lru_w = conv_w.shape[-1]
    vd = subln_g.shape[-1]
    dh = vd // 2
    attn_w = DIFF_HEADS * vd

    mod = _modulation(c, ada_w, ada_b)
    cos_t, sin_t = _rope_tables(positions, dh)
    for l in range(depth):
        lambda_init = 0.8 - 0.6 * math.exp(-0.3 * l)
        mod_l = mod[l]
        xy, q, k, v = _inproj(x, mod_l, ln1_g[l], w_in[l].astype(BF16), cos_t, sin_t,
                              lru_w=lru_w, attn_w=attn_w, dh=dh)
        lru = _lru(xy, conv_w[l], conv_b[l], gate_a_w[l], gate_a_b[l], gate_x_w[l],
                   gate_x_b[l], lru_lambda[l])
        att = _attention(q, k, v, positions, lam_q1[l], lam_k1[l], lam_q2[l], lam_k2[l],
                         subln_g[l], lambda_init, dh=dh)
        last = l == depth - 1
        j = l // 2
        if l % 2 == 0:
            x1, h2 = _outproj(lru, att, x, mod_l, ln2_g[l], w_out[l].astype(BF16))
            x = _ffn(h2, ffn_w_gate[j][None].astype(BF16), ffn_w_up[j][None].astype(BF16),
                     ffn_w_down[j][None].astype(BF16), x1, mod_l,
                     final_g=final_g if last else None)
        else:
            x1, h2, gates = _outproj(lru, att, x, mod_l, ln2_g[l], w_out[l].astype(BF16),
                                     router=moe_router[j])
            x = _ffn(h2, moe_w_gate[j].astype(BF16), moe_w_up[j].astype(BF16),
                     moe_w_down[j].astype(BF16), x1, mod_l, gates=gates,
                     final_g=final_g if last else None)
    return x
```

```python
import functools
import math

import jax
import jax.numpy as jnp
from jax import lax
from jax.experimental import pallas as pl
from jax.experimental.pallas import tpu as pltpu

F32 = jnp.float32
BF16 = jnp.bfloat16
HIGHEST = lax.Precision.HIGHEST

CHUNK = 64
LRU_BLOCKS = 8
CONV_W = 4
RG_C = 8.0
DIFF_HEADS = 4
ROPE_THETA = 10000.0
N_EXPERTS = 8
EPS = 1e-6
LANES = 128
SUBLANES = 8
VMEM_LIMIT = 56 * 1024 * 1024
MASK_VALUE = -0.5 * float(jnp.finfo(jnp.float32).max)


def _cparams(sem):
    return pltpu.CompilerParams(dimension_semantics=sem, vmem_limit_bytes=VMEM_LIMIT)


def _rmsnorm(x, g):
    return x * lax.rsqrt(jnp.mean(x * x, axis=-1, keepdims=True) + EPS) * g


def _mod_kernel(c_ref, w_ref, b_ref, o_ref):
    c = c_ref[...]
    s = c * jax.nn.sigmoid(c)
    o_ref[...] = jnp.dot(s, w_ref[...], precision=HIGHEST,
                         preferred_element_type=F32) + b_ref[...]


def _modulation(c, ada_w, ada_b, tn=1024):
    depth, d, n = ada_w.shape
    bsz = c.shape[0]
    rows = -(-bsz // SUBLANES) * SUBLANES
    c_pad = jnp.zeros((rows, d), F32).at[:bsz].set(c)
    out = pl.pallas_call(
        _mod_kernel,
        grid=(depth, n // tn),
        in_specs=[
            pl.BlockSpec((rows, d), lambda l, j: (0, 0)),
            pl.BlockSpec((None, d, tn), lambda l, j: (l, 0, j)),
            pl.BlockSpec((None, 1, tn), lambda l, j: (l, 0, j)),
        ],
        out_specs=pl.BlockSpec((None, rows, tn), lambda l, j: (l, 0, j)),
        out_shape=jax.ShapeDtypeStruct((depth, rows, n), F32),
        compiler_params=_cparams(("arbitrary", "arbitrary")),
        name="adaln_mod",
    )(c_pad, ada_w, ada_b.reshape(depth, 1, n))
    return out[:, :bsz].reshape(depth, bsz, 6, 1, d)


def _rope_table_kernel(pos_ref, inv_ref, cos_ref, sin_ref):
    ang = pos_ref[...].astype(F32) * inv_ref[...]
    cos_ref[...] = jnp.cos(ang)
    sin_ref[...] = jnp.sin(ang)


def _rope_tables(positions, dh):
    n_freq = dh // 2
    per_row = LANES // n_freq
    tok = positions.size
    rows = tok // per_row
    inv = ROPE_THETA ** (-jnp.arange(0, dh, 2, dtype=F32) / dh)
    pos_x = jnp.repeat(positions.reshape(-1), n_freq).reshape(rows, LANES)
    inv_x = jnp.tile(inv, per_row).reshape(1, LANES)
    tr = min(rows, 1024)
    cos, sin = pl.pallas_call(
        _rope_table_kernel,
        grid=(rows // tr,),
        in_specs=[pl.BlockSpec((tr, LANES), lambda i: (i, 0)),
                  pl.BlockSpec((1, LANES), lambda i: (0, 0))],
        out_specs=[pl.BlockSpec((tr, LANES), lambda i: (i, 0))] * 2,
        out_shape=[jax.ShapeDtypeStruct((rows, LANES), F32)] * 2,
        compiler_params=_cparams(("arbitrary",)),
        name="rope_tables",
    )(pos_x, inv_x)
    cos = cos.reshape(tok, n_freq)
    sin = sin.reshape(tok, n_freq)
    reps = LANES // dh
    cos_t = jnp.tile(jnp.concatenate([cos, cos], axis=-1), (1, reps))
    sin_t = jnp.tile(jnp.concatenate([-sin, sin], axis=-1), (1, reps))
    return cos_t, sin_t


def _inproj_kernel(x_ref, sc_ref, sh_ref, g_ref, w_ref, cos_ref, sin_ref,
                   xy_ref, q_ref, k_ref, v_ref, *, lru2, attn_w, dh):
    h = _rmsnorm(x_ref[...], g_ref[...]) * (1.0 + sc_ref[...]) + sh_ref[...]
    hb = h.astype(BF16)
    xy_ref[...] = jnp.dot(hb, w_ref[:, :lru2], preferred_element_type=F32)

    reps = attn_w // LANES
    cos = jnp.tile(cos_ref[...], (1, reps))
    sin = jnp.tile(sin_ref[...], (1, reps))
    lane = lax.broadcasted_iota(jnp.int32, cos.shape, 1)
    first_half = (lane % dh) < (dh // 2)

    def rope(t):
        fwd = pltpu.roll(t, attn_w - dh // 2, axis=1)
        bwd = pltpu.roll(t, dh // 2, axis=1)
        return t * cos + jnp.where(first_half, fwd, bwd) * sin

    q = jnp.dot(hb, w_ref[:, lru2:lru2 + attn_w], preferred_element_type=F32)
    q_ref[...] = (rope(q) * (dh ** -0.5 * math.log2(math.e))).astype(BF16)
    k = jnp.dot(hb, w_ref[:, lru2 + attn_w:lru2 + 2 * attn_w], preferred_element_type=F32)
    k_ref[...] = rope(k).astype(BF16)
    v = jnp.dot(hb, w_ref[:, lru2 + 2 * attn_w:], preferred_element_type=F32)
    v_ref[...] = v.astype(BF16)


def _inproj(x, mod_l, ln_g, w_in_b, cos_t, sin_t, *, lru_w, attn_w, dh, tm=512):
    bsz, seq, d = x.shape
    nt = seq // tm
    d_in = w_in_b.shape[1]
    lru2 = 2 * lru_w
    row = lambda k: pl.BlockSpec((None, None, 1, d), lambda b, i: (b, k, 0, 0))
    tok = lambda w: pl.BlockSpec((None, tm, w), lambda b, i: (b, i, 0))
    return pl.pallas_call(
        functools.partial(_inproj_kernel, lru2=lru2, attn_w=attn_w, dh=dh),
        grid=(bsz, nt),
        in_specs=[
            tok(d), row(1), row(0),
            pl.BlockSpec((1, d), lambda b, i: (0, 0)),
            pl.BlockSpec((d, d_in), lambda b, i: (0, 0)),
            pl.BlockSpec((tm, LANES), lambda b, i: (b * nt + i, 0)),
            pl.BlockSpec((tm, LANES), lambda b, i: (b * nt + i, 0)),
        ],
        out_specs=[tok(lru2), tok(attn_w), tok(attn_w), tok(attn_w)],
        out_shape=[jax.ShapeDtypeStruct((bsz, seq, lru2), F32),
                   jax.ShapeDtypeStruct((bsz, seq, attn_w), BF16),
                   jax.ShapeDtypeStruct((bsz, seq, attn_w), BF16),
                   jax.ShapeDtypeStruct((bsz, seq, attn_w), BF16)],
        compiler_params=_cparams(("arbitrary", "arbitrary")),
        name="inproj",
    )(x, mod_l, mod_l, ln_g.reshape(1, d), w_in_b, cos_t, sin_t)


def _gelu_tanh(x):
    return 0.5 * x * (1.0 + jnp.tanh(math.sqrt(2.0 / math.pi) * (x + 0.044715 * (x * x * x))))


def _lru_kernel(xy_ref, cw_ref, cb_ref, wg_ref, bg_ref, lam_ref, o_ref,
                xpad_ref, h_ref, *, t, w):
    @pl.when(pl.program_id(1) == 0)
    def _():
        xpad_ref[0:SUBLANES, :] = jnp.zeros((SUBLANES, w), F32)
        h_ref[...] = jnp.zeros_like(h_ref)

    xpad_ref[SUBLANES:SUBLANES + t, :] = xy_ref[:, :w]
    u = cb_ref[...]
    for j in range(CONV_W):
        off = SUBLANES - (CONV_W - 1) + j
        u = u + cw_ref[j:j + 1, :] * xpad_ref[off:off + t, :]
    xpad_ref[0:SUBLANES, :] = xpad_ref[t:t + SUBLANES, :]

    gates = jnp.dot(u.astype(BF16), wg_ref[...], preferred_element_type=F32) + bg_ref[...]
    r = jax.nn.sigmoid(gates[:, :w])
    ig = jax.nn.sigmoid(gates[:, w:])
    neg_lam = -lam_ref[...]
    softplus = jnp.maximum(neg_lam, 0.0) + jnp.log1p(jnp.exp(-jnp.abs(neg_lam)))
    log_a = (-RG_C) * r * softplus
    a = jnp.exp(log_a)
    bt = jnp.sqrt(1.0 - a * a) * (ig * u)

    row = lax.broadcasted_iota(jnp.int32, (t, w), 0)
    shift = 1
    while shift < t:
        keep = row >= shift
        a_prev = jnp.where(keep, pltpu.roll(a, shift, axis=0), 1.0)
        b_prev = jnp.where(keep, pltpu.roll(bt, shift, axis=0), 0.0)
        bt = a * b_prev + bt
        a = a * a_prev
        shift *= 2
    hs = a * h_ref[...] + bt
    h_ref[...] = hs[t - 1:t, :]
    o_ref[...] = (hs * _gelu_tanh(xy_ref[:, w:])).astype(BF16)


def _block_diag(wb):
    n, bw, _ = wb.shape
    eye = jnp.eye(n, dtype=wb.dtype)
    return jnp.einsum('nhk,nm->nhmk', wb, eye).reshape(n * bw, n * bw)


def _lru(xy, conv_w, conv_b, wa, ba, wx, bx, lam, *, t=256):
    bsz, seq, w2 = xy.shape
    w = w2 // 2
    wg = jnp.concatenate([_block_diag(wa), _block_diag(wx)], axis=1).astype(BF16)
    bg = jnp.concatenate([ba, bx]).reshape(1, 2 * w)
    const = lambda shape: pl.BlockSpec(shape, lambda b, i: (0,) * len(shape))
    return pl.pallas_call(
        functools.partial(_lru_kernel, t=t, w=w),
        grid=(bsz, seq // t),
        in_specs=[
            pl.BlockSpec((None, t, w2), lambda b, i: (b, i, 0)),
            const((CONV_W, w)), const((1, w)), const((w, 2 * w)), const((1, 2 * w)),
            const((1, w)),
        ],
        out_specs=pl.BlockSpec((None, t, w), lambda b, i: (b, i, 0)),
        out_shape=jax.ShapeDtypeStruct((bsz, seq, w), BF16),
        scratch_shapes=[pltpu.VMEM((t + SUBLANES, w), F32), pltpu.VMEM((1, w), F32)],
        compiler_params=_cparams(("arbitrary", "arbitrary")),
        name="rglru",
    )(xy, conv_w, conv_b.reshape(1, w), wg, bg, lam.reshape(1, w))


def _attn_kernel(qmin_ref, qmax_ref, kmin_ref, kmax_ref,
                 q_ref, k_ref, v_ref, cq_ref, ck_ref, lq1_ref, lk1_ref, lq2_ref, lk2_ref,
                 g_ref, o_ref, m_ref, l_ref, acc_ref, *, tq, tk, nk, dh, lambda_init):
    b = pl.program_id(0)
    i = pl.program_id(2)
    m_ref[...] = jnp.full(m_ref.shape, -jnp.inf, F32)
    l_ref[...] = jnp.zeros(l_ref.shape, F32)
    acc_ref[...] = jnp.zeros(acc_ref.shape, F32)

    q = q_ref[...]
    lane = lax.broadcasted_iota(jnp.int32, q.shape, 1)
    qc = (jnp.where(lane < dh, q, jnp.zeros_like(q)), jnp.where(lane >= dh, q, jnp.zeros_like(q)))
    q_lo = qmin_ref[b, i]
    q_hi = qmax_ref[b, i]

    def process(j, masked):
        start = pl.multiple_of(j * tk, tk)
        kb = k_ref[pl.ds(start, tk), :]
        vb = v_ref[pl.ds(start, tk), :]
        if masked:
            ck = ck_ref[:, pl.ds(start, tk)]
            visible = ck <= jnp.tile(cq_ref[...], (1, tk // LANES))
        for c in range(2):
            s = lax.dot_general(qc[c], kb, (((1,), (1,)), ((), ())),
                                preferred_element_type=F32)
            if masked:
                s = jnp.where(visible, s, MASK_VALUE)
            m_prev = m_ref[c]
            m_new = jnp.maximum(m_prev, jnp.max(s, axis=-1, keepdims=True))
            alpha = jnp.exp2(m_prev - m_new)
            p = jnp.exp2(s - pltpu.repeat(m_new, tk // LANES, axis=1))
            p_lanes = p[:, :LANES]
            for t in range(1, tk // LANES):
                p_lanes = p_lanes + p[:, t * LANES:(t + 1) * LANES]
            l_ref[c] = alpha * l_ref[c] + p_lanes
            acc_ref[c] = alpha * acc_ref[c] + jnp.dot(p.astype(BF16), vb,
                                                      preferred_element_type=F32)
            m_ref[c] = m_new

    def body(j, carry):
        k_lo = kmin_ref[b, j]
        k_hi = kmax_ref[b, j]
        needed = k_lo <= q_hi
        needs_mask = k_hi > q_lo

        @pl.when(jnp.logical_and(needed, needs_mask))
        def _():
            process(j, True)

        @pl.when(jnp.logical_and(needed, jnp.logical_not(needs_mask)))
        def _():
            process(j, False)

        return carry

    lax.fori_loop(0, nk, body, 0)

    lam = (jnp.exp(jnp.sum(lq1_ref[...] * lk1_ref[...], keepdims=True))
           - jnp.exp(jnp.sum(lq2_ref[...] * lk2_ref[...], keepdims=True)) + lambda_init)
    l0 = jnp.sum(l_ref[0], axis=-1, keepdims=True)
    l1 = jnp.sum(l_ref[1], axis=-1, keepdims=True)
    o = acc_ref[0] / l0 - lam * (acc_ref[1] / l1)
    o_ref[...] = (_rmsnorm(o, g_ref[...]) * (1.0 - lambda_init)).astype(o_ref.dtype)


def _attention(q, k, v, positions, lq1, lk1, lq2, lk2, subln_g, lambda_init, *,
               dh, tq=1024, tk=1024):
    bsz, seq, aw = q.shape
    vd = 2 * dh
    heads = aw // vd
    nq, nk = seq // tq, seq // tk
    chunk = positions // CHUNK
    qmin = chunk.reshape(bsz, nq, tq).min(-1)
    qmax = chunk.reshape(bsz, nq, tq).max(-1)
    kmin = chunk.reshape(bsz, nk, tk).min(-1)
    kmax = chunk.reshape(bsz, nk, tk).max(-1)
    cq = jnp.broadcast_to(chunk[:, :, None], (bsz, seq, LANES))
    ck = chunk.reshape(bsz, 1, seq)
    vec = lambda n: pl.BlockSpec((1, n), lambda b, h, i, *_: (0, 0))
    grid_spec = pltpu.PrefetchScalarGridSpec(
        num_scalar_prefetch=4,
        grid=(bsz, heads, nq),
        in_specs=[
            pl.BlockSpec((None, tq, vd), lambda b, h, i, *_: (b, i, h)),
            pl.BlockSpec((None, seq, vd), lambda b, h, i, *_: (b, 0, h)),
            pl.BlockSpec((None, seq, vd), lambda b, h, i, *_: (b, 0, h)),
            pl.BlockSpec((None, tq, LANES), lambda b, h, i, *_: (b, i, 0)),
            pl.BlockSpec((None, 1, seq), lambda b, h, i, *_: (b, 0, 0)),
            vec(dh), vec(dh), vec(dh), vec(dh), vec(vd),
        ],
        out_specs=pl.BlockSpec((None, tq, vd), lambda b, h, i, *_: (b, i, h)),
        scratch_shapes=[pltpu.VMEM((2, tq, LANES), F32), pltpu.VMEM((2, tq, LANES), F32),
                        pltpu.VMEM((2, tq, vd), F32)],
    )
    return pl.pallas_call(
        functools.partial(_attn_kernel, tq=tq, tk=tk, nk=nk, dh=dh, lambda_init=lambda_init),
        grid_spec=grid_spec,
        out_shape=jax.ShapeDtypeStruct((bsz, seq, aw), BF16),
        compiler_params=_cparams(("arbitrary", "arbitrary", "arbitrary")),
        name="diff_attn",
    )(qmin, qmax, kmin, kmax, q, k, v, cq, ck,
      lq1.reshape(1, dh), lk1.reshape(1, dh), lq2.reshape(1, dh), lk2.reshape(1, dh),
      subln_g.reshape(1, vd))


def _outproj_kernel(*refs, w, with_router):
    if with_router:
        (lru_ref, att_ref, x_ref, g1_ref, sc_ref, sh_ref, ln_ref, wo_ref, rt_ref,
         x1_ref, h2_ref, route_ref, cend_ref, carry_ref) = refs
    else:
        (lru_ref, att_ref, x_ref, g1_ref, sc_ref, sh_ref, ln_ref, wo_ref,
         x1_ref, h2_ref) = refs
    y = (jnp.dot(lru_ref[...], wo_ref[:w, :], preferred_element_type=F32)
         + jnp.dot(att_ref[...], wo_ref[w:, :], preferred_element_type=F32))
    x1 = x_ref[...] + g1_ref[...] * y
    x1_ref[...] = x1
    h2 = _rmsnorm(x1, ln_ref[...]) * (1.0 + sc_ref[...]) + sh_ref[...]
    h2_ref[...] = h2.astype(BF16)
    if with_router:
        logits = jnp.dot(h2, rt_ref[...], precision=HIGHEST, preferred_element_type=F32)
        lane = lax.broadcasted_iota(jnp.int32, logits.shape, 1)
        lg = jnp.where(lane < N_EXPERTS, logits, -jnp.inf)
        m1 = jnp.max(lg, axis=-1, keepdims=True)
        i1 = jnp.min(jnp.where(lg == m1, lane, LANES), axis=-1, keepdims=True)
        lg2 = jnp.where(lane == i1, -jnp.inf, lg)
        m2 = jnp.max(lg2, axis=-1, keepdims=True)
        i2 = jnp.min(jnp.where(lg2 == m2, lane, LANES), axis=-1, keepdims=True)
        e2 = jnp.exp(m2 - m1)
        w1 = 1.0 / (1.0 + e2)
        w2 = e2 / (1.0 + e2)

        @pl.when(jnp.logical_and(pl.program_id(0) == 0, pl.program_id(1) == 0))
        def _():
            carry_ref[...] = jnp.zeros_like(carry_ref)

        tm = logits.shape[0]
        onehot = jnp.where(lane == i1, 1.0, jnp.where(lane == i2, 1.0, 0.0))
        tri = (lax.broadcasted_iota(jnp.int32, (tm, tm), 0)
               > lax.broadcasted_iota(jnp.int32, (tm, tm), 1))
        prefix = jnp.dot(jnp.where(tri, 1.0, 0.0).astype(BF16), onehot.astype(BF16),
                         preferred_element_type=F32) + carry_ref[...]
        rank1 = jnp.sum(jnp.where(lane == i1, prefix, 0.0), axis=-1, keepdims=True)
        rank2 = jnp.sum(jnp.where(lane == i2, prefix, 0.0), axis=-1, keepdims=True)
        carry = carry_ref[...] + jnp.sum(onehot, axis=0, keepdims=True)
        carry_ref[...] = carry
        cend_ref[...] = jnp.broadcast_to(carry, cend_ref.shape)
        cols = (i1.astype(F32), i2.astype(F32), rank1, rank2, w1, w2)
        route = jnp.zeros(logits.shape, F32)
        for n, col in enumerate(cols):
            route = jnp.where(lane == n, col, route)
        route_ref[...] = route


def _outproj(lru, att, x, mod_l, ln_g, w_out_b, router=None, *, tm=512):
    bsz, seq, d = x.shape
    w = lru.shape[-1]
    with_router = router is not None
    row = lambda k: pl.BlockSpec((None, None, 1, d), lambda b, i: (b, k, 0, 0))
    tok = lambda n: pl.BlockSpec((None, tm, n), lambda b, i: (b, i, 0))
    in_specs = [tok(w), tok(w), tok(d), row(2), row(4), row(3),
                pl.BlockSpec((1, d), lambda b, i: (0, 0)),
                pl.BlockSpec((d, d), lambda b, i: (0, 0))]
    args = [lru, att, x, mod_l, mod_l, mod_l, ln_g.reshape(1, d), w_out_b]
    out_specs = [tok(d), tok(d)]
    out_shape = [jax.ShapeDtypeStruct((bsz, seq, d), F32),
                 jax.ShapeDtypeStruct((bsz, seq, d), BF16)]
    scratch = []
    if with_router:
        nt = seq // tm
        rt = jnp.zeros((d, LANES), F32).at[:, :N_EXPERTS].set(router)
        in_specs.append(pl.BlockSpec((d, LANES), lambda b, i: (0, 0)))
        args.append(rt)
        out_specs.append(tok(LANES))
        out_shape.append(jax.ShapeDtypeStruct((bsz, seq, LANES), F32))
        out_specs.append(pl.BlockSpec((SUBLANES, LANES), lambda b, i: (b * nt + i, 0)))
        out_shape.append(jax.ShapeDtypeStruct((bsz * nt * SUBLANES, LANES), F32))
        scratch.append(pltpu.VMEM((1, LANES), F32))
    return pl.pallas_call(
        functools.partial(_outproj_kernel, w=w, with_router=with_router),
        grid=(bsz, seq // tm),
        in_specs=in_specs, out_specs=out_specs, out_shape=out_shape,
        scratch_shapes=scratch,
        compiler_params=_cparams(("arbitrary", "arbitrary")),
        name="outproj_router" if with_router else "outproj",
    )(*args)


def _ffn_kernel(*refs, final_norm):
    if final_norm:
        h_ref, wg_ref, wu_ref, wd_ref, x1_ref, g2_ref, fg_ref, o_ref, acc_ref = refs
    else:
        h_ref, wg_ref, wu_ref, wd_ref, x1_ref, g2_ref, o_ref, acc_ref = refs
    j = pl.program_id(2)

    @pl.when(j == 0)
    def _():
        acc_ref[...] = jnp.zeros_like(acc_ref)

    h = h_ref[...]
    act = (jax.nn.silu(jnp.dot(h, wg_ref[...], preferred_element_type=F32))
           * jnp.dot(h, wu_ref[...], preferred_element_type=F32))
    acc_ref[...] += jnp.dot(act.astype(BF16), wd_ref[...], preferred_element_type=F32)

    @pl.when(j == pl.num_programs(2) - 1)
    def _():
        out = x1_ref[...] + g2_ref[...] * acc_ref[...]
        if final_norm:
            out = _rmsnorm(out, fg_ref[...])
        o_ref[...] = out


def _ffn(h2, wg, wu, wd, x1, mod_l, final_g=None, *, tm=1024, tf=512):
    bsz, seq, d = x1.shape
    ff = wg.shape[1]
    final_norm = final_g is not None
    tok = lambda n: pl.BlockSpec((None, tm, n), lambda b, i, j: (b, i, 0))
    in_specs = [tok(d),
                pl.BlockSpec((d, tf), lambda b, i, j: (0, j)),
                pl.BlockSpec((d, tf), lambda b, i, j: (0, j)),
                pl.BlockSpec((tf, d), lambda b, i, j: (j, 0)),
                tok(d),
                pl.BlockSpec((None, None, 1, d), lambda b, i, j: (b, 5, 0, 0))]
    args = [h2, wg, wu, wd, x1, mod_l]
    if final_norm:
        in_specs.append(pl.BlockSpec((1, d), lambda b, i, j: (0, 0)))
        args.append(final_g.reshape(1, d))
    return pl.pallas_call(
        functools.partial(_ffn_kernel, final_norm=final_norm),
        grid=(bsz, seq // tm, ff // tf),
        in_specs=in_specs,
        out_specs=tok(d),
        out_shape=jax.ShapeDtypeStruct((bsz, seq, d), F32),
        scratch_shapes=[pltpu.VMEM((tm, d), F32)],
        compiler_params=_cparams(("arbitrary",) * 3),
        name="dense_ffn",
    )(*args)


MOE_TB = 512
MOE_TG = 256
MOE_TF = 512


def _moe_plan(route, cend, *, m_tok):
    i32 = jnp.int32
    nb = m_tok // MOE_TB
    n_e = N_EXPERTS
    ntg = (2 * m_tok + n_e * MOE_TF) // MOE_TG
    ntf = ntg * MOE_TG // MOE_TF
    pmax = ntg + n_e * nb
    e1 = route[:, 0].astype(i32)
    e2 = route[:, 1].astype(i32)
    r1 = route[:, 2].astype(i32)
    r2 = route[:, 3].astype(i32)
    cend8 = cend.reshape(nb, SUBLANES, LANES)[:, 0, :n_e].astype(i32)
    first = jnp.concatenate([jnp.zeros((1, n_e), i32), cend8[:-1]])
    cnt = cend8[-1]
    gsz = (cnt + MOE_TF - 1) // MOE_TF * MOE_TF
    gend = jnp.cumsum(gsz)
    goff = gend - gsz
    total = gend[-1]
    d1 = goff[e1] + r1
    d2 = goff[e2] + r2

    row0 = jnp.arange(ntg, dtype=i32) * MOE_TG
    tvalid = row0 < total
    te = jnp.minimum(jnp.searchsorted(gend, row0, side='right').astype(i32), n_e - 1)
    ra = row0 - goff[te]
    rb = jnp.minimum(ra + MOE_TG, cnt[te])
    nonempty = jnp.logical_and(tvalid, rb > ra)
    lo = jnp.sum(cend8[:, te] <= ra[None, :], axis=0).astype(i32)
    hi = jnp.sum(first[:, te] <= (rb - 1)[None, :], axis=0).astype(i32) - 1
    lo = jnp.where(nonempty, lo, 0)
    hi = jnp.where(nonempty, hi, 0)
    npairs = jnp.where(tvalid, hi - lo + 1, 0)
    pend = jnp.cumsum(npairs)
    pstart = pend - npairs
    n_pairs = pend[-1]
    p = jnp.arange(pmax, dtype=i32)
    pc = jnp.minimum(p, n_pairs - 1)
    g_tile = jnp.minimum(jnp.searchsorted(pend, pc, side='right').astype(i32), ntg - 1)
    g_tb = lo[g_tile] + pc - pstart[g_tile]
    pvalid = p < n_pairs
    g_flag = jnp.where(pvalid, 4 + (pc == pstart[g_tile]) + 2 * (pc == pend[g_tile] - 1), 0)

    has = cend8 > first
    tlo = (goff[None, :] + first) // MOE_TG
    thi = (goff[None, :] + cend8 - 1) // MOE_TG
    cpairs = jnp.where(has, thi - tlo + 1, 0).reshape(-1)
    tlo = tlo.reshape(-1)
    cend_p = jnp.cumsum(cpairs)
    cstart_p = cend_p - cpairs
    qidx = jnp.minimum(jnp.searchsorted(cend_p, pc, side='right').astype(i32), nb * n_e - 1)
    c_tile = tlo[qidx] + pc - cstart_p[qidx]
    c_tb = qidx // n_e
    tb_start = cstart_p[c_tb * n_e]
    tb_end = cend_p[c_tb * n_e + n_e - 1]
    c_flag = jnp.where(pvalid, 4 + (pc == tb_start) + 2 * (pc == tb_end - 1), 0)

    frow0 = jnp.arange(ntf, dtype=i32) * MOE_TF
    f_valid = (frow0 < total).astype(i32)
    f_idx = jnp.minimum(jnp.arange(ntf, dtype=i32), total // MOE_TF - 1)
    f_exp = jnp.minimum(jnp.searchsorted(gend, f_idx * MOE_TF, side='right').astype(i32),
                        n_e - 1)
    return dict(d1=d1, d2=d2, g_tile=g_tile, g_tb=g_tb, g_flag=g_flag.astype(i32),
                c_tile=c_tile, c_tb=c_tb, c_flag=c_flag.astype(i32),
                f_idx=f_idx, f_exp=f_exp, f_valid=f_valid, ntg=ntg, ntf=ntf, pmax=pmax)


def _moe_gather_kernel(tile_ref, tb_ref, flag_ref, h_ref, d1_ref, d2_ref, w1_ref, w2_ref,
                       xs_ref, gs_ref, acc_ref, gacc_ref):
    p = pl.program_id(0)
    flag = flag_ref[p]

    @pl.when((flag & 1) != 0)
    def _():
        acc_ref[...] = jnp.zeros_like(acc_ref)
        gacc_ref[...] = jnp.zeros_like(gacc_ref)

    @pl.when((flag & 4) != 0)
    def _():
        rows = tile_ref[p] * MOE_TG + lax.broadcasted_iota(jnp.int32, (MOE_TG, MOE_TB), 0)
        m1 = d1_ref[...] == rows
        m2 = d2_ref[...] == rows
        sel = jnp.where(m1, 1.0, jnp.where(m2, 1.0, 0.0)).astype(BF16)
        acc_ref[...] += jnp.dot(sel, h_ref[...], preferred_element_type=F32)
        gate = jnp.where(m1, w1_ref[...], jnp.where(m2, w2_ref[...], 0.0))
        gacc_ref[...] += jnp.sum(gate, axis=-1, keepdims=True)

    @pl.when((flag & 2) != 0)
    def _():
        xs_ref[...] = acc_ref[...].astype(BF16)
        gs_ref[...] = gacc_ref[...]


def _moe_gather(h2, plan, w1, w2):
    m_tok, d = h2.shape
    nb = m_tok // MOE_TB
    rows = plan['ntg'] * MOE_TG
    tokrow = lambda: pl.BlockSpec((None, 1, MOE_TB), lambda p, t, b, f: (b[p], 0, 0))
    as_rows = lambda a: a.reshape(nb, 1, MOE_TB)
    grid_spec = pltpu.PrefetchScalarGridSpec(
        num_scalar_prefetch=3,
        grid=(plan['pmax'],),
        in_specs=[pl.BlockSpec((MOE_TB, d), lambda p, t, b, f: (b[p], 0)),
                  tokrow(), tokrow(), tokrow(), tokrow()],
        out_specs=[pl.BlockSpec((MOE_TG, d), lambda p, t, b, f: (t[p], 0)),
                   pl.BlockSpec((MOE_TG, LANES), lambda p, t, b, f: (t[p], 0))],
        scratch_shapes=[pltpu.VMEM((MOE_TG, d), F32), pltpu.VMEM((MOE_TG, LANES), F32)],
    )
    return pl.pallas_call(
        _moe_gather_kernel,
        grid_spec=grid_spec,
        out_shape=[jax.ShapeDtypeStruct((rows, d), BF16),
                   jax.ShapeDtypeStruct((rows, LANES), F32)],
        compiler_params=_cparams(("arbitrary",)),
        name="moe_gather",
    )(plan['g_tile'], plan['g_tb'], plan['g_flag'], h2,
      as_rows(plan['d1']), as_rows(plan['d2']), as_rows(w1), as_rows(w2))


def _moe_ffn_kernel(idx_ref, exp_ref, valid_ref, xs_ref, wg_ref, wu_ref, wd_ref, gs_ref,
                    o_ref, acc_ref):
    n = pl.program_id(0)
    j = pl.program_id(1)

    @pl.when(valid_ref[n] != 0)
    def _():
        @pl.when(j == 0)
        def _():
            acc_ref[...] = jnp.zeros_like(acc_ref)

        h = xs_ref[...]
        act = (jax.nn.silu(jnp.dot(h, wg_ref[...], preferred_element_type=F32))
               * jnp.dot(h, wu_ref[...], preferred_element_type=F32))
        acc_ref[...] += jnp.dot(act.astype(BF16), wd_ref[...], preferred_element_type=F32)

        @pl.when(j == pl.num_programs(1) - 1)
        def _():
            gate = pltpu.repeat(gs_ref[...], acc_ref.shape[1] // LANES, axis=1)
            o_ref[...] = (gate * acc_ref[...]).astype(o_ref.dtype)


def _moe_ffn(xs, gs, wg, wu, wd, plan, *, tf=512):
    rows, d = xs.shape
    ff = wg.shape[2]
    nj = ff // tf

    def ff_tile(n, j, v):
        return j * v[n] + (nj - 1) * (1 - v[n])

    grid_spec = pltpu.PrefetchScalarGridSpec(
        num_scalar_prefetch=3,
        grid=(plan['ntf'], nj),
        in_specs=[pl.BlockSpec((MOE_TF, d), lambda n, j, i, e, v: (i[n], 0)),
                  pl.BlockSpec((None, d, tf), lambda n, j, i, e, v: (e[n], 0, ff_tile(n, j, v))),
                  pl.BlockSpec((None, d, tf), lambda n, j, i, e, v: (e[n], 0, ff_tile(n, j, v))),
                  pl.BlockSpec((None, tf, d), lambda n, j, i, e, v: (e[n], ff_tile(n, j, v), 0)),
                  pl.BlockSpec((MOE_TF, LANES), lambda n, j, i, e, v: (i[n], 0))],
        out_specs=pl.BlockSpec((MOE_TF, d), lambda n, j, i, e, v: (i[n], 0)),
        scratch_shapes=[pltpu.VMEM((MOE_TF, d), F32)],
    )
    return pl.pallas_call(
        _moe_ffn_kernel,
        grid_spec=grid_spec,
        out_shape=jax.ShapeDtypeStruct((rows, d), BF16),
        compiler_params=_cparams(("arbitrary", "arbitrary")),
        name="moe_ffn",
    )(plan['f_idx'], plan['f_exp'], plan['f_valid'], xs, wg, wu, wd, gs)


def _moe_combine_kernel(*refs, final_norm):
    if final_norm:
        (tile_ref, tb_ref, flag_ref, ye_ref, d1_ref, d2_ref, x1_ref, g2_ref, fg_ref,
         o_ref, acc_ref) = refs
    else:
        (tile_ref, tb_ref, flag_ref, ye_ref, d1_ref, d2_ref, x1_ref, g2_ref,
         o_ref, acc_ref) = refs
    p = pl.program_id(0)
    flag = flag_ref[p]

    @pl.when((flag & 1) != 0)
    def _():
        acc_ref[...] = jnp.zeros_like(acc_ref)

    @pl.when((flag & 4) != 0)
    def _():
        cols = tile_ref[p] * MOE_TG + lax.broadcasted_iota(jnp.int32, (MOE_TB, MOE_TG), 1)
        d1 = pltpu.repeat(d1_ref[...], MOE_TG // LANES, axis=1)
        d2 = pltpu.repeat(d2_ref[...], MOE_TG // LANES, axis=1)
        sel = jnp.where(d1 == cols, 1.0, jnp.where(d2 == cols, 1.0, 0.0)).astype(BF16)
        acc_ref[...] += jnp.dot(sel, ye_ref[...], preferred_element_type=F32)

    @pl.when((flag & 2) != 0)
    def _():
        out = x1_ref[...] + g2_ref[...] * acc_ref[...]
        if final_norm:
            out = _rmsnorm(out, fg_ref[...])
        o_ref[...] = out


def _moe_combine(ye, plan, x1, mod_l, final_g=None):
    bsz, seq, d = x1.shape
    m_tok = bsz * seq
    blocks_per_seq = seq // MOE_TB
    final_norm = final_g is not None
    rep = lambda a: jnp.broadcast_to(a[:, None], (m_tok, LANES))
    in_specs = [pl.BlockSpec((MOE_TG, d), lambda p, t, b, f: (t[p], 0)),
                pl.BlockSpec((MOE_TB, LANES), lambda p, t, b, f: (b[p], 0)),
                pl.BlockSpec((MOE_TB, LANES), lambda p, t, b, f: (b[p], 0)),
                pl.BlockSpec((MOE_TB, d), lambda p, t, b, f: (b[p], 0)),
                pl.BlockSpec((None, None, 1, d),
                             lambda p, t, b, f: (b[p] // blocks_per_seq, 5, 0, 0))]
    args = [ye, rep(plan['d1']), rep(plan['d2']), x1.reshape(m_tok, d), mod_l]
    if final_norm:
        in_specs.append(pl.BlockSpec((1, d), lambda p, t, b, f: (0, 0)))
        args.append(final_g.reshape(1, d))
    grid_spec = pltpu.PrefetchScalarGridSpec(
        num_scalar_prefetch=3,
        grid=(plan['pmax'],),
        in_specs=in_specs,
        out_specs=pl.BlockSpec((MOE_TB, d), lambda p, t, b, f: (b[p], 0)),
        scratch_shapes=[pltpu.VMEM((MOE_TB, d), F32)],
    )
    out = pl.pallas_call(
        functools.partial(_moe_combine_kernel, final_norm=final_norm),
        grid_spec=grid_spec,
        out_shape=jax.ShapeDtypeStruct((m_tok, d), F32),
        compiler_params=_cparams(("arbitrary",)),
        name="moe_combine",
    )(plan['c_tile'], plan['c_tb'], plan['c_flag'], *args)
    return out.reshape(bsz, seq, d)


def _moe(h2, route, cend, x1, mod_l, wg, wu, wd, final_g=None):
    bsz, seq, d = x1.shape
    m_tok = bsz * seq
    route = route.reshape(m_tok, LANES)
    plan = _moe_plan(route, cend, m_tok=m_tok)
    xs, gs = _moe_gather(h2.reshape(m_tok, d), plan, route[:, 4], route[:, 5])
    ye = _moe_ffn(xs, gs, wg, wu, wd, plan)
    return _moe_combine(ye, plan, x1, mod_l, final_g)


def kernel(x, c, positions, ada_w, ada_b, ln1_g, ln2_g, w_in, conv_w, conv_b, gate_a_w, gate_a_b, gate_x_w, gate_x_b, lru_lambda, lam_q1, lam_k1, lam_q2, lam_k2, subln_g, w_out, ffn_w_gate, ffn_w_up, ffn_w_down, moe_router, moe_w_gate, moe_w_up, moe_w_down, final_g):
    depth = ada_w.shape[0]
    lru_w = conv_w.shape[-1]
    vd = subln_g.shape[-1]
    dh = vd // 2
    attn_w = DIFF_HEADS * vd

    mod = _modulation(c, ada_w, ada_b)
    cos_t, sin_t = _rope_tables(positions, dh)
    for l in range(depth):
        lambda_init = 0.8 - 0.6 * math.exp(-0.3 * l)
        mod_l = mod[l]
        xy, q, k, v = _inproj(x, mod_l, ln1_g[l], w_in[l].astype(BF16), cos_t, sin_t,
                              lru_w=lru_w, attn_w=attn_w, dh=dh)
        lru = _lru(xy, conv_w[l], conv_b[l], gate_a_w[l], gate_a_b[l], gate_x_w[l],
                   gate_x_b[l], lru_lambda[l])
        att = _attention(q, k, v, positions, lam_q1[l], lam_k1[l], lam_q2[l], lam_k2[l],
                         subln_g[l], lambda_init, dh=dh)
        fg = final_g if l == depth - 1 else None
        j = l // 2
        if l % 2 == 0:
            x1, h2 = _outproj(lru, att, x, mod_l, ln2_g[l], w_out[l].astype(BF16))
            x = _ffn(h2, ffn_w_gate[j].astype(BF16), ffn_w_up[j].astype(BF16),
                     ffn_w_down[j].astype(BF16), x1, mod_l, final_g=fg)
        else:
            x1, h2, route, cend = _outproj(lru, att, x, mod_l, ln2_g[l], w_out[l].astype(BF16),
                                           router=moe_router[j])
            x = _moe(h2, route, cend, x1, mod_l, moe_w_gate[j].astype(BF16),
                     moe_w_up[j].astype(BF16), moe_w_down[j].astype(BF16), final_g=fg)
    return x
```

```python
import functools
import math

import jax
import jax.numpy as jnp
from jax import lax
from jax.experimental import pallas as pl
from jax.experimental.pallas import tpu as pltpu

F32 = jnp.float32
BF16 = jnp.bfloat16
HIGHEST = lax.Precision.HIGHEST

CHUNK = 64
LRU_BLOCKS = 8
CONV_W = 4
RG_C = 8.0
DIFF_HEADS = 4
ROPE_THETA = 10000.0
N_EXPERTS = 8
EPS = 1e-6
LANES = 128
SUBLANES = 8
VMEM_LIMIT = 56 * 1024 * 1024
MASK_VALUE = -0.5 * float(jnp.finfo(jnp.float32).max)


def _cparams(sem):
    return pltpu.CompilerParams(dimension_semantics=sem, vmem_limit_bytes=VMEM_LIMIT)


def _rmsnorm(x, g):
    return x * lax.rsqrt(jnp.mean(x * x, axis=-1, keepdims=True) + EPS) * g


def _mod_kernel(c_ref, w_ref, b_ref, o_ref):
    c = c_ref[...]
    s = c * jax.nn.sigmoid(c)
    o_ref[...] = jnp.dot(s, w_ref[...], precision=HIGHEST,
                         preferred_element_type=F32) + b_ref[...]


def _modulation(c, ada_w, ada_b, tn=1024):
    depth, d, n = ada_w.shape
    bsz = c.shape[0]
    rows = -(-bsz // SUBLANES) * SUBLANES
    c_pad = jnp.zeros((rows, d), F32).at[:bsz].set(c)
    out = pl.pallas_call(
        _mod_kernel,
        grid=(depth, n // tn),
        in_specs=[
            pl.BlockSpec((rows, d), lambda l, j: (0, 0)),
            pl.BlockSpec((None, d, tn), lambda l, j: (l, 0, j)),
            pl.BlockSpec((None, 1, tn), lambda l, j: (l, 0, j)),
        ],
        out_specs=pl.BlockSpec((None, rows, tn), lambda l, j: (l, 0, j)),
        out_shape=jax.ShapeDtypeStruct((depth, rows, n), F32),
        compiler_params=_cparams(("arbitrary", "arbitrary")),
        name="adaln_mod",
    )(c_pad, ada_w, ada_b.reshape(depth, 1, n))
    return out[:, :bsz].reshape(depth, bsz, 6, 1, d)


def _rope_table_kernel(pos_ref, inv_ref, cos_ref, sin_ref):
    ang = pos_ref[...].astype(F32) * inv_ref[...]
    cos_ref[...] = jnp.cos(ang)
    sin_ref[...] = jnp.sin(ang)


def _rope_tables(positions, dh):
    n_freq = dh // 2
    per_row = LANES // n_freq
    tok = positions.size
    rows = tok // per_row
    inv = ROPE_THETA ** (-jnp.arange(0, dh, 2, dtype=F32) / dh)
    pos_x = jnp.repeat(positions.reshape(-1), n_freq).reshape(rows, LANES)
    inv_x = jnp.tile(inv, per_row).reshape(1, LANES)
    tr = min(rows, 1024)
    cos, sin = pl.pallas_call(
        _rope_table_kernel,
        grid=(rows // tr,),
        in_specs=[pl.BlockSpec((tr, LANES), lambda i: (i, 0)),
                  pl.BlockSpec((1, LANES), lambda i: (0, 0))],
        out_specs=[pl.BlockSpec((tr, LANES), lambda i: (i, 0))] * 2,
        out_shape=[jax.ShapeDtypeStruct((rows, LANES), F32)] * 2,
        compiler_params=_cparams(("arbitrary",)),
        name="rope_tables",
    )(pos_x, inv_x)
    cos = cos.reshape(tok, n_freq)
    sin = sin.reshape(tok, n_freq)
    reps = LANES // dh
    cos_t = jnp.tile(jnp.concatenate([cos, cos], axis=-1), (1, reps))
    sin_t = jnp.tile(jnp.concatenate([-sin, sin], axis=-1), (1, reps))
    return cos_t, sin_t


def _inproj_kernel(x_ref, sc_ref, sh_ref, g_ref, w_ref, cos_ref, sin_ref,
                   xy_ref, q_ref, k_ref, v_ref, *, lru2, attn_w, dh):
    h = _rmsnorm(x_ref[...], g_ref[...]) * (1.0 + sc_ref[...]) + sh_ref[...]
    hb = h.astype(BF16)
    xy_ref[...] = jnp.dot(hb, w_ref[:, :lru2], preferred_element_type=F32)

    reps = attn_w // LANES
    cos = jnp.tile(cos_ref[...], (1, reps))
    sin = jnp.tile(sin_ref[...], (1, reps))
    lane = lax.broadcasted_iota(jnp.int32, cos.shape, 1)
    first_half = (lane % dh) < (dh // 2)

    def rope(t):
        fwd = pltpu.roll(t, attn_w - dh // 2, axis=1)
        bwd = pltpu.roll(t, dh // 2, axis=1)
        return t * cos + jnp.where(first_half, fwd, bwd) * sin

    q = jnp.dot(hb, w_ref[:, lru2:lru2 + attn_w], preferred_element_type=F32)
    q_ref[...] = (rope(q) * (dh ** -0.5 * math.log2(math.e))).astype(BF16)
    k = jnp.dot(hb, w_ref[:, lru2 + attn_w:lru2 + 2 * attn_w], preferred_element_type=F32)
    k_ref[...] = rope(k).astype(BF16)
    v = jnp.dot(hb, w_ref[:, lru2 + 2 * attn_w:], preferred_element_type=F32)
    v_ref[...] = v.astype(BF16)


def _inproj(x, mod_l, ln_g, w_in_b, cos_t, sin_t, *, lru_w, attn_w, dh, tm=512):
    bsz, seq, d = x.shape
    nt = seq // tm
    d_in = w_in_b.shape[1]
    lru2 = 2 * lru_w
    row = lambda k: pl.BlockSpec((None, None, 1, d), lambda b, i: (b, k, 0, 0))
    tok = lambda w: pl.BlockSpec((None, tm, w), lambda b, i: (b, i, 0))
    return pl.pallas_call(
        functools.partial(_inproj_kernel, lru2=lru2, attn_w=attn_w, dh=dh),
        grid=(bsz, nt),
        in_specs=[
            tok(d), row(1), row(0),
            pl.BlockSpec((1, d), lambda b, i: (0, 0)),
            pl.BlockSpec((d, d_in), lambda b, i: (0, 0)),
            pl.BlockSpec((tm, LANES), lambda b, i: (b * nt + i, 0)),
            pl.BlockSpec((tm, LANES), lambda b, i: (b * nt + i, 0)),
        ],
        out_specs=[tok(lru2), tok(attn_w), tok(attn_w), tok(attn_w)],
        out_shape=[jax.ShapeDtypeStruct((bsz, seq, lru2), F32),
                   jax.ShapeDtypeStruct((bsz, seq, attn_w), BF16),
                   jax.ShapeDtypeStruct((bsz, seq, attn_w), BF16),
                   jax.ShapeDtypeStruct((bsz, seq, attn_w), BF16)],
        compiler_params=_cparams(("arbitrary", "arbitrary")),
        name="inproj",
    )(x, mod_l, mod_l, ln_g.reshape(1, d), w_in_b, cos_t, sin_t)


def _gelu_tanh(x):
    return 0.5 * x * (1.0 + jnp.tanh(math.sqrt(2.0 / math.pi) * (x + 0.044715 * (x * x * x))))


def _lru_kernel(xy_ref, cw_ref, cb_ref, wg_ref, bg_ref, lam_ref, o_ref,
                xpad_ref, h_ref, *, t, w):
    @pl.when(pl.program_id(1) == 0)
    def _():
        xpad_ref[0:SUBLANES, :] = jnp.zeros((SUBLANES, w), F32)
        h_ref[...] = jnp.zeros_like(h_ref)

    xpad_ref[SUBLANES:SUBLANES + t, :] = xy_ref[:, :w]
    u = cb_ref[...]
    for j in range(CONV_W):
        off = SUBLANES - (CONV_W - 1) + j
        u = u + cw_ref[j:j + 1, :] * xpad_ref[off:off + t, :]
    xpad_ref[0:SUBLANES, :] = xpad_ref[t:t + SUBLANES, :]

    gates = jnp.dot(u.astype(BF16), wg_ref[...], preferred_element_type=F32) + bg_ref[...]
    r = jax.nn.sigmoid(gates[:, :w])
    ig = jax.nn.sigmoid(gates[:, w:])
    neg_lam = -lam_ref[...]
    softplus = jnp.maximum(neg_lam, 0.0) + jnp.log1p(jnp.exp(-jnp.abs(neg_lam)))
    log_a = (-RG_C) * r * softplus
    a = jnp.exp(log_a)
    bt = jnp.sqrt(1.0 - a * a) * (ig * u)

    groups = t // SUBLANES
    a = a.reshape(groups, SUBLANES, w)
    bt = bt.reshape(groups, SUBLANES, w)
    sub = lax.broadcasted_iota(jnp.int32, a.shape, 1)
    shift = 1
    while shift < SUBLANES:
        keep = sub >= shift
        a_prev = jnp.where(keep, pltpu.roll(a, shift, axis=1), 1.0)
        b_prev = jnp.where(keep, pltpu.roll(bt, shift, axis=1), 0.0)
        bt = a * b_prev + bt
        a = a * a_prev
        shift *= 2
    carry = h_ref[...]
    rows = []
    for g in range(groups):
        hg = a[g] * carry + bt[g]
        rows.append(hg)
        carry = hg[SUBLANES - 1:SUBLANES, :]
    h_ref[...] = carry
    hs = jnp.concatenate(rows, axis=0)
    o_ref[...] = (hs * _gelu_tanh(xy_ref[:, w:])).astype(BF16)


def _block_diag(wb):
    n, bw, _ = wb.shape
    eye = jnp.eye(n, dtype=wb.dtype)
    return jnp.einsum('nhk,nm->nhmk', wb, eye).reshape(n * bw, n * bw)


def _lru(xy, conv_w, conv_b, wa, ba, wx, bx, lam, *, t=512):
    bsz, seq, w2 = xy.shape
    w = w2 // 2
    wg = jnp.concatenate([_block_diag(wa), _block_diag(wx)], axis=1).astype(BF16)
    bg = jnp.concatenate([ba, bx]).reshape(1, 2 * w)
    const = lambda shape: pl.BlockSpec(shape, lambda b, i: (0,) * len(shape))
    return pl.pallas_call(
        functools.partial(_lru_kernel, t=t, w=w),
        grid=(bsz, seq // t),
        in_specs=[
            pl.BlockSpec((None, t, w2), lambda b, i: (b, i, 0)),
            const((CONV_W, w)), const((1, w)), const((w, 2 * w)), const((1, 2 * w)),
            const((1, w)),
        ],
        out_specs=pl.BlockSpec((None, t, w), lambda b, i: (b, i, 0)),
        out_shape=jax.ShapeDtypeStruct((bsz, seq, w), BF16),
        scratch_shapes=[pltpu.VMEM((t + SUBLANES, w), F32), pltpu.VMEM((1, w), F32)],
        compiler_params=_cparams(("arbitrary", "arbitrary")),
        name="rglru",
    )(xy, conv_w, conv_b.reshape(1, w), wg, bg, lam.reshape(1, w))


def _attn_kernel(qmin_ref, qmax_ref, kmin_ref, kmax_ref,
                 q_ref, k_ref, v_ref, cq_ref, ck_ref, lq1_ref, lk1_ref, lq2_ref, lk2_ref,
                 g_ref, o_ref, m_ref, l_ref, acc_ref, *, tq, tk, nk, dh, lambda_init):
    b = pl.program_id(0)
    i = pl.program_id(2)
    m_ref[...] = jnp.full(m_ref.shape, -jnp.inf, F32)
    l_ref[...] = jnp.zeros(l_ref.shape, F32)
    acc_ref[...] = jnp.zeros(acc_ref.shape, F32)

    q = q_ref[...]
    lane = lax.broadcasted_iota(jnp.int32, q.shape, 1)
    qc = (jnp.where(lane < dh, q, jnp.zeros_like(q)), jnp.where(lane >= dh, q, jnp.zeros_like(q)))
    q_lo = qmin_ref[b, i]
    q_hi = qmax_ref[b, i]

    def process(j, masked):
        start = pl.multiple_of(j * tk, tk)
        kb = k_ref[pl.ds(start, tk), :]
        vb = v_ref[pl.ds(start, tk), :]
        if masked:
            ck = ck_ref[:, pl.ds(start, tk)]
            visible = ck <= jnp.tile(cq_ref[...], (1, tk // LANES))
        for c in range(2):
            s = lax.dot_general(qc[c], kb, (((1,), (1,)), ((), ())),
                                preferred_element_type=F32)
            if masked:
                s = jnp.where(visible, s, MASK_VALUE)
            m_prev = m_ref[c]
            m_new = jnp.maximum(m_prev, jnp.max(s, axis=-1, keepdims=True))
            alpha = jnp.exp2(m_prev - m_new)
            p = jnp.exp2(s - jnp.tile(m_new, (1, tk // LANES)))
            p_lanes = p[:, :LANES]
            for t in range(1, tk // LANES):
                p_lanes = p_lanes + p[:, t * LANES:(t + 1) * LANES]
            l_ref[c] = alpha * l_ref[c] + p_lanes
            acc_ref[c] = alpha * acc_ref[c] + jnp.dot(p.astype(BF16), vb,
                                                      preferred_element_type=F32)
            m_ref[c] = m_new

    def body(j, carry):
        k_lo = kmin_ref[b, j]
        k_hi = kmax_ref[b, j]
        needed = k_lo <= q_hi
        needs_mask = k_hi > q_lo

        @pl.when(jnp.logical_and(needed, needs_mask))
        def _():
            process(j, True)

        @pl.when(jnp.logical_and(needed, jnp.logical_not(needs_mask)))
        def _():
            process(j, False)

        return carry

    lax.fori_loop(0, nk, body, 0)

    lam = (jnp.exp(jnp.sum(lq1_ref[...] * lk1_ref[...], keepdims=True))
           - jnp.exp(jnp.sum(lq2_ref[...] * lk2_ref[...], keepdims=True)) + lambda_init)
    l0 = jnp.sum(l_ref[0], axis=-1, keepdims=True)
    l1 = jnp.sum(l_ref[1], axis=-1, keepdims=True)
    o = acc_ref[0] / l0 - lam * (acc_ref[1] / l1)
    o_ref[...] = (_rmsnorm(o, g_ref[...]) * (1.0 - lambda_init)).astype(o_ref.dtype)


def _attention(q, k, v, positions, lq1, lk1, lq2, lk2, subln_g, lambda_init, *,
               dh, tq=1024, tk=1024):
    bsz, seq, aw = q.shape
    vd = 2 * dh
    heads = aw // vd
    nq, nk = seq // tq, seq // tk
    chunk = positions // CHUNK
    qmin = chunk.reshape(bsz, nq, tq).min(-1)
    qmax = chunk.reshape(bsz, nq, tq).max(-1)
    kmin = chunk.reshape(bsz, nk, tk).min(-1)
    kmax = chunk.reshape(bsz, nk, tk).max(-1)
    cq = jnp.broadcast_to(chunk[:, :, None], (bsz, seq, LANES))
    ck = chunk.reshape(bsz, 1, seq)
    vec = lambda n: pl.BlockSpec((1, n), lambda b, h, i, *_: (0, 0))
    grid_spec = pltpu.PrefetchScalarGridSpec(
        num_scalar_prefetch=4,
        grid=(bsz, heads, nq),
        in_specs=[
            pl.BlockSpec((None, tq, vd), lambda b, h, i, *_: (b, i, h)),
            pl.BlockSpec((None, seq, vd), lambda b, h, i, *_: (b, 0, h)),
            pl.BlockSpec((None, seq, vd), lambda b, h, i, *_: (b, 0, h)),
            pl.BlockSpec((None, tq, LANES), lambda b, h, i, *_: (b, i, 0)),
            pl.BlockSpec((None, 1, seq), lambda b, h, i, *_: (b, 0, 0)),
            vec(dh), vec(dh), vec(dh), vec(dh), vec(vd),
        ],
        out_specs=pl.BlockSpec((None, tq, vd), lambda b, h, i, *_: (b, i, h)),
        scratch_shapes=[pltpu.VMEM((2, tq, LANES), F32), pltpu.VMEM((2, tq, LANES), F32),
                        pltpu.VMEM((2, tq, vd), F32)],
    )
    return pl.pallas_call(
        functools.partial(_attn_kernel, tq=tq, tk=tk, nk=nk, dh=dh, lambda_init=lambda_init),
        grid_spec=grid_spec,
        out_shape=jax.ShapeDtypeStruct((bsz, seq, aw), BF16),
        compiler_params=_cparams(("arbitrary", "arbitrary", "arbitrary")),
        name="diff_attn",
    )(qmin, qmax, kmin, kmax, q, k, v, cq, ck,
      lq1.reshape(1, dh), lk1.reshape(1, dh), lq2.reshape(1, dh), lk2.reshape(1, dh),
      subln_g.reshape(1, vd))


def _outproj_kernel(*refs, w, with_router):
    if with_router:
        (lru_ref, att_ref, x_ref, g1_ref, sc_ref, sh_ref, ln_ref, wo_ref, rt_ref,
         x1_ref, h2_ref, route_ref, cend_ref, carry_ref) = refs
    else:
        (lru_ref, att_ref, x_ref, g1_ref, sc_ref, sh_ref, ln_ref, wo_ref,
         x1_ref, h2_ref) = refs
    y = (jnp.dot(lru_ref[...], wo_ref[:w, :], preferred_element_type=F32)
         + jnp.dot(att_ref[...], wo_ref[w:, :], preferred_element_type=F32))
    x1 = x_ref[...] + g1_ref[...] * y
    x1_ref[...] = x1
    h2 = _rmsnorm(x1, ln_ref[...]) * (1.0 + sc_ref[...]) + sh_ref[...]
    h2_ref[...] = h2.astype(BF16)
    if with_router:
        def split(v):
            hi = v.astype(BF16)
            return hi, (v - hi.astype(F32)).astype(BF16)

        h_hi, h_lo = split(h2)
        r_hi, r_lo = split(rt_ref[...])
        nt_dims = (((1,), (1,)), ((), ()))
        dot_nt = lambda a, b: lax.dot_general(a, b, nt_dims, preferred_element_type=F32)
        logits = dot_nt(r_hi, h_hi) + (dot_nt(r_hi, h_lo) + dot_nt(r_lo, h_hi))
        n_rows, tm = logits.shape
        row = lax.broadcasted_iota(jnp.int32, logits.shape, 0)
        lg = jnp.where(row < N_EXPERTS, logits, -jnp.inf)
        m1 = jnp.max(lg, axis=0, keepdims=True)
        i1 = jnp.min(jnp.where(lg == m1, row, n_rows), axis=0, keepdims=True)
        lg2 = jnp.where(row == i1, -jnp.inf, lg)
        m2 = jnp.max(lg2, axis=0, keepdims=True)
        i2 = jnp.min(jnp.where(lg2 == m2, row, n_rows), axis=0, keepdims=True)
        e2 = jnp.exp(m2 - m1)
        w1 = 1.0 / (1.0 + e2)
        w2 = e2 / (1.0 + e2)

        @pl.when(jnp.logical_and(pl.program_id(0) == 0, pl.program_id(1) == 0))
        def _():
            carry_ref[...] = jnp.zeros_like(carry_ref)

        onehot = jnp.where(row == i1, 1.0, jnp.where(row == i2, 1.0, 0.0))
        tri = (lax.broadcasted_iota(jnp.int32, (tm, tm), 0)
               < lax.broadcasted_iota(jnp.int32, (tm, tm), 1))
        prefix = jnp.dot(onehot.astype(BF16), jnp.where(tri, 1.0, 0.0).astype(BF16),
                         preferred_element_type=F32) + jnp.tile(carry_ref[...], (1, tm // LANES))
        rank1 = jnp.sum(jnp.where(row == i1, prefix, 0.0), axis=0, keepdims=True)
        rank2 = jnp.sum(jnp.where(row == i2, prefix, 0.0), axis=0, keepdims=True)
        carry = carry_ref[...] + jnp.sum(onehot, axis=1, keepdims=True)
        carry_ref[...] = carry
        cend_ref[...] = carry
        fields = (i1.astype(F32), i2.astype(F32), rank1, rank2, w1, w2)
        field_row = lax.broadcasted_iota(jnp.int32, route_ref.shape, 0)
        route = jnp.zeros(route_ref.shape, F32)
        for n, val in enumerate(fields):
            route = jnp.where(field_row == n, val, route)
        route_ref[...] = route


def _outproj(lru, att, x, mod_l, ln_g, w_out_b, router=None, *, tm=512):
    bsz, seq, d = x.shape
    w = lru.shape[-1]
    with_router = router is not None
    row = lambda k: pl.BlockSpec((None, None, 1, d), lambda b, i: (b, k, 0, 0))
    tok = lambda n: pl.BlockSpec((None, tm, n), lambda b, i: (b, i, 0))
    in_specs = [tok(w), tok(w), tok(d), row(2), row(4), row(3),
                pl.BlockSpec((1, d), lambda b, i: (0, 0)),
                pl.BlockSpec((d, d), lambda b, i: (0, 0))]
    args = [lru, att, x, mod_l, mod_l, mod_l, ln_g.reshape(1, d), w_out_b]
    out_specs = [tok(d), tok(d)]
    out_shape = [jax.ShapeDtypeStruct((bsz, seq, d), F32),
                 jax.ShapeDtypeStruct((bsz, seq, d), BF16)]
    scratch = []
    if with_router:
        nt = seq // tm
        e_rows = 2 * SUBLANES
        rt = jnp.zeros((e_rows, d), F32).at[:N_EXPERTS].set(router.T)
        in_specs.append(pl.BlockSpec((e_rows, d), lambda b, i: (0, 0)))
        args.append(rt)
        out_specs.append(pl.BlockSpec((None, SUBLANES, tm), lambda b, i: (b * nt + i, 0, 0)))
        out_shape.append(jax.ShapeDtypeStruct((bsz * nt, SUBLANES, tm), F32))
        out_specs.append(pl.BlockSpec((e_rows, LANES), lambda b, i: (b * nt + i, 0)))
        out_shape.append(jax.ShapeDtypeStruct((bsz * nt * e_rows, LANES), F32))
        scratch.append(pltpu.VMEM((e_rows, LANES), F32))
    return pl.pallas_call(
        functools.partial(_outproj_kernel, w=w, with_router=with_router),
        grid=(bsz, seq // tm),
        in_specs=in_specs, out_specs=out_specs, out_shape=out_shape,
        scratch_shapes=scratch,
        compiler_params=_cparams(("arbitrary", "arbitrary")),
        name="outproj_router" if with_router else "outproj",
    )(*args)


def _ffn_kernel(*refs, final_norm):
    if final_norm:
        h_ref, wg_ref, wu_ref, wd_ref, x1_ref, g2_ref, fg_ref, o_ref, acc_ref = refs
    else:
        h_ref, wg_ref, wu_ref, wd_ref, x1_ref, g2_ref, o_ref, acc_ref = refs
    j = pl.program_id(2)

    @pl.when(j == 0)
    def _():
        acc_ref[...] = jnp.zeros_like(acc_ref)

    h = h_ref[...]
    act = (jax.nn.silu(jnp.dot(h, wg_ref[...], preferred_element_type=F32))
           * jnp.dot(h, wu_ref[...], preferred_element_type=F32))
    acc_ref[...] += jnp.dot(act.astype(BF16), wd_ref[...], preferred_element_type=F32)

    @pl.when(j == pl.num_programs(2) - 1)
    def _():
        out = x1_ref[...] + g2_ref[...] * acc_ref[...]
        if final_norm:
            out = _rmsnorm(out, fg_ref[...])
        o_ref[...] = out


def _ffn(h2, wg, wu, wd, x1, mod_l, final_g=None, *, tm=1024, tf=512):
    bsz, seq, d = x1.shape
    ff = wg.shape[1]
    final_norm = final_g is not None
    tok = lambda n: pl.BlockSpec((None, tm, n), lambda b, i, j: (b, i, 0))
    in_specs = [tok(d),
                pl.BlockSpec((d, tf), lambda b, i, j: (0, j)),
                pl.BlockSpec((d, tf), lambda b, i, j: (0, j)),
                pl.BlockSpec((tf, d), lambda b, i, j: (j, 0)),
                tok(d),
                pl.BlockSpec((None, None, 1, d), lambda b, i, j: (b, 5, 0, 0))]
    args = [h2, wg, wu, wd, x1, mod_l]
    if final_norm:
        in_specs.append(pl.BlockSpec((1, d), lambda b, i, j: (0, 0)))
        args.append(final_g.reshape(1, d))
    return pl.pallas_call(
        functools.partial(_ffn_kernel, final_norm=final_norm),
        grid=(bsz, seq // tm, ff // tf),
        in_specs=in_specs,
        out_specs=tok(d),
        out_shape=jax.ShapeDtypeStruct((bsz, seq, d), F32),
        scratch_shapes=[pltpu.VMEM((tm, d), F32)],
        compiler_params=_cparams(("arbitrary",) * 3),
        name="dense_ffn",
    )(*args)


MOE_TB = 512
MOE_TG = 256
MOE_TF = 512


def _pick(table, idx):
    n = table.shape[-1]
    hit = idx[..., None] == jnp.arange(n, dtype=jnp.int32)
    return jnp.sum(jnp.where(hit, table, 0), axis=-1)


def _count_le(sorted_vals, x):
    return jnp.sum(sorted_vals[None, :] <= x[:, None], axis=1).astype(jnp.int32)


def _moe_plan(route_t, cend, *, m_tok):
    i32 = jnp.int32
    nb = m_tok // MOE_TB
    n_e = N_EXPERTS
    ntg = (2 * m_tok + n_e * MOE_TF) // MOE_TG
    ntf = ntg * MOE_TG // MOE_TF
    pmax = ntg + n_e * nb
    e1 = route_t[:, 0, :].astype(i32)
    e2 = route_t[:, 1, :].astype(i32)
    r1 = route_t[:, 2, :].astype(i32)
    r2 = route_t[:, 3, :].astype(i32)
    cend8 = cend.reshape(nb, -1, LANES)[:, :n_e, 0].astype(i32)
    first = jnp.concatenate([jnp.zeros((1, n_e), i32), cend8[:-1]])
    cnt = cend8[-1]
    gsz = (cnt + MOE_TF - 1) // MOE_TF * MOE_TF
    gend = jnp.cumsum(gsz)
    goff = gend - gsz
    total = gend[-1]
    d1 = _pick(goff, e1) + r1
    d2 = _pick(goff, e2) + r2

    row0 = jnp.arange(ntg, dtype=i32) * MOE_TG
    tvalid = row0 < total
    te = jnp.minimum(_count_le(gend, row0), n_e - 1)
    ra = row0 - _pick(goff, te)
    rb = jnp.minimum(ra + MOE_TG, _pick(cnt, te))
    nonempty = jnp.logical_and(tvalid, rb > ra)
    te_hit = te[None, :, None] == jnp.arange(n_e, dtype=i32)
    cend_te = jnp.sum(jnp.where(te_hit, cend8[:, None, :], 0), axis=-1)
    first_te = jnp.sum(jnp.where(te_hit, first[:, None, :], 0), axis=-1)
    lo = jnp.sum(cend_te <= ra[None, :], axis=0).astype(i32)
    hi = jnp.sum(first_te <= (rb - 1)[None, :], axis=0).astype(i32) - 1
    lo = jnp.where(nonempty, lo, 0)
    hi = jnp.where(nonempty, hi, 0)
    npairs = jnp.where(tvalid, hi - lo + 1, 0)
    pend = jnp.cumsum(npairs)
    pstart = pend - npairs
    n_pairs = pend[-1]
    p = jnp.arange(pmax, dtype=i32)
    pc = jnp.minimum(p, n_pairs - 1)
    pvalid = p < n_pairs
    g_tile = jnp.minimum(_count_le(pend, pc), ntg - 1)
    g_start = _pick(pstart, g_tile)
    g_tb = _pick(lo, g_tile) + pc - g_start
    g_flag = jnp.where(pvalid, 4 + (pc == g_start) + 2 * (pc == _pick(pend, g_tile) - 1), 0)

    has = cend8 > first
    tlo = (goff[None, :] + first) // MOE_TG
    thi = (goff[None, :] + cend8 - 1) // MOE_TG
    cpairs = jnp.where(has, thi - tlo + 1, 0).reshape(-1)
    tlo = tlo.reshape(-1)
    cend_p = jnp.cumsum(cpairs)
    cstart_p = cend_p - cpairs
    qidx = jnp.minimum(_count_le(cend_p, pc), nb * n_e - 1)
    c_tile = _pick(tlo, qidx) + pc - _pick(cstart_p, qidx)
    c_tb = qidx // n_e
    tb_start = _pick(cstart_p.reshape(nb, n_e)[:, 0], c_tb)
    tb_end = _pick(cend_p.reshape(nb, n_e)[:, n_e - 1], c_tb)
    c_flag = jnp.where(pvalid, 4 + (pc == tb_start) + 2 * (pc == tb_end - 1), 0)

    frow0 = jnp.arange(ntf, dtype=i32) * MOE_TF
    f_valid = (frow0 < total).astype(i32)
    f_idx = jnp.minimum(jnp.arange(ntf, dtype=i32), total // MOE_TF - 1)
    f_exp = jnp.minimum(_count_le(gend, f_idx * MOE_TF), n_e - 1)
    return dict(d1=d1, d2=d2, g_tile=g_tile, g_tb=g_tb, g_flag=g_flag.astype(i32),
                c_tile=c_tile, c_tb=c_tb, c_flag=c_flag.astype(i32),
                f_idx=f_idx, f_exp=f_exp, f_valid=f_valid, ntg=ntg, ntf=ntf, pmax=pmax)


def _moe_gather_kernel(tile_ref, tb_ref, flag_ref, h_ref, d1_ref, d2_ref, w1_ref, w2_ref,
                       xs_ref, gs_ref, acc_ref, gacc_ref):
    p = pl.program_id(0)
    flag = flag_ref[p]

    @pl.when((flag & 1) != 0)
    def _():
        acc_ref[...] = jnp.zeros_like(acc_ref)
        gacc_ref[...] = jnp.zeros_like(gacc_ref)

    @pl.when((flag & 4) != 0)
    def _():
        rows = tile_ref[p] * MOE_TG + lax.broadcasted_iota(jnp.int32, (MOE_TG, MOE_TB), 0)
        m1 = d1_ref[...] == rows
        m2 = d2_ref[...] == rows
        sel = jnp.where(m1, 1.0, jnp.where(m2, 1.0, 0.0)).astype(BF16)
        acc_ref[...] += jnp.dot(sel, h_ref[...], preferred_element_type=F32)
        gate = jnp.where(m1, w1_ref[...], jnp.where(m2, w2_ref[...], 0.0))
        gacc_ref[...] += jnp.sum(gate, axis=-1, keepdims=True)

    @pl.when((flag & 2) != 0)
    def _():
        xs_ref[...] = acc_ref[...].astype(BF16)
        gs_ref[...] = gacc_ref[...]


def _moe_gather(h2, plan, w1, w2):
    m_tok, d = h2.shape
    nb = m_tok // MOE_TB
    rows = plan['ntg'] * MOE_TG
    tokrow = lambda: pl.BlockSpec((None, 1, MOE_TB), lambda p, t, b, f: (b[p], 0, 0))
    as_rows = lambda a: a.reshape(nb, 1, MOE_TB)
    grid_spec = pltpu.PrefetchScalarGridSpec(
        num_scalar_prefetch=3,
        grid=(plan['pmax'],),
        in_specs=[pl.BlockSpec((MOE_TB, d), lambda p, t, b, f: (b[p], 0)),
                  tokrow(), tokrow(), tokrow(), tokrow()],
        out_specs=[pl.BlockSpec((MOE_TG, d), lambda p, t, b, f: (t[p], 0)),
                   pl.BlockSpec((MOE_TG, LANES), lambda p, t, b, f: (t[p], 0))],
        scratch_shapes=[pltpu.VMEM((MOE_TG, d), F32), pltpu.VMEM((MOE_TG, LANES), F32)],
    )
    return pl.pallas_call(
        _moe_gather_kernel,
        grid_spec=grid_spec,
        out_shape=[jax.ShapeDtypeStruct((rows, d), BF16),
                   jax.ShapeDtypeStruct((rows, LANES), F32)],
        compiler_params=_cparams(("arbitrary",)),
        name="moe_gather",
    )(plan['g_tile'], plan['g_tb'], plan['g_flag'], h2,
      as_rows(plan['d1']), as_rows(plan['d2']), as_rows(w1), as_rows(w2))


def _moe_ffn_kernel(idx_ref, exp_ref, valid_ref, xs_ref, wg_ref, wu_ref, wd_ref, gs_ref,
                    o_ref, acc_ref):
    n = pl.program_id(0)
    j = pl.program_id(1)

    @pl.when(valid_ref[n] != 0)
    def _():
        @pl.when(j == 0)
        def _():
            acc_ref[...] = jnp.zeros_like(acc_ref)

        h = xs_ref[...]
        act = (jax.nn.silu(jnp.dot(h, wg_ref[...], preferred_element_type=F32))
               * jnp.dot(h, wu_ref[...], preferred_element_type=F32))
        acc_ref[...] += jnp.dot(act.astype(BF16), wd_ref[...], preferred_element_type=F32)

        @pl.when(j == pl.num_programs(1) - 1)
        def _():
            gate = jnp.tile(gs_ref[...], (1, acc_ref.shape[1] // LANES))
            o_ref[...] = (gate * acc_ref[...]).astype(o_ref.dtype)


def _moe_ffn(xs, gs, wg, wu, wd, plan, *, tf=1536):
    rows, d = xs.shape
    ff = wg.shape[2]
    tf = min(tf, ff)
    nj = ff // tf

    def ff_tile(n, j, v):
        return j * v[n] + (nj - 1) * (1 - v[n])

    grid_spec = pltpu.PrefetchScalarGridSpec(
        num_scalar_prefetch=3,
        grid=(plan['ntf'], nj),
        in_specs=[pl.BlockSpec((MOE_TF, d), lambda n, j, i, e, v: (i[n], 0)),
                  pl.BlockSpec((None, d, tf), lambda n, j, i, e, v: (e[n], 0, ff_tile(n, j, v))),
                  pl.BlockSpec((None, d, tf), lambda n, j, i, e, v: (e[n], 0, ff_tile(n, j, v))),
                  pl.BlockSpec((None, tf, d), lambda n, j, i, e, v: (e[n], ff_tile(n, j, v), 0)),
                  pl.BlockSpec((MOE_TF, LANES), lambda n, j, i, e, v: (i[n], 0))],
        out_specs=pl.BlockSpec((MOE_TF, d), lambda n, j, i, e, v: (i[n], 0)),
        scratch_shapes=[pltpu.VMEM((MOE_TF, d), F32)],
    )
    return pl.pallas_call(
        _moe_ffn_kernel,
        grid_spec=grid_spec,
        out_shape=jax.ShapeDtypeStruct((rows, d), BF16),
        compiler_params=_cparams(("arbitrary", "arbitrary")),
        name="moe_ffn",
    )(plan['f_idx'], plan['f_exp'], plan['f_valid'], xs, wg, wu, wd, gs)


def _moe_combine_kernel(*refs, final_norm):
    if final_norm:
        (tile_ref, tb_ref, flag_ref, ye_ref, d1_ref, d2_ref, x1_ref, g2_ref, fg_ref,
         o_ref, acc_ref) = refs
    else:
        (tile_ref, tb_ref, flag_ref, ye_ref, d1_ref, d2_ref, x1_ref, g2_ref,
         o_ref, acc_ref) = refs
    p = pl.program_id(0)
    flag = flag_ref[p]

    @pl.when((flag & 1) != 0)
    def _():
        acc_ref[...] = jnp.zeros_like(acc_ref)

    @pl.when((flag & 4) != 0)
    def _():
        cols = tile_ref[p] * MOE_TG + lax.broadcasted_iota(jnp.int32, (MOE_TB, MOE_TG), 1)
        d1 = jnp.tile(d1_ref[...], (1, MOE_TG // LANES))
        d2 = jnp.tile(d2_ref[...], (1, MOE_TG // LANES))
        sel = jnp.where(d1 == cols, 1.0, jnp.where(d2 == cols, 1.0, 0.0)).astype(BF16)
        acc_ref[...] += jnp.dot(sel, ye_ref[...], preferred_element_type=F32)

    @pl.when((flag & 2) != 0)
    def _():
        out = x1_ref[...] + g2_ref[...] * acc_ref[...]
        if final_norm:
            out = _rmsnorm(out, fg_ref[...])
        o_ref[...] = out


def _moe_combine(ye, plan, x1, mod_l, final_g=None):
    bsz, seq, d = x1.shape
    m_tok = bsz * seq
    blocks_per_seq = seq // MOE_TB
    final_norm = final_g is not None
    rep = lambda a: jnp.broadcast_to(a.reshape(m_tok, 1), (m_tok, LANES))
    in_specs = [pl.BlockSpec((MOE_TG, d), lambda p, t, b, f: (t[p], 0)),
                pl.BlockSpec((MOE_TB, LANES), lambda p, t, b, f: (b[p], 0)),
                pl.BlockSpec((MOE_TB, LANES), lambda p, t, b, f: (b[p], 0)),
                pl.BlockSpec((MOE_TB, d), lambda p, t, b, f: (b[p], 0)),
                pl.BlockSpec((None, None, 1, d),
                             lambda p, t, b, f: (b[p] // blocks_per_seq, 5, 0, 0))]
    args = [ye, rep(plan['d1']), rep(plan['d2']), x1.reshape(m_tok, d), mod_l]
    if final_norm:
        in_specs.append(pl.BlockSpec((1, d), lambda p, t, b, f: (0, 0)))
        args.append(final_g.reshape(1, d))
    grid_spec = pltpu.PrefetchScalarGridSpec(
        num_scalar_prefetch=3,
        grid=(plan['pmax'],),
        in_specs=in_specs,
        out_specs=pl.BlockSpec((MOE_TB, d), lambda p, t, b, f: (b[p], 0)),
        scratch_shapes=[pltpu.VMEM((MOE_TB, d), F32)],
    )
    out = pl.pallas_call(
        functools.partial(_moe_combine_kernel, final_norm=final_norm),
        grid_spec=grid_spec,
        out_shape=jax.ShapeDtypeStruct((m_tok, d), F32),
        compiler_params=_cparams(("arbitrary",)),
        name="moe_combine",
    )(plan['c_tile'], plan['c_tb'], plan['c_flag'], *args)
    return out.reshape(bsz, seq, d)


def _moe(h2, route_t, cend, x1, mod_l, wg, wu, wd, final_g=None):
    bsz, seq, d = x1.shape
    m_tok = bsz * seq
    plan = _moe_plan(route_t, cend, m_tok=m_tok)
    xs, gs = _moe_gather(h2.reshape(m_tok, d), plan, route_t[:, 4, :], route_t[:, 5, :])
    ye = _moe_ffn(xs, gs, wg, wu, wd, plan)
    return _moe_combine(ye, plan, x1, mod_l, final_g)


def kernel(x, c, positions, ada_w, ada_b, ln1_g, ln2_g, w_in, conv_w, conv_b, gate_a_w, gate_a_b, gate_x_w, gate_x_b, lru_lambda, lam_q1, lam_k1, lam_q2, lam_k2, subln_g, w_out, ffn_w_gate, ffn_w_up, ffn_w_down, moe_router, moe_w_gate, moe_w_up, moe_w_down, final_g):
    depth = ada_w.shape[0]
    lru_w = conv_w.shape[-1]
    vd = subln_g.shape[-1]
    dh = vd // 2
    attn_w = DIFF_HEADS * vd

    mod = _modulation(c, ada_w, ada_b)
    cos_t, sin_t = _rope_tables(positions, dh)
    for l in range(depth):
        lambda_init = 0.8 - 0.6 * math.exp(-0.3 * l)
        mod_l = mod[l]
        xy, q, k, v = _inproj(x, mod_l, ln1_g[l], w_in[l].astype(BF16), cos_t, sin_t,
                              lru_w=lru_w, attn_w=attn_w, dh=dh)
        lru = _lru(xy, conv_w[l], conv_b[l], gate_a_w[l], gate_a_b[l], gate_x_w[l],
                   gate_x_b[l], lru_lambda[l])
        att = _attention(q, k, v, positions, lam_q1[l], lam_k1[l], lam_q2[l], lam_k2[l],
                         subln_g[l], lambda_init, dh=dh)
        fg = final_g if l == depth - 1 else None
        j = l // 2
        if l % 2 == 0:
            x1, h2 = _outproj(lru, att, x, mod_l, ln2_g[l], w_out[l].astype(BF16))
            x = _ffn(h2, ffn_w_gate[j].astype(BF16), ffn_w_up[j].astype(BF16),
                     ffn_w_down[j].astype(BF16), x1, mod_l, final_g=fg)
        else:
            x1, h2, route, cend = _outproj(lru, att, x, mod_l, ln2_g[l], w_out[l].astype(BF16),
                                           router=moe_router[j])
            x = _moe(h2, route, cend, x1, mod_l, moe_w_gate[j].astype(BF16),
                     moe_w_up[j].astype(BF16), moe_w_down[j].astype(BF16), final_g=fg)
    return x
```

```python
import functools
import math

import jax
import jax.numpy as jnp
from jax import lax
from jax.experimental import pallas as pl
from jax.experimental.pallas import tpu as pltpu

F32 = jnp.float32
BF16 = jnp.bfloat16
HIGHEST = lax.Precision.HIGHEST

CHUNK = 64
LRU_BLOCKS = 8
CONV_W = 4
RG_C = 8.0
DIFF_HEADS = 4
ROPE_THETA = 10000.0
N_EXPERTS = 8
EPS = 1e-6
LANES = 128
SUBLANES = 8
VMEM_LIMIT = 56 * 1024 * 1024
MASK_VALUE = -0.5 * float(jnp.finfo(jnp.float32).max)


def _cparams(sem):
    return pltpu.CompilerParams(dimension_semantics=sem, vmem_limit_bytes=VMEM_LIMIT)


def _rmsnorm(x, g):
    return x * lax.rsqrt(jnp.mean(x * x, axis=-1, keepdims=True) + EPS) * g


def _mod_kernel(c_ref, w_ref, b_ref, o_ref):
    c = c_ref[...]
    s = c * jax.nn.sigmoid(c)
    o_ref[...] = jnp.dot(s, w_ref[...], precision=HIGHEST,
                         preferred_element_type=F32) + b_ref[...]


def _modulation(c, ada_w, ada_b, tn=1024):
    depth, d, n = ada_w.shape
    bsz = c.shape[0]
    rows = -(-bsz // SUBLANES) * SUBLANES
    c_pad = jnp.zeros((rows, d), F32).at[:bsz].set(c)
    out = pl.pallas_call(
        _mod_kernel,
        grid=(depth, n // tn),
        in_specs=[
            pl.BlockSpec((rows, d), lambda l, j: (0, 0)),
            pl.BlockSpec((None, d, tn), lambda l, j: (l, 0, j)),
            pl.BlockSpec((None, 1, tn), lambda l, j: (l, 0, j)),
        ],
        out_specs=pl.BlockSpec((None, rows, tn), lambda l, j: (l, 0, j)),
        out_shape=jax.ShapeDtypeStruct((depth, rows, n), F32),
        compiler_params=_cparams(("arbitrary", "arbitrary")),
        name="adaln_mod",
    )(c_pad, ada_w, ada_b.reshape(depth, 1, n))
    return out[:, :bsz].reshape(depth, bsz, 6, 1, d)


def _rope_table_kernel(pos_ref, inv_ref, cos_ref, sin_ref):
    ang = pos_ref[...].astype(F32) * inv_ref[...]
    cos_ref[...] = jnp.cos(ang)
    sin_ref[...] = jnp.sin(ang)


def _rope_tables(positions, dh):
    n_freq = dh // 2
    per_row = LANES // n_freq
    tok = positions.size
    rows = tok // per_row
    inv = ROPE_THETA ** (-jnp.arange(0, dh, 2, dtype=F32) / dh)
    pos_x = jnp.repeat(positions.reshape(-1), n_freq).reshape(rows, LANES)
    inv_x = jnp.tile(inv, per_row).reshape(1, LANES)
    tr = min(rows, 1024)
    cos, sin = pl.pallas_call(
        _rope_table_kernel,
        grid=(rows // tr,),
        in_specs=[pl.BlockSpec((tr, LANES), lambda i: (i, 0)),
                  pl.BlockSpec((1, LANES), lambda i: (0, 0))],
        out_specs=[pl.BlockSpec((tr, LANES), lambda i: (i, 0))] * 2,
        out_shape=[jax.ShapeDtypeStruct((rows, LANES), F32)] * 2,
        compiler_params=_cparams(("arbitrary",)),
        name="rope_tables",
    )(pos_x, inv_x)
    cos = cos.reshape(tok, n_freq)
    sin = sin.reshape(tok, n_freq)
    reps = LANES // dh
    cos_t = jnp.tile(jnp.concatenate([cos, cos], axis=-1), (1, reps))
    sin_t = jnp.tile(jnp.concatenate([-sin, sin], axis=-1), (1, reps))
    return cos_t, sin_t


def _inproj_kernel(x_ref, sc_ref, sh_ref, g_ref, w_ref, cos_ref, sin_ref,
                   xy_ref, q_ref, k_ref, v_ref, *, lru2, attn_w, dh):
    h = _rmsnorm(x_ref[...], g_ref[...]) * (1.0 + sc_ref[...]) + sh_ref[...]
    hb = h.astype(BF16)
    xy_ref[...] = jnp.dot(hb, w_ref[:, :lru2], preferred_element_type=F32)

    reps = attn_w // LANES
    cos = jnp.tile(cos_ref[...], (1, reps))
    sin = jnp.tile(sin_ref[...], (1, reps))
    lane = lax.broadcasted_iota(jnp.int32, cos.shape, 1)
    first_half = (lane % dh) < (dh // 2)

    def rope(t):
        fwd = pltpu.roll(t, attn_w - dh // 2, axis=1)
        bwd = pltpu.roll(t, dh // 2, axis=1)
        return t * cos + jnp.where(first_half, fwd, bwd) * sin

    q = jnp.dot(hb, w_ref[:, lru2:lru2 + attn_w], preferred_element_type=F32)
    q_ref[...] = (rope(q) * (dh ** -0.5 * math.log2(math.e))).astype(BF16)
    k = jnp.dot(hb, w_ref[:, lru2 + attn_w:lru2 + 2 * attn_w], preferred_element_type=F32)
    k_ref[...] = rope(k).astype(BF16)
    v = jnp.dot(hb, w_ref[:, lru2 + 2 * attn_w:], preferred_element_type=F32)
    v_ref[...] = v.astype(BF16)


def _inproj(x, mod_l, ln_g, w_in_b, cos_t, sin_t, *, lru_w, attn_w, dh, tm=512):
    bsz, seq, d = x.shape
    nt = seq // tm
    d_in = w_in_b.shape[1]
    lru2 = 2 * lru_w
    row = lambda k: pl.BlockSpec((None, None, 1, d), lambda b, i: (b, k, 0, 0))
    tok = lambda w: pl.BlockSpec((None, tm, w), lambda b, i: (b, i, 0))
    return pl.pallas_call(
        functools.partial(_inproj_kernel, lru2=lru2, attn_w=attn_w, dh=dh),
        grid=(bsz, nt),
        in_specs=[
            tok(d), row(1), row(0),
            pl.BlockSpec((1, d), lambda b, i: (0, 0)),
            pl.BlockSpec((d, d_in), lambda b, i: (0, 0)),
            pl.BlockSpec((tm, LANES), lambda b, i: (b * nt + i, 0)),
            pl.BlockSpec((tm, LANES), lambda b, i: (b * nt + i, 0)),
        ],
        out_specs=[tok(lru2), tok(attn_w), tok(attn_w), tok(attn_w)],
        out_shape=[jax.ShapeDtypeStruct((bsz, seq, lru2), F32),
                   jax.ShapeDtypeStruct((bsz, seq, attn_w), BF16),
                   jax.ShapeDtypeStruct((bsz, seq, attn_w), BF16),
                   jax.ShapeDtypeStruct((bsz, seq, attn_w), BF16)],
        compiler_params=_cparams(("arbitrary", "arbitrary")),
        name="inproj",
    )(x, mod_l, mod_l, ln_g.reshape(1, d), w_in_b, cos_t, sin_t)


def _gelu_tanh(x):
    return 0.5 * x * (1.0 + jnp.tanh(math.sqrt(2.0 / math.pi) * (x + 0.044715 * (x * x * x))))


def _lru_kernel(xy_ref, cw_ref, cb_ref, wg_ref, bg_ref, lam_ref, o_ref,
                xpad_ref, h_ref, *, t, w):
    @pl.when(pl.program_id(1) == 0)
    def _():
        xpad_ref[0:SUBLANES, :] = jnp.zeros((SUBLANES, w), F32)
        h_ref[...] = jnp.zeros_like(h_ref)

    xpad_ref[SUBLANES:SUBLANES + t, :] = xy_ref[:, :w]
    u = cb_ref[...]
    for j in range(CONV_W):
        off = SUBLANES - (CONV_W - 1) + j
        u = u + cw_ref[j:j + 1, :] * xpad_ref[off:off + t, :]
    xpad_ref[0:SUBLANES, :] = xpad_ref[t:t + SUBLANES, :]

    gates = jnp.dot(u.astype(BF16), wg_ref[...], preferred_element_type=F32) + bg_ref[...]
    r = jax.nn.sigmoid(gates[:, :w])
    ig = jax.nn.sigmoid(gates[:, w:])
    neg_lam = -lam_ref[...]
    softplus = jnp.maximum(neg_lam, 0.0) + jnp.log1p(jnp.exp(-jnp.abs(neg_lam)))
    log_a = (-RG_C) * r * softplus
    a = jnp.exp(log_a)
    bt = jnp.sqrt(1.0 - a * a) * (ig * u)

    groups = t // SUBLANES
    a = a.reshape(groups, SUBLANES, w)
    bt = bt.reshape(groups, SUBLANES, w)
    sub = lax.broadcasted_iota(jnp.int32, a.shape, 1)
    shift = 1
    while shift < SUBLANES:
        keep = sub >= shift
        a_prev = jnp.where(keep, pltpu.roll(a, shift, axis=1), 1.0)
        b_prev = jnp.where(keep, pltpu.roll(bt, shift, axis=1), 0.0)
        bt = a * b_prev + bt
        a = a * a_prev
        shift *= 2
    carry = h_ref[...]
    rows = []
    for g in range(groups):
        hg = a[g] * carry + bt[g]
        rows.append(hg)
        carry = hg[SUBLANES - 1:SUBLANES, :]
    h_ref[...] = carry
    hs = jnp.concatenate(rows, axis=0)
    o_ref[...] = (hs * _gelu_tanh(xy_ref[:, w:])).astype(BF16)


def _block_diag(wb):
    n, bw, _ = wb.shape
    eye = jnp.eye(n, dtype=wb.dtype)
    return jnp.einsum('nhk,nm->nhmk', wb, eye).reshape(n * bw, n * bw)


def _lru(xy, conv_w, conv_b, wa, ba, wx, bx, lam, *, t=512):
    bsz, seq, w2 = xy.shape
    w = w2 // 2
    wg = jnp.concatenate([_block_diag(wa), _block_diag(wx)], axis=1).astype(BF16)
    bg = jnp.concatenate([ba, bx]).reshape(1, 2 * w)
    const = lambda shape: pl.BlockSpec(shape, lambda b, i: (0,) * len(shape))
    return pl.pallas_call(
        functools.partial(_lru_kernel, t=t, w=w),
        grid=(bsz, seq // t),
        in_specs=[
            pl.BlockSpec((None, t, w2), lambda b, i: (b, i, 0)),
            const((CONV_W, w)), const((1, w)), const((w, 2 * w)), const((1, 2 * w)),
            const((1, w)),
        ],
        out_specs=pl.BlockSpec((None, t, w), lambda b, i: (b, i, 0)),
        out_shape=jax.ShapeDtypeStruct((bsz, seq, w), BF16),
        scratch_shapes=[pltpu.VMEM((t + SUBLANES, w), F32), pltpu.VMEM((1, w), F32)],
        compiler_params=_cparams(("arbitrary", "arbitrary")),
        name="rglru",
    )(xy, conv_w, conv_b.reshape(1, w), wg, bg, lam.reshape(1, w))


def _attn_kernel(qmin_ref, qmax_ref, kmin_ref, kmax_ref,
                 q_ref, k_ref, v_ref, cq_ref, ck_ref, lq1_ref, lk1_ref, lq2_ref, lk2_ref,
                 g_ref, o_ref, m_ref, l_ref, acc_ref, *, tq, tk, nk, dh, lambda_init):
    b = pl.program_id(0)
    i = pl.program_id(2)
    m_ref[...] = jnp.full(m_ref.shape, -jnp.inf, F32)
    l_ref[...] = jnp.zeros(l_ref.shape, F32)
    acc_ref[...] = jnp.zeros(acc_ref.shape, F32)

    q = q_ref[...]
    lane = lax.broadcasted_iota(jnp.int32, q.shape, 1)
    qc = (jnp.where(lane < dh, q, jnp.zeros_like(q)), jnp.where(lane >= dh, q, jnp.zeros_like(q)))
    q_lo = qmin_ref[b, i]
    q_hi = qmax_ref[b, i]

    def process(j, masked):
        start = pl.multiple_of(j * tk, tk)
        kb = k_ref[pl.ds(start, tk), :]
        vb = v_ref[pl.ds(start, tk), :]
        if masked:
            ck = ck_ref[:, pl.ds(start, tk)]
            visible = ck <= jnp.tile(cq_ref[...], (1, tk // LANES))
        for c in range(2):
            s = lax.dot_general(qc[c], kb, (((1,), (1,)), ((), ())),
                                preferred_element_type=F32)
            if masked:
                s = jnp.where(visible, s, MASK_VALUE)
            m_prev = m_ref[c]
            m_new = jnp.maximum(m_prev, jnp.max(s, axis=-1, keepdims=True))
            alpha = jnp.exp2(m_prev - m_new)
            p = jnp.exp2(s - jnp.tile(m_new, (1, tk // LANES)))
            p_lanes = p[:, :LANES]
            for t in range(1, tk // LANES):
                p_lanes = p_lanes + p[:, t * LANES:(t + 1) * LANES]
            l_ref[c] = alpha * l_ref[c] + p_lanes
            acc_ref[c] = alpha * acc_ref[c] + jnp.dot(p.astype(BF16), vb,
                                                      preferred_element_type=F32)
            m_ref[c] = m_new

    def body(j, carry):
        k_lo = kmin_ref[b, j]
        k_hi = kmax_ref[b, j]
        needed = k_lo <= q_hi
        needs_mask = k_hi > q_lo

        @pl.when(jnp.logical_and(needed, needs_mask))
        def _():
            process(j, True)

        @pl.when(jnp.logical_and(needed, jnp.logical_not(needs_mask)))
        def _():
            process(j, False)

        return carry

    lax.fori_loop(0, nk, body, 0)

    lam = (jnp.exp(jnp.sum(lq1_ref[...] * lk1_ref[...], keepdims=True))
           - jnp.exp(jnp.sum(lq2_ref[...] * lk2_ref[...], keepdims=True)) + lambda_init)
    l0 = jnp.sum(l_ref[0], axis=-1, keepdims=True)
    l1 = jnp.sum(l_ref[1], axis=-1, keepdims=True)
    o = acc_ref[0] / l0 - lam * (acc_ref[1] / l1)
    o_ref[...] = (_rmsnorm(o, g_ref[...]) * (1.0 - lambda_init)).astype(o_ref.dtype)


def _attention(q, k, v, positions, lq1, lk1, lq2, lk2, subln_g, lambda_init, *,
               dh, tq=1024, tk=1024):
    bsz, seq, aw = q.shape
    vd = 2 * dh
    heads = aw // vd
    nq, nk = seq // tq, seq // tk
    chunk = positions // CHUNK
    qmin = chunk.reshape(bsz, nq, tq).min(-1)
    qmax = chunk.reshape(bsz, nq, tq).max(-1)
    kmin = chunk.reshape(bsz, nk, tk).min(-1)
    kmax = chunk.reshape(bsz, nk, tk).max(-1)
    cq = jnp.broadcast_to(chunk[:, :, None], (bsz, seq, LANES))
    ck = chunk.reshape(bsz, 1, seq)
    vec = lambda n: pl.BlockSpec((1, n), lambda b, h, i, *_: (0, 0))
    grid_spec = pltpu.PrefetchScalarGridSpec(
        num_scalar_prefetch=4,
        grid=(bsz, heads, nq),
        in_specs=[
            pl.BlockSpec((None, tq, vd), lambda b, h, i, *_: (b, i, h)),
            pl.BlockSpec((None, seq, vd), lambda b, h, i, *_: (b, 0, h)),
            pl.BlockSpec((None, seq, vd), lambda b, h, i, *_: (b, 0, h)),
            pl.BlockSpec((None, tq, LANES), lambda b, h, i, *_: (b, i, 0)),
            pl.BlockSpec((None, 1, seq), lambda b, h, i, *_: (b, 0, 0)),
            vec(dh), vec(dh), vec(dh), vec(dh), vec(vd),
        ],
        out_specs=pl.BlockSpec((None, tq, vd), lambda b, h, i, *_: (b, i, h)),
        scratch_shapes=[pltpu.VMEM((2, tq, LANES), F32), pltpu.VMEM((2, tq, LANES), F32),
                        pltpu.VMEM((2, tq, vd), F32)],
    )
    return pl.pallas_call(
        functools.partial(_attn_kernel, tq=tq, tk=tk, nk=nk, dh=dh, lambda_init=lambda_init),
        grid_spec=grid_spec,
        out_shape=jax.ShapeDtypeStruct((bsz, seq, aw), BF16),
        compiler_params=_cparams(("arbitrary", "arbitrary", "arbitrary")),
        name="diff_attn",
    )(qmin, qmax, kmin, kmax, q, k, v, cq, ck,
      lq1.reshape(1, dh), lk1.reshape(1, dh), lq2.reshape(1, dh), lk2.reshape(1, dh),
      subln_g.reshape(1, vd))


def _outproj_kernel(*refs, w, with_router):
    if with_router:
        (lru_ref, att_ref, x_ref, g1_ref, sc_ref, sh_ref, ln_ref, wo_ref, rt_ref,
         x1_ref, h2_ref, route_ref, cnt_ref) = refs
    else:
        (lru_ref, att_ref, x_ref, g1_ref, sc_ref, sh_ref, ln_ref, wo_ref,
         x1_ref, h2_ref) = refs
    y = (jnp.dot(lru_ref[...], wo_ref[:w, :], preferred_element_type=F32)
         + jnp.dot(att_ref[...], wo_ref[w:, :], preferred_element_type=F32))
    x1 = x_ref[...] + g1_ref[...] * y
    x1_ref[...] = x1
    h2 = _rmsnorm(x1, ln_ref[...]) * (1.0 + sc_ref[...]) + sh_ref[...]
    h2_ref[...] = h2.astype(BF16)
    if with_router:
        def split(v):
            hi = v.astype(BF16)
            return hi, (v - hi.astype(F32)).astype(BF16)

        h_hi, h_lo = split(h2)
        r_hi, r_lo = split(rt_ref[...])
        nt_dims = (((1,), (1,)), ((), ()))
        dot_nt = lambda a, b: lax.dot_general(a, b, nt_dims, preferred_element_type=F32)
        logits = dot_nt(r_hi, h_hi) + (dot_nt(r_hi, h_lo) + dot_nt(r_lo, h_hi))
        n_rows, tm = logits.shape
        row = lax.broadcasted_iota(jnp.int32, logits.shape, 0)
        lg = jnp.where(row < N_EXPERTS, logits, -jnp.inf)
        m1 = jnp.max(lg, axis=0, keepdims=True)
        i1 = jnp.min(jnp.where(lg == m1, row, n_rows), axis=0, keepdims=True)
        lg2 = jnp.where(row == i1, -jnp.inf, lg)
        m2 = jnp.max(lg2, axis=0, keepdims=True)
        i2 = jnp.min(jnp.where(lg2 == m2, row, n_rows), axis=0, keepdims=True)
        e2 = jnp.exp(m2 - m1)
        w1 = 1.0 / (1.0 + e2)
        w2 = e2 / (1.0 + e2)

        onehot = jnp.where(row == i1, 1.0, jnp.where(row == i2, 1.0, 0.0))
        tri = (lax.broadcasted_iota(jnp.int32, (tm, tm), 0)
               < lax.broadcasted_iota(jnp.int32, (tm, tm), 1))
        prefix = jnp.dot(onehot.astype(BF16), jnp.where(tri, 1.0, 0.0).astype(BF16),
                         preferred_element_type=F32)
        count = jnp.sum(onehot, axis=1, keepdims=True)
        seg_len = jnp.floor((count + (MOE_SEG - 1)) * (1.0 / MOE_SEG)) * MOE_SEG
        seg_off = jnp.zeros_like(seg_len)
        for e in range(N_EXPERTS - 1):
            seg_off = seg_off + jnp.where(row[:, :1] > e, seg_len[e:e + 1, :], 0.0)
        local = prefix + seg_off
        pos1 = jnp.sum(jnp.where(row == i1, local, 0.0), axis=0, keepdims=True)
        pos2 = jnp.sum(jnp.where(row == i2, local, 0.0), axis=0, keepdims=True)
        cnt_ref[...] = jnp.broadcast_to(count, cnt_ref.shape)
        fields = (i1.astype(F32), i2.astype(F32), pos1, pos2, w1, w2)
        field_row = lax.broadcasted_iota(jnp.int32, route_ref.shape, 0)
        route = jnp.zeros(route_ref.shape, F32)
        for n, val in enumerate(fields):
            route = jnp.where(field_row == n, val, route)
        route_ref[...] = route


def _outproj(lru, att, x, mod_l, ln_g, w_out_b, router=None, *, tm=512):
    bsz, seq, d = x.shape
    w = lru.shape[-1]
    with_router = router is not None
    row = lambda k: pl.BlockSpec((None, None, 1, d), lambda b, i: (b, k, 0, 0))
    tok = lambda n: pl.BlockSpec((None, tm, n), lambda b, i: (b, i, 0))
    in_specs = [tok(w), tok(w), tok(d), row(2), row(4), row(3),
                pl.BlockSpec((1, d), lambda b, i: (0, 0)),
                pl.BlockSpec((d, d), lambda b, i: (0, 0))]
    args = [lru, att, x, mod_l, mod_l, mod_l, ln_g.reshape(1, d), w_out_b]
    out_specs = [tok(d), tok(d)]
    out_shape = [jax.ShapeDtypeStruct((bsz, seq, d), F32),
                 jax.ShapeDtypeStruct((bsz, seq, d), BF16)]
    scratch = []
    if with_router:
        nt = seq // tm
        e_rows = 2 * SUBLANES
        rt = jnp.zeros((e_rows, d), F32).at[:N_EXPERTS].set(router.T)
        in_specs.append(pl.BlockSpec((e_rows, d), lambda b, i: (0, 0)))
        args.append(rt)
        assert tm == MOE_TB
        out_specs.append(pl.BlockSpec((None, SUBLANES, tm), lambda b, i: (b * nt + i, 0, 0)))
        out_shape.append(jax.ShapeDtypeStruct((bsz * nt, SUBLANES, tm), F32))
        out_specs.append(pl.BlockSpec((e_rows, LANES), lambda b, i: (b * nt + i, 0)))
        out_shape.append(jax.ShapeDtypeStruct((bsz * nt * e_rows, LANES), F32))
    return pl.pallas_call(
        functools.partial(_outproj_kernel, w=w, with_router=with_router),
        grid=(bsz, seq // tm),
        in_specs=in_specs, out_specs=out_specs, out_shape=out_shape,
        scratch_shapes=scratch,
        compiler_params=_cparams(("arbitrary", "arbitrary")),
        name="outproj_router" if with_router else "outproj",
    )(*args)


def _ffn_kernel(*refs, final_norm):
    if final_norm:
        h_ref, wg_ref, wu_ref, wd_ref, x1_ref, g2_ref, fg_ref, o_ref, acc_ref = refs
    else:
        h_ref, wg_ref, wu_ref, wd_ref, x1_ref, g2_ref, o_ref, acc_ref = refs
    j = pl.program_id(2)

    @pl.when(j == 0)
    def _():
        acc_ref[...] = jnp.zeros_like(acc_ref)

    h = h_ref[...]
    act = (jax.nn.silu(jnp.dot(h, wg_ref[...], preferred_element_type=F32))
           * jnp.dot(h, wu_ref[...], preferred_element_type=F32))
    acc_ref[...] += jnp.dot(act.astype(BF16), wd_ref[...], preferred_element_type=F32)

    @pl.when(j == pl.num_programs(2) - 1)
    def _():
        out = x1_ref[...] + g2_ref[...] * acc_ref[...]
        if final_norm:
            out = _rmsnorm(out, fg_ref[...])
        o_ref[...] = out


def _ffn(h2, wg, wu, wd, x1, mod_l, final_g=None, *, tm=1024, tf=512):
    bsz, seq, d = x1.shape
    ff = wg.shape[1]
    final_norm = final_g is not None
    tok = lambda n: pl.BlockSpec((None, tm, n), lambda b, i, j: (b, i, 0))
    in_specs = [tok(d),
                pl.BlockSpec((d, tf), lambda b, i, j: (0, j)),
                pl.BlockSpec((d, tf), lambda b, i, j: (0, j)),
                pl.BlockSpec((tf, d), lambda b, i, j: (j, 0)),
                tok(d),
                pl.BlockSpec((None, None, 1, d), lambda b, i, j: (b, 5, 0, 0))]
    args = [h2, wg, wu, wd, x1, mod_l]
    if final_norm:
        in_specs.append(pl.BlockSpec((1, d), lambda b, i, j: (0, 0)))
        args.append(final_g.reshape(1, d))
    return pl.pallas_call(
        functools.partial(_ffn_kernel, final_norm=final_norm),
        grid=(bsz, seq // tm, ff // tf),
        in_specs=in_specs,
        out_specs=tok(d),
        out_shape=jax.ShapeDtypeStruct((bsz, seq, d), F32),
        scratch_shapes=[pltpu.VMEM((tm, d), F32)],
        compiler_params=_cparams(("arbitrary",) * 3),
        name="dense_ffn",
    )(*args)


MOE_TB = 512
MOE_SEG = 16
MOE_TF = 512
MOE_LR = 2 * MOE_TB + N_EXPERTS * MOE_SEG


def _moe_plan(cnt, *, m_tok):
    i32 = jnp.int32
    nb = m_tok // MOE_TB
    n_e = N_EXPERTS
    rows = -(-(2 * m_tok + nb * n_e * MOE_SEG + n_e * MOE_TF) // MOE_TF) * MOE_TF
    ntf = rows // MOE_TF
    n = cnt.reshape(nb, -1, LANES)[:, :n_e, 0].astype(i32)
    seg_n = (n + MOE_SEG - 1) // MOE_SEG
    seg_src = jnp.cumsum(seg_n, axis=1) - seg_n
    used = jnp.sum(seg_n, axis=0)
    per_tile = MOE_TF // MOE_SEG
    gsz = (used + per_tile - 1) // per_tile * per_tile
    gend = jnp.cumsum(gsz)
    goff = gend - gsz
    seg_dst = goff[None, :] + jnp.cumsum(seg_n, axis=0) - seg_n
    total_tiles = gend[-1] // per_tile
    tile = jnp.arange(ntf, dtype=i32)
    f_valid = (tile < total_tiles).astype(i32)
    f_exp = jnp.minimum(
        jnp.sum(gend[None, :] <= (jnp.minimum(tile, total_tiles - 1) * per_tile)[:, None],
                axis=1).astype(i32), n_e - 1)
    tail_dst = jnp.concatenate([goff + used, gend[-1:]])
    tail_n = jnp.concatenate([gsz - used, rows // MOE_SEG - gend[-1:]])
    return dict(seg_n=seg_n.reshape(-1), seg_src=seg_src.reshape(-1),
                seg_dst=seg_dst.reshape(-1), blk_n=jnp.sum(seg_n, axis=1),
                tail_dst=tail_dst, tail_n=tail_n,
                f_exp=f_exp, f_valid=f_valid, rows=rows, ntf=ntf, nb=nb)


def _seg_rows(unit):
    return pl.ds(pl.multiple_of(unit * MOE_SEG, MOE_SEG), MOE_SEG)


def _wait_segments(sem, buf_ref, n):
    def body(_, carry):
        pltpu.make_async_copy(buf_ref.at[pl.ds(0, MOE_SEG)], buf_ref.at[pl.ds(0, MOE_SEG)],
                              sem).wait()
        return carry
    lax.fori_loop(0, n, body, 0)


def _moe_scatter_kernel(seg_n_ref, seg_src_ref, seg_dst_ref, tail_dst_ref, tail_n_ref,
                        h_ref, p1_ref, p2_ref, xs_hbm, buf_ref, zero_ref, sem, tail_sem,
                        issued_ref):
    tb = pl.program_id(0)
    nb = pl.num_programs(0)
    slot = tb % 2
    buf = buf_ref.at[slot]

    @pl.when(tb >= 2)
    def _():
        _wait_segments(sem.at[slot], buf, issued_ref[slot])

    rows = lax.broadcasted_iota(jnp.int32, (MOE_LR, MOE_TB), 0)
    sel = jnp.where(p1_ref[...] == rows, 1.0, jnp.where(p2_ref[...] == rows, 1.0, 0.0))
    buf[...] = jnp.dot(sel.astype(BF16), h_ref[...],
                       preferred_element_type=F32).astype(buf_ref.dtype)

    issued = 0
    for e in range(N_EXPERTS):
        k = tb * N_EXPERTS + e
        n, src, dst = seg_n_ref[k], seg_src_ref[k], seg_dst_ref[k]

        def copy_seg(g, carry):
            pltpu.make_async_copy(buf.at[_seg_rows(src + g)], xs_hbm.at[_seg_rows(dst + g)],
                                  sem.at[slot]).start()
            return carry
        lax.fori_loop(0, n, copy_seg, 0)
        issued = issued + n
    issued_ref[slot] = issued

    @pl.when(tb == nb - 1)
    def _():
        zero_ref[...] = jnp.zeros_like(zero_ref)
        n_tail = 0
        for e in range(N_EXPERTS + 1):
            n, dst = tail_n_ref[e], tail_dst_ref[e]

            def zero_seg(g, carry):
                pltpu.make_async_copy(zero_ref, xs_hbm.at[_seg_rows(dst + g)], tail_sem).start()
                return carry
            lax.fori_loop(0, n, zero_seg, 0)
            n_tail = n_tail + n
        _wait_segments(tail_sem, zero_ref, n_tail)
        _wait_segments(sem.at[slot], buf, issued_ref[slot])

        @pl.when(nb >= 2)
        def _():
            _wait_segments(sem.at[1 - slot], buf, issued_ref[1 - slot])


def _moe_scatter(h2, plan, p1, p2):
    m_tok, d = h2.shape
    nb = plan['nb']
    tokrow = lambda: pl.BlockSpec((None, 1, MOE_TB), lambda t, *_: (t, 0, 0))
    grid_spec = pltpu.PrefetchScalarGridSpec(
        num_scalar_prefetch=5,
        grid=(nb,),
        in_specs=[pl.BlockSpec((MOE_TB, d), lambda t, *_: (t, 0)), tokrow(), tokrow()],
        out_specs=pl.BlockSpec(memory_space=pl.ANY),
        scratch_shapes=[pltpu.VMEM((2, MOE_LR, d), BF16), pltpu.VMEM((MOE_SEG, d), BF16),
                        pltpu.SemaphoreType.DMA((2,)), pltpu.SemaphoreType.DMA(()),
                        pltpu.SMEM((2,), jnp.int32)],
    )
    return pl.pallas_call(
        _moe_scatter_kernel,
        grid_spec=grid_spec,
        out_shape=jax.ShapeDtypeStruct((plan['rows'], d), BF16),
        compiler_params=_cparams(("arbitrary",)),
        name="moe_scatter",
    )(plan['seg_n'], plan['seg_src'], plan['seg_dst'], plan['tail_dst'], plan['tail_n'],
      h2, p1.reshape(nb, 1, MOE_TB), p2.reshape(nb, 1, MOE_TB))


def _moe_ffn_kernel(exp_ref, valid_ref, xs_ref, wg_ref, wu_ref, wd_ref, o_ref, acc_ref):
    n = pl.program_id(0)
    j = pl.program_id(1)

    @pl.when(valid_ref[n] == 0)
    def _():
        o_ref[...] = jnp.zeros_like(o_ref)

    @pl.when(valid_ref[n] != 0)
    def _():
        @pl.when(j == 0)
        def _():
            acc_ref[...] = jnp.zeros_like(acc_ref)

        h = xs_ref[...]
        act = (jax.nn.silu(jnp.dot(h, wg_ref[...], preferred_element_type=F32))
               * jnp.dot(h, wu_ref[...], preferred_element_type=F32))
        acc_ref[...] += jnp.dot(act.astype(BF16), wd_ref[...], preferred_element_type=F32)

        @pl.when(j == pl.num_programs(1) - 1)
        def _():
            o_ref[...] = acc_ref[...].astype(o_ref.dtype)


def _moe_ffn(xs, wg, wu, wd, plan, *, tf=1536):
    rows, d = xs.shape
    ff = wg.shape[2]
    tf = min(tf, ff)
    nj = ff // tf

    def ff_tile(n, j, v):
        return j * v[n] + (nj - 1) * (1 - v[n])

    grid_spec = pltpu.PrefetchScalarGridSpec(
        num_scalar_prefetch=2,
        grid=(plan['ntf'], nj),
        in_specs=[pl.BlockSpec((MOE_TF, d), lambda n, j, e, v: (n, 0)),
                  pl.BlockSpec((None, d, tf), lambda n, j, e, v: (e[n], 0, ff_tile(n, j, v))),
                  pl.BlockSpec((None, d, tf), lambda n, j, e, v: (e[n], 0, ff_tile(n, j, v))),
                  pl.BlockSpec((None, tf, d), lambda n, j, e, v: (e[n], ff_tile(n, j, v), 0))],
        out_specs=pl.BlockSpec((MOE_TF, d), lambda n, j, e, v: (n, 0)),
        scratch_shapes=[pltpu.VMEM((MOE_TF, d), F32)],
    )
    return pl.pallas_call(
        _moe_ffn_kernel,
        grid_spec=grid_spec,
        out_shape=jax.ShapeDtypeStruct((rows, d), BF16),
        compiler_params=_cparams(("arbitrary", "arbitrary")),
        name="moe_ffn",
    )(plan['f_exp'], plan['f_valid'], xs, wg, wu, wd)


def _moe_combine_kernel(*refs, final_norm):
    if final_norm:
        (seg_n_ref, seg_src_ref, seg_dst_ref, blk_n_ref, ye_hbm, p1_ref, p2_ref, w1_ref, w2_ref,
         x1_ref, g2_ref, fg_ref, o_ref, buf_ref, sem) = refs
    else:
        (seg_n_ref, seg_src_ref, seg_dst_ref, blk_n_ref, ye_hbm, p1_ref, p2_ref, w1_ref, w2_ref,
         x1_ref, g2_ref, o_ref, buf_ref, sem) = refs
    tb = pl.program_id(0)
    nb = pl.num_programs(0)
    slot = tb % 2

    def fetch(block, into):
        for e in range(N_EXPERTS):
            k = block * N_EXPERTS + e
            n, src, dst = seg_n_ref[k], seg_src_ref[k], seg_dst_ref[k]

            def copy_seg(g, carry):
                pltpu.make_async_copy(ye_hbm.at[_seg_rows(dst + g)],
                                      buf_ref.at[into, _seg_rows(src + g)], sem.at[into]).start()
                return carry
            lax.fori_loop(0, n, copy_seg, 0)

    @pl.when(tb == 0)
    def _():
        fetch(tb, slot)

    @pl.when(tb + 1 < nb)
    def _():
        fetch(tb + 1, 1 - slot)

    buf = buf_ref.at[slot]
    _wait_segments(sem.at[slot], buf, blk_n_ref[tb])

    def clear(g, carry):
        buf[_seg_rows(g), :] = jnp.zeros((MOE_SEG, buf.shape[1]), buf.dtype)
        return carry
    lax.fori_loop(blk_n_ref[tb], MOE_LR // MOE_SEG, clear, 0)

    ye = buf[...]
    reps = MOE_LR // LANES
    cols = lax.broadcasted_iota(jnp.int32, (MOE_TB, MOE_LR), 1)

    def unsort(p_ref):
        hit = jnp.tile(p_ref[...], (1, reps)) == cols
        return jnp.dot(jnp.where(hit, 1.0, 0.0).astype(BF16), ye, preferred_element_type=F32)

    lanes = x1_ref.shape[1] // LANES
    y = (jnp.tile(w1_ref[...], (1, lanes)) * unsort(p1_ref)
         + jnp.tile(w2_ref[...], (1, lanes)) * unsort(p2_ref))
    out = x1_ref[...] + g2_ref[...] * y
    if final_norm:
        out = _rmsnorm(out, fg_ref[...])
    o_ref[...] = out


def _moe_combine(ye, plan, route_t, x1, mod_l, final_g=None):
    bsz, seq, d = x1.shape
    m_tok = bsz * seq
    nb = plan['nb']
    blocks_per_seq = seq // MOE_TB
    final_norm = final_g is not None
    rep = lambda a, dt: jnp.broadcast_to(a.reshape(m_tok, 1).astype(dt), (m_tok, LANES))
    tokrep = lambda: pl.BlockSpec((MOE_TB, LANES), lambda t, *_: (t, 0))
    in_specs = [pl.BlockSpec(memory_space=pl.ANY), tokrep(), tokrep(), tokrep(), tokrep(),
                pl.BlockSpec((MOE_TB, d), lambda t, *_: (t, 0)),
                pl.BlockSpec((None, None, 1, d), lambda t, *_: (t // blocks_per_seq, 5, 0, 0))]
    args = [ye, rep(route_t[:, 2, :], jnp.int32), rep(route_t[:, 3, :], jnp.int32),
            rep(route_t[:, 4, :], F32), rep(route_t[:, 5, :], F32),
            x1.reshape(m_tok, d), mod_l]
    if final_norm:
        in_specs.append(pl.BlockSpec((1, d), lambda t, *_: (0, 0)))
        args.append(final_g.reshape(1, d))
    grid_spec = pltpu.PrefetchScalarGridSpec(
        num_scalar_prefetch=4,
        grid=(nb,),
        in_specs=in_specs,
        out_specs=pl.BlockSpec((MOE_TB, d), lambda t, *_: (t, 0)),
        scratch_shapes=[pltpu.VMEM((2, MOE_LR, d), BF16), pltpu.SemaphoreType.DMA((2,))],
    )
    out = pl.pallas_call(
        functools.partial(_moe_combine_kernel, final_norm=final_norm),
        grid_spec=grid_spec,
        out_shape=jax.ShapeDtypeStruct((m_tok, d), F32),
        compiler_params=_cparams(("arbitrary",)),
        name="moe_combine",
    )(plan['seg_n'], plan['seg_src'], plan['seg_dst'], plan['blk_n'], *args)
    return out.reshape(bsz, seq, d)


def _moe(h2, route_t, cnt, x1, mod_l, wg, wu, wd, final_g=None):
    bsz, seq, d = x1.shape
    m_tok = bsz * seq
    plan = _moe_plan(cnt, m_tok=m_tok)
    p1 = route_t[:, 2, :].astype(jnp.int32)
    p2 = route_t[:, 3, :].astype(jnp.int32)
    xs = _moe_scatter(h2.reshape(m_tok, d), plan, p1, p2)
    ye = _moe_ffn(xs, wg, wu, wd, plan)
    return _moe_combine(ye, plan, route_t, x1, mod_l, final_g)


def kernel(x, c, positions, ada_w, ada_b, ln1_g, ln2_g, w_in, conv_w, conv_b, gate_a_w, gate_a_b, gate_x_w, gate_x_b, lru_lambda, lam_q1, lam_k1, lam_q2, lam_k2, subln_g, w_out, ffn_w_gate, ffn_w_up, ffn_w_down, moe_router, moe_w_gate, moe_w_up, moe_w_down, final_g):
    depth = ada_w.shape[0]
    lru_w = conv_w.shape[-1]
    vd = subln_g.shape[-1]
    dh = vd // 2
    attn_w = DIFF_HEADS * vd

    mod = _modulation(c, ada_w, ada_b)
    cos_t, sin_t = _rope_tables(positions, dh)
    for l in range(depth):
        lambda_init = 0.8 - 0.6 * math.exp(-0.3 * l)
        mod_l = mod[l]
        xy, q, k, v = _inproj(x, mod_l, ln1_g[l], w_in[l].astype(BF16), cos_t, sin_t,
                              lru_w=lru_w, attn_w=attn_w, dh=dh)
        lru = _lru(xy, conv_w[l], conv_b[l], gate_a_w[l], gate_a_b[l], gate_x_w[l],
                   gate_x_b[l], lru_lambda[l])
        att = _attention(q, k, v, positions, lam_q1[l], lam_k1[l], lam_q2[l], lam_k2[l],
                         subln_g[l], lambda_init, dh=dh)
        fg = final_g if l == depth - 1 else None
        j = l // 2
        if l % 2 == 0:
            x1, h2 = _outproj(lru, att, x, mod_l, ln2_g[l], w_out[l].astype(BF16))
            x = _ffn(h2, ffn_w_gate[j].astype(BF16), ffn_w_up[j].astype(BF16),
                     ffn_w_down[j].astype(BF16), x1, mod_l, final_g=fg)
        else:
            x1, h2, route, cend = _outproj(lru, att, x, mod_l, ln2_g[l], w_out[l].astype(BF16),
                                           router=moe_router[j])
            x = _moe(h2, route, cend, x1, mod_l, moe_w_gate[j].astype(BF16),
                     moe_w_up[j].astype(BF16), moe_w_down[j].astype(BF16), final_g=fg)
    return x
```

```python
import functools
import math

import jax
import jax.numpy as jnp
from jax import lax
from jax.experimental import pallas as pl
from jax.experimental.pallas import tpu as pltpu

F32 = jnp.float32
BF16 = jnp.bfloat16
HIGHEST = lax.Precision.HIGHEST

CHUNK = 64
LRU_BLOCKS = 8
CONV_W = 4
RG_C = 8.0
DIFF_HEADS = 4
ROPE_THETA = 10000.0
N_EXPERTS = 8
EPS = 1e-6
LANES = 128
SUBLANES = 8
VMEM_LIMIT = 56 * 1024 * 1024
MASK_VALUE = -0.5 * float(jnp.finfo(jnp.float32).max)


def _cparams(sem):
    return pltpu.CompilerParams(dimension_semantics=sem, vmem_limit_bytes=VMEM_LIMIT)


def _rmsnorm(x, g):
    return x * lax.rsqrt(jnp.mean(x * x, axis=-1, keepdims=True) + EPS) * g


def _mod_kernel(c_ref, w_ref, b_ref, o_ref):
    c = c_ref[...]
    s = c * jax.nn.sigmoid(c)
    o_ref[...] = jnp.dot(s, w_ref[...], precision=HIGHEST,
                         preferred_element_type=F32) + b_ref[...]


def _modulation(c, ada_w, ada_b, tn=1024):
    depth, d, n = ada_w.shape
    bsz = c.shape[0]
    rows = -(-bsz // SUBLANES) * SUBLANES
    c_pad = jnp.zeros((rows, d), F32).at[:bsz].set(c)
    out = pl.pallas_call(
        _mod_kernel,
        grid=(depth, n // tn),
        in_specs=[
            pl.BlockSpec((rows, d), lambda l, j: (0, 0)),
            pl.BlockSpec((None, d, tn), lambda l, j: (l, 0, j)),
            pl.BlockSpec((None, 1, tn), lambda l, j: (l, 0, j)),
        ],
        out_specs=pl.BlockSpec((None, rows, tn), lambda l, j: (l, 0, j)),
        out_shape=jax.ShapeDtypeStruct((depth, rows, n), F32),
        compiler_params=_cparams(("arbitrary", "arbitrary")),
        name="adaln_mod",
    )(c_pad, ada_w, ada_b.reshape(depth, 1, n))
    return out[:, :bsz].reshape(depth, bsz, 6, 1, d)


def _rope_table_kernel(pos_ref, inv_ref, cos_ref, sin_ref):
    ang = pos_ref[...].astype(F32) * inv_ref[...]
    cos_ref[...] = jnp.cos(ang)
    sin_ref[...] = jnp.sin(ang)


def _rope_tables(positions, dh):
    n_freq = dh // 2
    per_row = LANES // n_freq
    tok = positions.size
    rows = tok // per_row
    inv = ROPE_THETA ** (-jnp.arange(0, dh, 2, dtype=F32) / dh)
    pos_x = jnp.repeat(positions.reshape(-1), n_freq).reshape(rows, LANES)
    inv_x = jnp.tile(inv, per_row).reshape(1, LANES)
    tr = min(rows, 1024)
    cos, sin = pl.pallas_call(
        _rope_table_kernel,
        grid=(rows // tr,),
        in_specs=[pl.BlockSpec((tr, LANES), lambda i: (i, 0)),
                  pl.BlockSpec((1, LANES), lambda i: (0, 0))],
        out_specs=[pl.BlockSpec((tr, LANES), lambda i: (i, 0))] * 2,
        out_shape=[jax.ShapeDtypeStruct((rows, LANES), F32)] * 2,
        compiler_params=_cparams(("arbitrary",)),
        name="rope_tables",
    )(pos_x, inv_x)
    cos = cos.reshape(tok, n_freq)
    sin = sin.reshape(tok, n_freq)
    reps = LANES // dh
    cos_t = jnp.tile(jnp.concatenate([cos, cos], axis=-1), (1, reps))
    sin_t = jnp.tile(jnp.concatenate([-sin, sin], axis=-1), (1, reps))
    return cos_t, sin_t


def _inproj_kernel(x_ref, sc_ref, sh_ref, g_ref, w_ref, cos_ref, sin_ref,
                   cw_ref, cb_ref, wg_ref, bg_ref, lam_ref,
                   lru_ref, q_ref, k_ref, v_ref, xpad_ref, h_ref, *, lru_w, attn_w, dh):
    @pl.when(pl.program_id(1) == 0)
    def _():
        xpad_ref[0:SUBLANES, :] = jnp.zeros((SUBLANES, lru_w), F32)
        h_ref[...] = jnp.zeros_like(h_ref)

    h = _rmsnorm(x_ref[...], g_ref[...]) * (1.0 + sc_ref[...]) + sh_ref[...]
    hb = h.astype(BF16)
    lru2 = 2 * lru_w
    xy = jnp.dot(hb, w_ref[:, :lru2], preferred_element_type=F32)

    reps = attn_w // LANES
    cos = jnp.tile(cos_ref[...], (1, reps))
    sin = jnp.tile(sin_ref[...], (1, reps))
    lane = lax.broadcasted_iota(jnp.int32, cos.shape, 1)
    first_half = (lane % dh) < (dh // 2)

    def rope(t):
        fwd = pltpu.roll(t, attn_w - dh // 2, axis=1)
        bwd = pltpu.roll(t, dh // 2, axis=1)
        return t * cos + jnp.where(first_half, fwd, bwd) * sin

    q = jnp.dot(hb, w_ref[:, lru2:lru2 + attn_w], preferred_element_type=F32)
    q_ref[...] = (rope(q) * (dh ** -0.5 * math.log2(math.e))).astype(BF16)
    k = jnp.dot(hb, w_ref[:, lru2 + attn_w:lru2 + 2 * attn_w], preferred_element_type=F32)
    k_ref[...] = rope(k).astype(BF16)
    v = jnp.dot(hb, w_ref[:, lru2 + 2 * attn_w:], preferred_element_type=F32)
    v_ref[...] = v.astype(BF16)

    u, gates = _lru_conv_gates(xy[:, :lru_w], cw_ref, cb_ref, wg_ref, bg_ref, xpad_ref)
    a, bt = _lru_coeffs(u, gates, lam_ref)
    lru_ref[...] = _lru_scan(a, bt, xy[:, lru_w:], h_ref).astype(lru_ref.dtype)


def _inproj(x, mod_l, ln_g, w_in_b, cos_t, sin_t, conv_w, conv_b, wa, ba, wx, bx, lam, *,
            attn_w, dh, tm=512):
    bsz, seq, d = x.shape
    nt = seq // tm
    d_in = w_in_b.shape[1]
    lru_w = conv_w.shape[-1]
    wg = jnp.concatenate([_block_diag(wa), _block_diag(wx)], axis=1).astype(BF16)
    bg = jnp.concatenate([ba, bx]).reshape(1, 2 * lru_w)
    row = lambda k: pl.BlockSpec((None, None, 1, d), lambda b, i: (b, k, 0, 0))
    tok = lambda w: pl.BlockSpec((None, tm, w), lambda b, i: (b, i, 0))
    const = lambda shape: pl.BlockSpec(shape, lambda b, i: (0,) * len(shape))
    return pl.pallas_call(
        functools.partial(_inproj_kernel, lru_w=lru_w, attn_w=attn_w, dh=dh),
        grid=(bsz, nt),
        in_specs=[
            tok(d), row(1), row(0), const((1, d)), const((d, d_in)),
            pl.BlockSpec((tm, LANES), lambda b, i: (b * nt + i, 0)),
            pl.BlockSpec((tm, LANES), lambda b, i: (b * nt + i, 0)),
            const((CONV_W, lru_w)), const((1, lru_w)), const((lru_w, 2 * lru_w)),
            const((1, 2 * lru_w)), const((1, lru_w)),
        ],
        out_specs=[tok(lru_w), tok(attn_w), tok(attn_w), tok(attn_w)],
        out_shape=[jax.ShapeDtypeStruct((bsz, seq, lru_w), BF16),
                   jax.ShapeDtypeStruct((bsz, seq, attn_w), BF16),
                   jax.ShapeDtypeStruct((bsz, seq, attn_w), BF16),
                   jax.ShapeDtypeStruct((bsz, seq, attn_w), BF16)],
        scratch_shapes=[pltpu.VMEM((tm + SUBLANES, lru_w), F32), pltpu.VMEM((1, lru_w), F32)],
        compiler_params=_cparams(("arbitrary", "arbitrary")),
        name="inproj_lru",
    )(x, mod_l, mod_l, ln_g.reshape(1, d), w_in_b, cos_t, sin_t,
      conv_w, conv_b.reshape(1, lru_w), wg, bg, lam.reshape(1, lru_w))


def _gelu_tanh(x):
    return 0.5 * x * (1.0 + jnp.tanh(math.sqrt(2.0 / math.pi) * (x + 0.044715 * (x * x * x))))


def _lru_conv_gates(xr, cw_ref, cb_ref, wg_ref, bg_ref, xpad_ref):
    t, w = xr.shape
    xpad_ref[SUBLANES:SUBLANES + t, :] = xr
    u = cb_ref[...]
    for j in range(CONV_W):
        off = SUBLANES - (CONV_W - 1) + j
        u = u + cw_ref[j:j + 1, :] * xpad_ref[off:off + t, :]
    xpad_ref[0:SUBLANES, :] = xpad_ref[t:t + SUBLANES, :]
    gates = jnp.dot(u.astype(BF16), wg_ref[...], preferred_element_type=F32) + bg_ref[...]
    return u, gates


def _lru_coeffs(u, gates, lam_ref):
    w = u.shape[1]
    r = jax.nn.sigmoid(gates[:, :w])
    ig = jax.nn.sigmoid(gates[:, w:])
    neg_lam = -lam_ref[...]
    softplus = jnp.maximum(neg_lam, 0.0) + jnp.log1p(jnp.exp(-jnp.abs(neg_lam)))
    log_a = (-RG_C) * r * softplus
    a = jnp.exp(log_a)
    return a, jnp.sqrt(1.0 - a * a) * (ig * u)


def _lru_scan(a, bt, yr, h_ref):
    t, w = a.shape
    groups = t // SUBLANES
    a = a.reshape(groups, SUBLANES, w)
    bt = bt.reshape(groups, SUBLANES, w)
    sub = lax.broadcasted_iota(jnp.int32, a.shape, 1)
    shift = 1
    while shift < SUBLANES:
        keep = sub >= shift
        a_prev = jnp.where(keep, pltpu.roll(a, shift, axis=1), 1.0)
        b_prev = jnp.where(keep, pltpu.roll(bt, shift, axis=1), 0.0)
        bt = a * b_prev + bt
        a = a * a_prev
        shift *= 2
    carry = h_ref[...]
    rows = []
    for g in range(groups):
        hg = a[g] * carry + bt[g]
        rows.append(hg)
        carry = hg[SUBLANES - 1:SUBLANES, :]
    h_ref[...] = carry
    return jnp.concatenate(rows, axis=0) * _gelu_tanh(yr)


def _block_diag(wb):
    n, bw, _ = wb.shape
    eye = jnp.eye(n, dtype=wb.dtype)
    return jnp.einsum('nhk,nm->nhmk', wb, eye).reshape(n * bw, n * bw)


def _attn_kernel(qmin_ref, qmax_ref, kmin_ref, kmax_ref,
                 q_ref, k_ref, v_ref, cq_ref, ck_ref, lq1_ref, lk1_ref, lq2_ref, lk2_ref,
                 g_ref, o_ref, m_ref, l_ref, acc_ref, *, tq, tk, nk, dh, lambda_init):
    b = pl.program_id(0)
    i = pl.program_id(2)
    m_ref[...] = jnp.full(m_ref.shape, -jnp.inf, F32)
    l_ref[...] = jnp.zeros(l_ref.shape, F32)
    acc_ref[...] = jnp.zeros(acc_ref.shape, F32)

    q = q_ref[...]
    lane = lax.broadcasted_iota(jnp.int32, q.shape, 1)
    qc = (jnp.where(lane < dh, q, jnp.zeros_like(q)), jnp.where(lane >= dh, q, jnp.zeros_like(q)))
    q_lo = qmin_ref[b, i]
    q_hi = qmax_ref[b, i]

    def process(j, masked):
        start = pl.multiple_of(j * tk, tk)
        kb = k_ref[pl.ds(start, tk), :]
        vb = v_ref[pl.ds(start, tk), :]
        if masked:
            ck = ck_ref[:, pl.ds(start, tk)]
            visible = ck <= jnp.tile(cq_ref[...], (1, tk // LANES))
        for c in range(2):
            s = lax.dot_general(qc[c], kb, (((1,), (1,)), ((), ())),
                                preferred_element_type=F32)
            if masked:
                s = jnp.where(visible, s, MASK_VALUE)
            m_prev = m_ref[c]
            m_new = jnp.maximum(m_prev, jnp.max(s, axis=-1, keepdims=True))
            alpha = jnp.exp2(m_prev - m_new)
            p = jnp.exp2(s - jnp.tile(m_new, (1, tk // LANES)))
            p_lanes = p[:, :LANES]
            for t in range(1, tk // LANES):
                p_lanes = p_lanes + p[:, t * LANES:(t + 1) * LANES]
            l_ref[c] = alpha * l_ref[c] + p_lanes
            acc_ref[c] = alpha * acc_ref[c] + jnp.dot(p.astype(BF16), vb,
                                                      preferred_element_type=F32)
            m_ref[c] = m_new

    def body(j, carry):
        k_lo = kmin_ref[b, j]
        k_hi = kmax_ref[b, j]
        needed = k_lo <= q_hi
        needs_mask = k_hi > q_lo

        @pl.when(jnp.logical_and(needed, needs_mask))
        def _():
            process(j, True)

        @pl.when(jnp.logical_and(needed, jnp.logical_not(needs_mask)))
        def _():
            process(j, False)

        return carry

    lax.fori_loop(0, nk, body, 0)

    lam = (jnp.exp(jnp.sum(lq1_ref[...] * lk1_ref[...], keepdims=True))
           - jnp.exp(jnp.sum(lq2_ref[...] * lk2_ref[...], keepdims=True)) + lambda_init)
    l0 = jnp.sum(l_ref[0], axis=-1, keepdims=True)
    l1 = jnp.sum(l_ref[1], axis=-1, keepdims=True)
    o = acc_ref[0] / l0 - lam * (acc_ref[1] / l1)
    o_ref[...] = (_rmsnorm(o, g_ref[...]) * (1.0 - lambda_init)).astype(o_ref.dtype)


def _attention(q, k, v, positions, lq1, lk1, lq2, lk2, subln_g, lambda_init, *,
               dh, tq=1024, tk=1024):
    bsz, seq, aw = q.shape
    vd = 2 * dh
    heads = aw // vd
    nq, nk = seq // tq, seq // tk
    chunk = positions // CHUNK
    qmin = chunk.reshape(bsz, nq, tq).min(-1)
    qmax = chunk.reshape(bsz, nq, tq).max(-1)
    kmin = chunk.reshape(bsz, nk, tk).min(-1)
    kmax = chunk.reshape(bsz, nk, tk).max(-1)
    cq = jnp.broadcast_to(chunk[:, :, None], (bsz, seq, LANES))
    ck = chunk.reshape(bsz, 1, seq)
    vec = lambda n: pl.BlockSpec((1, n), lambda b, h, i, *_: (0, 0))
    grid_spec = pltpu.PrefetchScalarGridSpec(
        num_scalar_prefetch=4,
        grid=(bsz, heads, nq),
        in_specs=[
            pl.BlockSpec((None, tq, vd), lambda b, h, i, *_: (b, i, h)),
            pl.BlockSpec((None, seq, vd), lambda b, h, i, *_: (b, 0, h)),
            pl.BlockSpec((None, seq, vd), lambda b, h, i, *_: (b, 0, h)),
            pl.BlockSpec((None, tq, LANES), lambda b, h, i, *_: (b, i, 0)),
            pl.BlockSpec((None, 1, seq), lambda b, h, i, *_: (b, 0, 0)),
            vec(dh), vec(dh), vec(dh), vec(dh), vec(vd),
        ],
        out_specs=pl.BlockSpec((None, tq, vd), lambda b, h, i, *_: (b, i, h)),
        scratch_shapes=[pltpu.VMEM((2, tq, LANES), F32), pltpu.VMEM((2, tq, LANES), F32),
                        pltpu.VMEM((2, tq, vd), F32)],
    )
    return pl.pallas_call(
        functools.partial(_attn_kernel, tq=tq, tk=tk, nk=nk, dh=dh, lambda_init=lambda_init),
        grid_spec=grid_spec,
        out_shape=jax.ShapeDtypeStruct((bsz, seq, aw), BF16),
        compiler_params=_cparams(("arbitrary", "arbitrary", "arbitrary")),
        name="diff_attn",
    )(qmin, qmax, kmin, kmax, q, k, v, cq, ck,
      lq1.reshape(1, dh), lk1.reshape(1, dh), lq2.reshape(1, dh), lk2.reshape(1, dh),
      subln_g.reshape(1, vd))


def _outproj_router_kernel(lru_ref, att_ref, x_ref, g1_ref, sc_ref, sh_ref, ln_ref, wo_ref,
                           rt_ref, x1_ref, h2_ref, route_ref, cnt_ref, *, w):
    y = (jnp.dot(lru_ref[...], wo_ref[:w, :], preferred_element_type=F32)
         + jnp.dot(att_ref[...], wo_ref[w:, :], preferred_element_type=F32))
    x1 = x_ref[...] + g1_ref[...] * y
    x1_ref[...] = x1
    h2 = _rmsnorm(x1, ln_ref[...]) * (1.0 + sc_ref[...]) + sh_ref[...]
    h2_ref[...] = h2.astype(BF16)
    _route(h2, rt_ref, route_ref, cnt_ref)


def _route(h2, rt_ref, route_ref, cnt_ref):
    def split(v):
        hi = v.astype(BF16)
        return hi, (v - hi.astype(F32)).astype(BF16)

    h_hi, h_lo = split(h2)
    r_hi, r_lo = split(rt_ref[...])
    nt_dims = (((1,), (1,)), ((), ()))
    dot_nt = lambda a, b: lax.dot_general(a, b, nt_dims, preferred_element_type=F32)
    logits = dot_nt(r_hi, h_hi) + (dot_nt(r_hi, h_lo) + dot_nt(r_lo, h_hi))
    n_rows, tm = logits.shape
    row = lax.broadcasted_iota(jnp.int32, logits.shape, 0)
    lg = jnp.where(row < N_EXPERTS, logits, -jnp.inf)
    m1 = jnp.max(lg, axis=0, keepdims=True)
    i1 = jnp.min(jnp.where(lg == m1, row, n_rows), axis=0, keepdims=True)
    lg2 = jnp.where(row == i1, -jnp.inf, lg)
    m2 = jnp.max(lg2, axis=0, keepdims=True)
    i2 = jnp.min(jnp.where(lg2 == m2, row, n_rows), axis=0, keepdims=True)
    e2 = jnp.exp(m2 - m1)
    w1 = 1.0 / (1.0 + e2)
    w2 = e2 / (1.0 + e2)

    onehot = jnp.where(row == i1, 1.0, jnp.where(row == i2, 1.0, 0.0))
    tri = (lax.broadcasted_iota(jnp.int32, (tm, tm), 0)
           < lax.broadcasted_iota(jnp.int32, (tm, tm), 1))
    prefix = jnp.dot(onehot.astype(BF16), jnp.where(tri, 1.0, 0.0).astype(BF16),
                     preferred_element_type=F32)
    count = jnp.sum(onehot, axis=1, keepdims=True)
    seg_len = jnp.floor((count + (MOE_SEG - 1)) * (1.0 / MOE_SEG)) * MOE_SEG
    seg_off = jnp.zeros_like(seg_len)
    for e in range(N_EXPERTS - 1):
        seg_off = seg_off + jnp.where(row[:, :1] > e, seg_len[e:e + 1, :], 0.0)
    local = prefix + seg_off
    pos1 = jnp.sum(jnp.where(row == i1, local, 0.0), axis=0, keepdims=True)
    pos2 = jnp.sum(jnp.where(row == i2, local, 0.0), axis=0, keepdims=True)
    cnt_ref[...] = jnp.broadcast_to(count, cnt_ref.shape)
    fields = (i1.astype(F32), i2.astype(F32), pos1, pos2, w1, w2)
    field_row = lax.broadcasted_iota(jnp.int32, route_ref.shape, 0)
    route = jnp.zeros(route_ref.shape, F32)
    for n, val in enumerate(fields):
        route = jnp.where(field_row == n, val, route)
    route_ref[...] = route


def _outproj_router(lru, att, x, mod_l, ln_g, w_out_b, router):
    bsz, seq, d = x.shape
    w = lru.shape[-1]
    tm = MOE_TB
    nt = seq // tm
    e_rows = 2 * SUBLANES
    rt = jnp.zeros((e_rows, d), F32).at[:N_EXPERTS].set(router.T)
    row = lambda k: pl.BlockSpec((None, None, 1, d), lambda b, i: (b, k, 0, 0))
    tok = lambda n: pl.BlockSpec((None, tm, n), lambda b, i: (b, i, 0))
    return pl.pallas_call(
        functools.partial(_outproj_router_kernel, w=w),
        grid=(bsz, nt),
        in_specs=[tok(w), tok(w), tok(d), row(2), row(4), row(3),
                  pl.BlockSpec((1, d), lambda b, i: (0, 0)),
                  pl.BlockSpec((d, d), lambda b, i: (0, 0)),
                  pl.BlockSpec((e_rows, d), lambda b, i: (0, 0))],
        out_specs=[
            tok(d), tok(d),
            pl.BlockSpec((None, SUBLANES, tm), lambda b, i: (b * nt + i, 0, 0)),
            pl.BlockSpec((e_rows, LANES), lambda b, i: (b * nt + i, 0))],
        out_shape=[jax.ShapeDtypeStruct((bsz, seq, d), F32),
                   jax.ShapeDtypeStruct((bsz, seq, d), BF16),
                   jax.ShapeDtypeStruct((bsz * nt, SUBLANES, tm), F32),
                   jax.ShapeDtypeStruct((bsz * nt * e_rows, LANES), F32)],
        compiler_params=_cparams(("arbitrary", "arbitrary")),
        name="outproj_router",
    )(lru, att, x, mod_l, mod_l, mod_l, ln_g.reshape(1, d), w_out_b, rt)


def _ffn_kernel(*refs, w, final_norm):
    (lru_ref, att_ref, x_ref, g1_ref, sc_ref, sh_ref, ln_ref, wo_ref,
     wg_ref, wu_ref, wd_ref, g2_ref) = refs[:12]
    fg_ref = refs[12] if final_norm else None
    o_ref, x1_ref, h2_ref, acc_ref = refs[-4:]
    j = pl.program_id(2)

    @pl.when(j == 0)
    def _():
        y = (jnp.dot(lru_ref[...], wo_ref[:w, :], preferred_element_type=F32)
             + jnp.dot(att_ref[...], wo_ref[w:, :], preferred_element_type=F32))
        x1 = x_ref[...] + g1_ref[...] * y
        x1_ref[...] = x1
        h2 = _rmsnorm(x1, ln_ref[...]) * (1.0 + sc_ref[...]) + sh_ref[...]
        h2_ref[...] = h2.astype(BF16)
        acc_ref[...] = jnp.zeros_like(acc_ref)

    h = h2_ref[...]
    act = (jax.nn.silu(jnp.dot(h, wg_ref[...], preferred_element_type=F32))
           * jnp.dot(h, wu_ref[...], preferred_element_type=F32))
    acc_ref[...] += jnp.dot(act.astype(BF16), wd_ref[...], preferred_element_type=F32)

    @pl.when(j == pl.num_programs(2) - 1)
    def _():
        out = x1_ref[...] + g2_ref[...] * acc_ref[...]
        if final_norm:
            out = _rmsnorm(out, fg_ref[...])
        o_ref[...] = out


def _ffn(lru, att, x, mod_l, ln_g, w_out_b, wg, wu, wd, final_g=None, *, tm=512, tf=1536):
    bsz, seq, d = x.shape
    w = lru.shape[-1]
    ff = wg.shape[1]
    tf = min(tf, ff)
    final_norm = final_g is not None
    row = lambda k: pl.BlockSpec((None, None, 1, d), lambda b, i, j: (b, k, 0, 0))
    tok = lambda n: pl.BlockSpec((None, tm, n), lambda b, i, j: (b, i, 0))
    in_specs = [tok(w), tok(w), tok(d), row(2), row(4), row(3),
                pl.BlockSpec((1, d), lambda b, i, j: (0, 0)),
                pl.BlockSpec((d, d), lambda b, i, j: (0, 0)),
                pl.BlockSpec((d, tf), lambda b, i, j: (0, j)),
                pl.BlockSpec((d, tf), lambda b, i, j: (0, j)),
                pl.BlockSpec((tf, d), lambda b, i, j: (j, 0)),
                row(5)]
    args = [lru, att, x, mod_l, mod_l, mod_l, ln_g.reshape(1, d), w_out_b, wg, wu, wd, mod_l]
    if final_norm:
        in_specs.append(pl.BlockSpec((1, d), lambda b, i, j: (0, 0)))
        args.append(final_g.reshape(1, d))
    return pl.pallas_call(
        functools.partial(_ffn_kernel, w=w, final_norm=final_norm),
        grid=(bsz, seq // tm, ff // tf),
        in_specs=in_specs,
        out_specs=tok(d),
        out_shape=jax.ShapeDtypeStruct((bsz, seq, d), F32),
        scratch_shapes=[pltpu.VMEM((tm, d), F32), pltpu.VMEM((tm, d), BF16),
                        pltpu.VMEM((tm, d), F32)],
        compiler_params=_cparams(("arbitrary",) * 3),
        name="outproj_ffn",
    )(*args)


MOE_TB = 512
MOE_SEG = 16
MOE_TF = 512
MOE_LR = 2 * MOE_TB + N_EXPERTS * MOE_SEG


def _moe_plan(cnt, *, m_tok):
    i32 = jnp.int32
    nb = m_tok // MOE_TB
    n_e = N_EXPERTS
    rows = -(-(2 * m_tok + nb * n_e * MOE_SEG + n_e * MOE_TF) // MOE_TF) * MOE_TF
    ntf = rows // MOE_TF
    n = cnt.reshape(nb, -1, LANES)[:, :n_e, 0].astype(i32)
    seg_n = (n + MOE_SEG - 1) // MOE_SEG
    seg_src = jnp.cumsum(seg_n, axis=1) - seg_n
    used = jnp.sum(seg_n, axis=0)
    per_tile = MOE_TF // MOE_SEG
    gsz = (used + per_tile - 1) // per_tile * per_tile
    gend = jnp.cumsum(gsz)
    goff = gend - gsz
    seg_dst = goff[None, :] + jnp.cumsum(seg_n, axis=0) - seg_n
    total_tiles = gend[-1] // per_tile
    tile = jnp.arange(ntf, dtype=i32)
    f_valid = (tile < total_tiles).astype(i32)
    f_exp = jnp.minimum(
        jnp.sum(gend[None, :] <= (jnp.minimum(tile, total_tiles - 1) * per_tile)[:, None],
                axis=1).astype(i32), n_e - 1)
    tail_dst = jnp.concatenate([goff + used, gend[-1:]])
    tail_n = jnp.concatenate([gsz - used, rows // MOE_SEG - gend[-1:]])
    return dict(seg_n=seg_n.reshape(-1), seg_src=seg_src.reshape(-1),
                seg_dst=seg_dst.reshape(-1), blk_n=jnp.sum(seg_n, axis=1),
                tail_dst=tail_dst, tail_n=tail_n,
                f_exp=f_exp, f_valid=f_valid, rows=rows, ntf=ntf, nb=nb)


def _seg_rows(unit):
    return pl.ds(pl.multiple_of(unit * MOE_SEG, MOE_SEG), MOE_SEG)


def _wait_segments(sem, buf_ref, n):
    def body(_, carry):
        pltpu.make_async_copy(buf_ref.at[pl.ds(0, MOE_SEG)], buf_ref.at[pl.ds(0, MOE_SEG)],
                              sem).wait()
        return carry
    lax.fori_loop(0, n, body, 0)


def _moe_scatter_kernel(seg_n_ref, seg_src_ref, seg_dst_ref, tail_dst_ref, tail_n_ref,
                        h_ref, p1_ref, p2_ref, xs_hbm, buf_ref, zero_ref, sem, tail_sem,
                        issued_ref):
    tb = pl.program_id(0)
    nb = pl.num_programs(0)
    slot = tb % 2
    buf = buf_ref.at[slot]

    @pl.when(tb >= 2)
    def _():
        _wait_segments(sem.at[slot], buf, issued_ref[slot])

    rows = lax.broadcasted_iota(jnp.int32, (MOE_LR, MOE_TB), 0)
    sel = jnp.where(p1_ref[...] == rows, 1.0, jnp.where(p2_ref[...] == rows, 1.0, 0.0))
    buf[...] = jnp.dot(sel.astype(BF16), h_ref[...],
                       preferred_element_type=F32).astype(buf_ref.dtype)

    issued = 0
    for e in range(N_EXPERTS):
        k = tb * N_EXPERTS + e
        n, src, dst = seg_n_ref[k], seg_src_ref[k], seg_dst_ref[k]

        def copy_seg(g, carry):
            pltpu.make_async_copy(buf.at[_seg_rows(src + g)], xs_hbm.at[_seg_rows(dst + g)],
                                  sem.at[slot]).start()
            return carry
        lax.fori_loop(0, n, copy_seg, 0)
        issued = issued + n
    issued_ref[slot] = issued

    @pl.when(tb == nb - 1)
    def _():
        zero_ref[...] = jnp.zeros_like(zero_ref)
        n_tail = 0
        for e in range(N_EXPERTS + 1):
            n, dst = tail_n_ref[e], tail_dst_ref[e]

            def zero_seg(g, carry):
                pltpu.make_async_copy(zero_ref, xs_hbm.at[_seg_rows(dst + g)], tail_sem).start()
                return carry
            lax.fori_loop(0, n, zero_seg, 0)
            n_tail = n_tail + n
        _wait_segments(tail_sem, zero_ref, n_tail)
        _wait_segments(sem.at[slot], buf, issued_ref[slot])

        @pl.when(nb >= 2)
        def _():
            _wait_segments(sem.at[1 - slot], buf, issued_ref[1 - slot])


def _moe_scatter(h2, plan, p1, p2):
    m_tok, d = h2.shape
    nb = plan['nb']
    tokrow = lambda: pl.BlockSpec((None, 1, MOE_TB), lambda t, *_: (t, 0, 0))
    grid_spec = pltpu.PrefetchScalarGridSpec(
        num_scalar_prefetch=5,
        grid=(nb,),
        in_specs=[pl.BlockSpec((MOE_TB, d), lambda t, *_: (t, 0)), tokrow(), tokrow()],
        out_specs=pl.BlockSpec(memory_space=pl.ANY),
        scratch_shapes=[pltpu.VMEM((2, MOE_LR, d), BF16), pltpu.VMEM((MOE_SEG, d), BF16),
                        pltpu.SemaphoreType.DMA((2,)), pltpu.SemaphoreType.DMA(()),
                        pltpu.SMEM((2,), jnp.int32)],
    )
    return pl.pallas_call(
        _moe_scatter_kernel,
        grid_spec=grid_spec,
        out_shape=jax.ShapeDtypeStruct((plan['rows'], d), BF16),
        compiler_params=_cparams(("arbitrary",)),
        name="moe_scatter",
    )(plan['seg_n'], plan['seg_src'], plan['seg_dst'], plan['tail_dst'], plan['tail_n'],
      h2, p1.reshape(nb, 1, MOE_TB), p2.reshape(nb, 1, MOE_TB))


def _moe_ffn_kernel(exp_ref, valid_ref, xs_ref, wg_ref, wu_ref, wd_ref, o_ref, acc_ref):
    n = pl.program_id(0)
    j = pl.program_id(1)

    @pl.when(valid_ref[n] == 0)
    def _():
        o_ref[...] = jnp.zeros_like(o_ref)

    @pl.when(valid_ref[n] != 0)
    def _():
        @pl.when(j == 0)
        def _():
            acc_ref[...] = jnp.zeros_like(acc_ref)

        h = xs_ref[...]
        act = (jax.nn.silu(jnp.dot(h, wg_ref[...], preferred_element_type=F32))
               * jnp.dot(h, wu_ref[...], preferred_element_type=F32))
        acc_ref[...] += jnp.dot(act.astype(BF16), wd_ref[...], preferred_element_type=F32)

        @pl.when(j == pl.num_programs(1) - 1)
        def _():
            o_ref[...] = acc_ref[...].astype(o_ref.dtype)


def _moe_ffn(xs, wg, wu, wd, plan, *, tf=1536):
    rows, d = xs.shape
    ff = wg.shape[2]
    tf = min(tf, ff)
    nj = ff // tf

    def ff_tile(n, j, v):
        return j * v[n] + (nj - 1) * (1 - v[n])

    grid_spec = pltpu.PrefetchScalarGridSpec(
        num_scalar_prefetch=2,
        grid=(plan['ntf'], nj),
        in_specs=[pl.BlockSpec((MOE_TF, d), lambda n, j, e, v: (n, 0)),
                  pl.BlockSpec((None, d, tf), lambda n, j, e, v: (e[n], 0, ff_tile(n, j, v))),
                  pl.BlockSpec((None, d, tf), lambda n, j, e, v: (e[n], 0, ff_tile(n, j, v))),
                  pl.BlockSpec((None, tf, d), lambda n, j, e, v: (e[n], ff_tile(n, j, v), 0))],
        out_specs=pl.BlockSpec((MOE_TF, d), lambda n, j, e, v: (n, 0)),
        scratch_shapes=[pltpu.VMEM((MOE_TF, d), F32)],
    )
    return pl.pallas_call(
        _moe_ffn_kernel,
        grid_spec=grid_spec,
        out_shape=jax.ShapeDtypeStruct((rows, d), BF16),
        compiler_params=_cparams(("arbitrary", "arbitrary")),
        name="moe_ffn",
    )(plan['f_exp'], plan['f_valid'], xs, wg, wu, wd)


def _moe_combine_kernel(*refs, final_norm):
    if final_norm:
        (seg_n_ref, seg_src_ref, seg_dst_ref, blk_n_ref, ye_hbm, p1_ref, p2_ref, w1_ref, w2_ref,
         x1_ref, g2_ref, fg_ref, o_ref, buf_ref, sem) = refs
    else:
        (seg_n_ref, seg_src_ref, seg_dst_ref, blk_n_ref, ye_hbm, p1_ref, p2_ref, w1_ref, w2_ref,
         x1_ref, g2_ref, o_ref, buf_ref, sem) = refs
    tb = pl.program_id(0)
    nb = pl.num_programs(0)
    slot = tb % 2

    def fetch(block, into):
        for e in range(N_EXPERTS):
            k = block * N_EXPERTS + e
            n, src, dst = seg_n_ref[k], seg_src_ref[k], seg_dst_ref[k]

            def copy_seg(g, carry):
                pltpu.make_async_copy(ye_hbm.at[_seg_rows(dst + g)],
                                      buf_ref.at[into, _seg_rows(src + g)], sem.at[into]).start()
                return carry
            lax.fori_loop(0, n, copy_seg, 0)

    @pl.when(tb == 0)
    def _():
        fetch(tb, slot)

    @pl.when(tb + 1 < nb)
    def _():
        fetch(tb + 1, 1 - slot)

    buf = buf_ref.at[slot]
    _wait_segments(sem.at[slot], buf, blk_n_ref[tb])

    def clear(g, carry):
        buf[_seg_rows(g), :] = jnp.zeros((MOE_SEG, buf.shape[1]), buf.dtype)
        return carry
    lax.fori_loop(blk_n_ref[tb], MOE_LR // MOE_SEG, clear, 0)

    ye = buf[...]
    reps = MOE_LR // LANES
    cols = lax.broadcasted_iota(jnp.int32, (MOE_TB, MOE_LR), 1)

    def unsort(p_ref):
        hit = jnp.tile(p_ref[...], (1, reps)) == cols
        return jnp.dot(jnp.where(hit, 1.0, 0.0).astype(BF16), ye, preferred_element_type=F32)

    lanes = x1_ref.shape[1] // LANES
    y = (jnp.tile(w1_ref[...], (1, lanes)) * unsort(p1_ref)
         + jnp.tile(w2_ref[...], (1, lanes)) * unsort(p2_ref))
    out = x1_ref[...] + g2_ref[...] * y
    if final_norm:
        out = _rmsnorm(out, fg_ref[...])
    o_ref[...] = out


def _moe_combine(ye, plan, route_t, x1, mod_l, final_g=None):
    bsz, seq, d = x1.shape
    m_tok = bsz * seq
    nb = plan['nb']
    blocks_per_seq = seq // MOE_TB
    final_norm = final_g is not None
    rep = lambda a, dt: jnp.broadcast_to(a.reshape(m_tok, 1).astype(dt), (m_tok, LANES))
    tokrep = lambda: pl.BlockSpec((MOE_TB, LANES), lambda t, *_: (t, 0))
    in_specs = [pl.BlockSpec(memory_space=pl.ANY), tokrep(), tokrep(), tokrep(), tokrep(),
                pl.BlockSpec((MOE_TB, d), lambda t, *_: (t, 0)),
                pl.BlockSpec((None, None, 1, d), lambda t, *_: (t // blocks_per_seq, 5, 0, 0))]
    args = [ye, rep(route_t[:, 2, :], jnp.int32), rep(route_t[:, 3, :], jnp.int32),
            rep(route_t[:, 4, :], F32), rep(route_t[:, 5, :], F32),
            x1.reshape(m_tok, d), mod_l]
    if final_norm:
        in_specs.append(pl.BlockSpec((1, d), lambda t, *_: (0, 0)))
        args.append(final_g.reshape(1, d))
    grid_spec = pltpu.PrefetchScalarGridSpec(
        num_scalar_prefetch=4,
        grid=(nb,),
        in_specs=in_specs,
        out_specs=pl.BlockSpec((MOE_TB, d), lambda t, *_: (t, 0)),
        scratch_shapes=[pltpu.VMEM((2, MOE_LR, d), BF16), pltpu.SemaphoreType.DMA((2,))],
    )
    out = pl.pallas_call(
        functools.partial(_moe_combine_kernel, final_norm=final_norm),
        grid_spec=grid_spec,
        out_shape=jax.ShapeDtypeStruct((m_tok, d), F32),
        compiler_params=_cparams(("arbitrary",)),
        name="moe_combine",
    )(plan['seg_n'], plan['seg_src'], plan['seg_dst'], plan['blk_n'], *args)
    return out.reshape(bsz, seq, d)


def _moe(h2, route_t, cnt, x1, mod_l, wg, wu, wd, final_g=None):
    bsz, seq, d = x1.shape
    m_tok = bsz * seq
    plan = _moe_plan(cnt, m_tok=m_tok)
    p1 = route_t[:, 2, :].astype(jnp.int32)
    p2 = route_t[:, 3, :].astype(jnp.int32)
    xs = _moe_scatter(h2.reshape(m_tok, d), plan, p1, p2)
    ye = _moe_ffn(xs, wg, wu, wd, plan)
    return _moe_combine(ye, plan, route_t, x1, mod_l, final_g)


def kernel(x, c, positions, ada_w, ada_b, ln1_g, ln2_g, w_in, conv_w, conv_b, gate_a_w, gate_a_b, gate_x_w, gate_x_b, lru_lambda, lam_q1, lam_k1, lam_q2, lam_k2, subln_g, w_out, ffn_w_gate, ffn_w_up, ffn_w_down, moe_router, moe_w_gate, moe_w_up, moe_w_down, final_g):
    depth = ada_w.shape[0]
    vd = subln_g.shape[-1]
    dh = vd // 2
    attn_w = DIFF_HEADS * vd

    mod = _modulation(c, ada_w, ada_b)
    cos_t, sin_t = _rope_tables(positions, dh)
    for l in range(depth):
        lambda_init = 0.8 - 0.6 * math.exp(-0.3 * l)
        mod_l = mod[l]
        lru, q, k, v = _inproj(x, mod_l, ln1_g[l], w_in[l].astype(BF16), cos_t, sin_t,
                               conv_w[l], conv_b[l], gate_a_w[l], gate_a_b[l], gate_x_w[l],
                               gate_x_b[l], lru_lambda[l], attn_w=attn_w, dh=dh)
        att = _attention(q, k, v, positions, lam_q1[l], lam_k1[l], lam_q2[l], lam_k2[l],
                         subln_g[l], lambda_init, dh=dh)
        fg = final_g if l == depth - 1 else None
        j = l // 2
        if l % 2 == 0:
            x = _ffn(lru, att, x, mod_l, ln2_g[l], w_out[l].astype(BF16),
                     ffn_w_gate[j].astype(BF16), ffn_w_up[j].astype(BF16),
                     ffn_w_down[j].astype(BF16), final_g=fg)
        else:
            x1, h2, route, cnt = _outproj_router(lru, att, x, mod_l, ln2_g[l],
                                                 w_out[l].astype(BF16), moe_router[j])
            x = _moe(h2, route, cnt, x1, mod_l, moe_w_gate[j].astype(BF16),
                     moe_w_up[j].astype(BF16), moe_w_down[j].astype(BF16), final_g=fg)
    return x
```

```python
import functools
import math

import jax
import jax.numpy as jnp
from jax import lax
from jax.experimental import pallas as pl
from jax.experimental.pallas import tpu as pltpu

F32 = jnp.float32
BF16 = jnp.bfloat16
HIGHEST = lax.Precision.HIGHEST

CHUNK = 64
LRU_BLOCKS = 8
CONV_W = 4
RG_C = 8.0
DIFF_HEADS = 4
ROPE_THETA = 10000.0
N_EXPERTS = 8
EPS = 1e-6
LANES = 128
SUBLANES = 8
VMEM_LIMIT = 56 * 1024 * 1024
MASK_VALUE = -0.5 * float(jnp.finfo(jnp.float32).max)


def _cparams(sem):
    return pltpu.CompilerParams(dimension_semantics=sem, vmem_limit_bytes=VMEM_LIMIT)


def _rmsnorm(x, g):
    return x * lax.rsqrt(jnp.mean(x * x, axis=-1, keepdims=True) + EPS) * g


def _mod_kernel(c_ref, w_ref, b_ref, o_ref):
    c = c_ref[...]
    s = c * jax.nn.sigmoid(c)
    o_ref[...] = jnp.dot(s, w_ref[...], precision=HIGHEST,
                         preferred_element_type=F32) + b_ref[...]


def _modulation(c, ada_w, ada_b, tn=1024):
    depth, d, n = ada_w.shape
    bsz = c.shape[0]
    rows = -(-bsz // SUBLANES) * SUBLANES
    c_pad = jnp.zeros((rows, d), F32).at[:bsz].set(c)
    out = pl.pallas_call(
        _mod_kernel,
        grid=(depth, n // tn),
        in_specs=[
            pl.BlockSpec((rows, d), lambda l, j: (0, 0)),
            pl.BlockSpec((None, d, tn), lambda l, j: (l, 0, j)),
            pl.BlockSpec((None, 1, tn), lambda l, j: (l, 0, j)),
        ],
        out_specs=pl.BlockSpec((None, rows, tn), lambda l, j: (l, 0, j)),
        out_shape=jax.ShapeDtypeStruct((depth, rows, n), F32),
        compiler_params=_cparams(("arbitrary", "arbitrary")),
        name="adaln_mod",
    )(c_pad, ada_w, ada_b.reshape(depth, 1, n))
    return out[:, :bsz].reshape(depth, bsz, 6, 1, d)


def _rope_table_kernel(pos_ref, inv_ref, cos_ref, sin_ref, *, n_freq):
    ang = pos_ref[...].astype(F32) * inv_ref[...]
    groups = LANES // n_freq
    row = lax.broadcasted_iota(jnp.int32, (LANES, groups * LANES), 0)
    col = lax.broadcasted_iota(jnp.int32, (LANES, groups * LANES), 1)
    hit = row == (col // LANES) * n_freq + col % n_freq
    spread_cos = jnp.where(hit, 1.0, 0.0).astype(BF16)
    first_half = col % (2 * n_freq) < n_freq
    spread_sin = jnp.where(hit, jnp.where(first_half, -1.0, 1.0), 0.0).astype(BF16)

    def spread(t, e):
        out = None
        rest = t
        for _ in range(3):
            piece = rest.astype(BF16)
            rest = rest - piece.astype(F32)
            term = jnp.dot(piece, e, preferred_element_type=F32)
            out = term if out is None else out + term
        return out

    c = spread(jnp.cos(ang), spread_cos)
    s = spread(jnp.sin(ang), spread_sin)
    for g in range(groups):
        cos_ref[g] = c[:, g * LANES:(g + 1) * LANES]
        sin_ref[g] = s[:, g * LANES:(g + 1) * LANES]


def _rope_tables(positions, dh):
    n_freq = dh // 2
    groups = LANES // n_freq
    tok = positions.size
    rows = tok // groups
    inv = ROPE_THETA ** (-jnp.arange(0, dh, 2, dtype=F32) / dh)
    pos_x = jnp.repeat(positions.reshape(groups, rows).T, n_freq, axis=1)
    inv_x = jnp.tile(inv, groups).reshape(1, LANES)
    tr = min(rows, 1024)
    cos, sin = pl.pallas_call(
        functools.partial(_rope_table_kernel, n_freq=n_freq),
        grid=(rows // tr,),
        in_specs=[pl.BlockSpec((tr, LANES), lambda i: (i, 0)),
                  pl.BlockSpec((1, LANES), lambda i: (0, 0))],
        out_specs=[pl.BlockSpec((groups, tr, LANES), lambda i: (0, i, 0))] * 2,
        out_shape=[jax.ShapeDtypeStruct((groups, rows, LANES), F32)] * 2,
        compiler_params=_cparams(("arbitrary",)),
        name="rope_tables",
    )(pos_x, inv_x)
    return cos.reshape(tok, LANES), sin.reshape(tok, LANES)


def _inproj_kernel(x_ref, sc_ref, sh_ref, g_ref, w_ref, cos_ref, sin_ref,
                   cw_ref, cb_ref, wg_ref, bg_ref, lam_ref,
                   lru_ref, q_ref, k_ref, v_ref, xpad_ref, h_ref, *, lru_w, attn_w, dh):
    @pl.when(pl.program_id(1) == 0)
    def _():
        xpad_ref[0:SUBLANES, :] = jnp.zeros((SUBLANES, lru_w), F32)
        h_ref[...] = jnp.zeros_like(h_ref)

    h = _rmsnorm(x_ref[...], g_ref[...]) * (1.0 + sc_ref[...]) + sh_ref[...]
    hb = h.astype(BF16)
    lru2 = 2 * lru_w
    xy = jnp.dot(hb, w_ref[:, :lru2], preferred_element_type=F32)

    reps = attn_w // LANES
    cos = jnp.tile(cos_ref[...], (1, reps))
    sin = jnp.tile(sin_ref[...], (1, reps))
    lane = lax.broadcasted_iota(jnp.int32, cos.shape, 1)
    first_half = (lane % dh) < (dh // 2)

    def rope(t):
        fwd = pltpu.roll(t, attn_w - dh // 2, axis=1)
        bwd = pltpu.roll(t, dh // 2, axis=1)
        return t * cos + jnp.where(first_half, fwd, bwd) * sin

    q = jnp.dot(hb, w_ref[:, lru2:lru2 + attn_w], preferred_element_type=F32)
    q_ref[...] = (rope(q) * (dh ** -0.5 * math.log2(math.e))).astype(BF16)
    k = jnp.dot(hb, w_ref[:, lru2 + attn_w:lru2 + 2 * attn_w], preferred_element_type=F32)
    k_ref[...] = rope(k).astype(BF16)
    v = jnp.dot(hb, w_ref[:, lru2 + 2 * attn_w:], preferred_element_type=F32)
    v_ref[...] = v.astype(BF16)

    u, gates = _lru_conv_gates(xy[:, :lru_w], cw_ref, cb_ref, wg_ref, bg_ref, xpad_ref)
    a, bt = _lru_coeffs(u, gates, lam_ref)
    lru_ref[...] = _lru_scan(a, bt, xy[:, lru_w:], h_ref).astype(lru_ref.dtype)


def _inproj(x, mod_l, ln_g, w_in_b, cos_t, sin_t, conv_w, conv_b, wa, ba, wx, bx, lam, *,
            attn_w, dh, tm=512):
    bsz, seq, d = x.shape
    nt = seq // tm
    d_in = w_in_b.shape[1]
    lru_w = conv_w.shape[-1]
    wg = jnp.concatenate([_block_diag(wa), _block_diag(wx)], axis=1).astype(BF16)
    bg = jnp.concatenate([ba, bx]).reshape(1, 2 * lru_w)
    row = lambda k: pl.BlockSpec((None, None, 1, d), lambda b, i: (b, k, 0, 0))
    tok = lambda w: pl.BlockSpec((None, tm, w), lambda b, i: (b, i, 0))
    const = lambda shape: pl.BlockSpec(shape, lambda b, i: (0,) * len(shape))
    return pl.pallas_call(
        functools.partial(_inproj_kernel, lru_w=lru_w, attn_w=attn_w, dh=dh),
        grid=(bsz, nt),
        in_specs=[
            tok(d), row(1), row(0), const((1, d)), const((d, d_in)),
            pl.BlockSpec((tm, LANES), lambda b, i: (b * nt + i, 0)),
            pl.BlockSpec((tm, LANES), lambda b, i: (b * nt + i, 0)),
            const((CONV_W, lru_w)), const((1, lru_w)), const((lru_w, 2 * lru_w)),
            const((1, 2 * lru_w)), const((1, lru_w)),
        ],
        out_specs=[tok(lru_w), tok(attn_w), tok(attn_w), tok(attn_w)],
        out_shape=[jax.ShapeDtypeStruct((bsz, seq, lru_w), BF16),
                   jax.ShapeDtypeStruct((bsz, seq, attn_w), BF16),
                   jax.ShapeDtypeStruct((bsz, seq, attn_w), BF16),
                   jax.ShapeDtypeStruct((bsz, seq, attn_w), BF16)],
        scratch_shapes=[pltpu.VMEM((tm + SUBLANES, lru_w), F32), pltpu.VMEM((1, lru_w), F32)],
        compiler_params=_cparams(("arbitrary", "arbitrary")),
        name="inproj_lru",
    )(x, mod_l, mod_l, ln_g.reshape(1, d), w_in_b, cos_t, sin_t,
      conv_w, conv_b.reshape(1, lru_w), wg, bg, lam.reshape(1, lru_w))


def _gelu_tanh(x):
    return 0.5 * x * (1.0 + jnp.tanh(math.sqrt(2.0 / math.pi) * (x + 0.044715 * (x * x * x))))


def _lru_conv_gates(xr, cw_ref, cb_ref, wg_ref, bg_ref, xpad_ref):
    t, w = xr.shape
    xpad_ref[SUBLANES:SUBLANES + t, :] = xr
    u = cb_ref[...]
    for j in range(CONV_W):
        off = SUBLANES - (CONV_W - 1) + j
        u = u + cw_ref[j:j + 1, :] * xpad_ref[off:off + t, :]
    xpad_ref[0:SUBLANES, :] = xpad_ref[t:t + SUBLANES, :]
    gates = jnp.dot(u.astype(BF16), wg_ref[...], preferred_element_type=F32) + bg_ref[...]
    return u, gates


def _lru_coeffs(u, gates, lam_ref):
    w = u.shape[1]
    r = jax.nn.sigmoid(gates[:, :w])
    ig = jax.nn.sigmoid(gates[:, w:])
    neg_lam = -lam_ref[...]
    softplus = jnp.maximum(neg_lam, 0.0) + jnp.log1p(jnp.exp(-jnp.abs(neg_lam)))
    log_a = (-RG_C) * r * softplus
    a = jnp.exp(log_a)
    return a, jnp.sqrt(1.0 - a * a) * (ig * u)


def _lru_scan(a, bt, yr, h_ref):
    t, w = a.shape
    groups = t // SUBLANES
    a = a.reshape(groups, SUBLANES, w)
    bt = bt.reshape(groups, SUBLANES, w)
    sub = lax.broadcasted_iota(jnp.int32, a.shape, 1)
    shift = 1
    while shift < SUBLANES:
        keep = sub >= shift
        a_prev = jnp.where(keep, pltpu.roll(a, shift, axis=1), 1.0)
        b_prev = jnp.where(keep, pltpu.roll(bt, shift, axis=1), 0.0)
        bt = a * b_prev + bt
        a = a * a_prev
        shift *= 2
    carry = h_ref[...]
    rows = []
    for g in range(groups):
        hg = a[g] * carry + bt[g]
        rows.append(hg)
        carry = hg[SUBLANES - 1:SUBLANES, :]
    h_ref[...] = carry
    return jnp.concatenate(rows, axis=0) * _gelu_tanh(yr)


def _block_diag(wb):
    n, bw, _ = wb.shape
    eye = jnp.eye(n, dtype=wb.dtype)
    return jnp.einsum('nhk,nm->nhmk', wb, eye).reshape(n * bw, n * bw)


def _attn_kernel(qmin_ref, qmax_ref, kmin_ref, kmax_ref,
                 q_ref, k_ref, v_ref, cq_ref, ck_ref, lq1_ref, lk1_ref, lq2_ref, lk2_ref,
                 g_ref, o_ref, m_ref, l_ref, acc_ref, *, tq, tk, nk, dh, lambda_init):
    b = pl.program_id(0)
    i = pl.program_id(2)
    m_ref[...] = jnp.full(m_ref.shape, -jnp.inf, F32)
    l_ref[...] = jnp.zeros(l_ref.shape, F32)
    acc_ref[...] = jnp.zeros(acc_ref.shape, F32)

    q = q_ref[...]
    lane = lax.broadcasted_iota(jnp.int32, q.shape, 1)
    qc = (jnp.where(lane < dh, q, jnp.zeros_like(q)), jnp.where(lane >= dh, q, jnp.zeros_like(q)))
    q_lo = qmin_ref[b, i]
    q_hi = qmax_ref[b, i]

    def process(j, masked):
        start = pl.multiple_of(j * tk, tk)
        kb = k_ref[pl.ds(start, tk), :]
        vb = v_ref[pl.ds(start, tk), :]
        if masked:
            ck = ck_ref[:, pl.ds(start, tk)]
            visible = ck <= jnp.tile(cq_ref[...], (1, tk // LANES))
        for c in range(2):
            s = lax.dot_general(qc[c], kb, (((1,), (1,)), ((), ())),
                                preferred_element_type=F32)
            if masked:
                s = jnp.where(visible, s, MASK_VALUE)
            m_prev = m_ref[c]
            m_new = jnp.maximum(m_prev, jnp.max(s, axis=-1, keepdims=True))
            alpha = jnp.exp2(m_prev - m_new)
            p = jnp.exp2(s - jnp.tile(m_new, (1, tk // LANES)))
            p_lanes = p[:, :LANES]
            for t in range(1, tk // LANES):
                p_lanes = p_lanes + p[:, t * LANES:(t + 1) * LANES]
            l_ref[c] = alpha * l_ref[c] + p_lanes
            acc_ref[c] = alpha * acc_ref[c] + jnp.dot(p.astype(BF16), vb,
                                                      preferred_element_type=F32)
            m_ref[c] = m_new

    def body(j, carry):
        k_lo = kmin_ref[b, j]
        k_hi = kmax_ref[b, j]
        needed = k_lo <= q_hi
        needs_mask = k_hi > q_lo

        @pl.when(jnp.logical_and(needed, needs_mask))
        def _():
            process(j, True)

        @pl.when(jnp.logical_and(needed, jnp.logical_not(needs_mask)))
        def _():
            process(j, False)

        return carry

    lax.fori_loop(0, nk, body, 0)

    lam = (jnp.exp(jnp.sum(lq1_ref[...] * lk1_ref[...], keepdims=True))
           - jnp.exp(jnp.sum(lq2_ref[...] * lk2_ref[...], keepdims=True)) + lambda_init)
    l0 = jnp.sum(l_ref[0], axis=-1, keepdims=True)
    l1 = jnp.sum(l_ref[1], axis=-1, keepdims=True)
    o = acc_ref[0] / l0 - lam * (acc_ref[1] / l1)
    o_ref[...] = (_rmsnorm(o, g_ref[...]) * (1.0 - lambda_init)).astype(o_ref.dtype)


def _attention(q, k, v, positions, lq1, lk1, lq2, lk2, subln_g, lambda_init, *,
               dh, tq=1024, tk=1024):
    bsz, seq, aw = q.shape
    vd = 2 * dh
    heads = aw // vd
    nq, nk = seq // tq, seq // tk
    chunk = positions // CHUNK
    qmin = chunk.reshape(bsz, nq, tq).min(-1)
    qmax = chunk.reshape(bsz, nq, tq).max(-1)
    kmin = chunk.reshape(bsz, nk, tk).min(-1)
    kmax = chunk.reshape(bsz, nk, tk).max(-1)
    cq = jnp.broadcast_to(chunk[:, :, None], (bsz, seq, LANES))
    ck = chunk.reshape(bsz, 1, seq)
    vec = lambda n: pl.BlockSpec((1, n), lambda b, h, i, *_: (0, 0))
    grid_spec = pltpu.PrefetchScalarGridSpec(
        num_scalar_prefetch=4,
        grid=(bsz, heads, nq),
        in_specs=[
            pl.BlockSpec((None, tq, vd), lambda b, h, i, *_: (b, i, h)),
            pl.BlockSpec((None, seq, vd), lambda b, h, i, *_: (b, 0, h)),
            pl.BlockSpec((None, seq, vd), lambda b, h, i, *_: (b, 0, h)),
            pl.BlockSpec((None, tq, LANES), lambda b, h, i, *_: (b, i, 0)),
            pl.BlockSpec((None, 1, seq), lambda b, h, i, *_: (b, 0, 0)),
            vec(dh), vec(dh), vec(dh), vec(dh), vec(vd),
        ],
        out_specs=pl.BlockSpec((None, tq, vd), lambda b, h, i, *_: (b, i, h)),
        scratch_shapes=[pltpu.VMEM((2, tq, LANES), F32), pltpu.VMEM((2, tq, LANES), F32),
                        pltpu.VMEM((2, tq, vd), F32)],
    )
    return pl.pallas_call(
        functools.partial(_attn_kernel, tq=tq, tk=tk, nk=nk, dh=dh, lambda_init=lambda_init),
        grid_spec=grid_spec,
        out_shape=jax.ShapeDtypeStruct((bsz, seq, aw), BF16),
        compiler_params=_cparams(("arbitrary", "arbitrary", "arbitrary")),
        name="diff_attn",
    )(qmin, qmax, kmin, kmax, q, k, v, cq, ck,
      lq1.reshape(1, dh), lk1.reshape(1, dh), lq2.reshape(1, dh), lk2.reshape(1, dh),
      subln_g.reshape(1, vd))


def _outproj_router_kernel(lru_ref, att_ref, x_ref, g1_ref, sc_ref, sh_ref, ln_ref, wo_ref,
                           rt_ref, x1_ref, h2_ref, route_ref, cnt_ref, *, w):
    y = (jnp.dot(lru_ref[...], wo_ref[:w, :], preferred_element_type=F32)
         + jnp.dot(att_ref[...], wo_ref[w:, :], preferred_element_type=F32))
    x1 = x_ref[...] + g1_ref[...] * y
    x1_ref[...] = x1
    h2 = _rmsnorm(x1, ln_ref[...]) * (1.0 + sc_ref[...]) + sh_ref[...]
    h2_ref[...] = h2.astype(BF16)
    _route(h2, rt_ref, route_ref, cnt_ref)


def _route(h2, rt_ref, route_ref, cnt_ref):
    def split(v):
        hi = v.astype(BF16)
        return hi, (v - hi.astype(F32)).astype(BF16)

    h_hi, h_lo = split(h2)
    r_hi, r_lo = split(rt_ref[...])
    nt_dims = (((1,), (1,)), ((), ()))
    dot_nt = lambda a, b: lax.dot_general(a, b, nt_dims, preferred_element_type=F32)
    logits = dot_nt(r_hi, h_hi) + (dot_nt(r_hi, h_lo) + dot_nt(r_lo, h_hi))
    n_rows, tm = logits.shape
    row = lax.broadcasted_iota(jnp.int32, logits.shape, 0)
    lg = jnp.where(row < N_EXPERTS, logits, -jnp.inf)
    m1 = jnp.max(lg, axis=0, keepdims=True)
    i1 = jnp.min(jnp.where(lg == m1, row, n_rows), axis=0, keepdims=True)
    lg2 = jnp.where(row == i1, -jnp.inf, lg)
    m2 = jnp.max(lg2, axis=0, keepdims=True)
    i2 = jnp.min(jnp.where(lg2 == m2, row, n_rows), axis=0, keepdims=True)
    e2 = jnp.exp(m2 - m1)
    w1 = 1.0 / (1.0 + e2)
    w2 = e2 / (1.0 + e2)

    onehot = jnp.where(row == i1, 1.0, jnp.where(row == i2, 1.0, 0.0))
    tri = (lax.broadcasted_iota(jnp.int32, (tm, tm), 0)
           < lax.broadcasted_iota(jnp.int32, (tm, tm), 1))
    prefix = jnp.dot(onehot.astype(BF16), jnp.where(tri, 1.0, 0.0).astype(BF16),
                     preferred_element_type=F32)
    count = jnp.sum(onehot, axis=1, keepdims=True)
    seg_len = jnp.floor((count + (MOE_SEG - 1)) * (1.0 / MOE_SEG)) * MOE_SEG
    seg_off = jnp.zeros_like(seg_len)
    for e in range(N_EXPERTS - 1):
        seg_off = seg_off + jnp.where(row[:, :1] > e, seg_len[e:e + 1, :], 0.0)
    local = prefix + seg_off
    pos1 = jnp.sum(jnp.where(row == i1, local, 0.0), axis=0, keepdims=True)
    pos2 = jnp.sum(jnp.where(row == i2, local, 0.0), axis=0, keepdims=True)
    cnt_ref[...] = jnp.broadcast_to(count, cnt_ref.shape)
    fields = (i1.astype(F32), i2.astype(F32), pos1, pos2, w1, w2)
    field_row = lax.broadcasted_iota(jnp.int32, route_ref.shape, 0)
    route = jnp.zeros(route_ref.shape, F32)
    for n, val in enumerate(fields):
        route = jnp.where(field_row == n, val, route)
    route_ref[...] = route


def _outproj_router(lru, att, x, mod_l, ln_g, w_out_b, router):
    bsz, seq, d = x.shape
    w = lru.shape[-1]
    tm = MOE_TB
    nt = seq // tm
    e_rows = 2 * SUBLANES
    rt = jnp.zeros((e_rows, d), F32).at[:N_EXPERTS].set(router.T)
    row = lambda k: pl.BlockSpec((None, None, 1, d), lambda b, i: (b, k, 0, 0))
    tok = lambda n: pl.BlockSpec((None, tm, n), lambda b, i: (b, i, 0))
    return pl.pallas_call(
        functools.partial(_outproj_router_kernel, w=w),
        grid=(bsz, nt),
        in_specs=[tok(w), tok(w), tok(d), row(2), row(4), row(3),
                  pl.BlockSpec((1, d), lambda b, i: (0, 0)),
                  pl.BlockSpec((d, d), lambda b, i: (0, 0)),
                  pl.BlockSpec((e_rows, d), lambda b, i: (0, 0))],
        out_specs=[
            tok(d), tok(d),
            pl.BlockSpec((None, SUBLANES, tm), lambda b, i: (b * nt + i, 0, 0)),
            pl.BlockSpec((e_rows, LANES), lambda b, i: (b * nt + i, 0))],
        out_shape=[jax.ShapeDtypeStruct((bsz, seq, d), F32),
                   jax.ShapeDtypeStruct((bsz, seq, d), BF16),
                   jax.ShapeDtypeStruct((bsz * nt, SUBLANES, tm), F32),
                   jax.ShapeDtypeStruct((bsz * nt * e_rows, LANES), F32)],
        compiler_params=_cparams(("arbitrary", "arbitrary")),
        name="outproj_router",
    )(lru, att, x, mod_l, mod_l, mod_l, ln_g.reshape(1, d), w_out_b, rt)


def _ffn_kernel(*refs, w, final_norm):
    (lru_ref, att_ref, x_ref, g1_ref, sc_ref, sh_ref, ln_ref, wo_ref,
     wg_ref, wu_ref, wd_ref, g2_ref) = refs[:12]
    fg_ref = refs[12] if final_norm else None
    o_ref, x1_ref, h2_ref, acc_ref = refs[-4:]
    j = pl.program_id(2)

    @pl.when(j == 0)
    def _():
        y = (jnp.dot(lru_ref[...], wo_ref[:w, :], preferred_element_type=F32)
             + jnp.dot(att_ref[...], wo_ref[w:, :], preferred_element_type=F32))
        x1 = x_ref[...] + g1_ref[...] * y
        x1_ref[...] = x1
        h2 = _rmsnorm(x1, ln_ref[...]) * (1.0 + sc_ref[...]) + sh_ref[...]
        h2_ref[...] = h2.astype(BF16)
        acc_ref[...] = jnp.zeros_like(acc_ref)

    h = h2_ref[...]
    act = (jax.nn.silu(jnp.dot(h, wg_ref[...], preferred_element_type=F32))
           * jnp.dot(h, wu_ref[...], preferred_element_type=F32))
    acc_ref[...] += jnp.dot(act.astype(BF16), wd_ref[...], preferred_element_type=F32)

    @pl.when(j == pl.num_programs(2) - 1)
    def _():
        out = x1_ref[...] + g2_ref[...] * acc_ref[...]
        if final_norm:
            out = _rmsnorm(out, fg_ref[...])
        o_ref[...] = out


def _ffn(lru, att, x, mod_l, ln_g, w_out_b, wg, wu, wd, final_g=None, *, tm=512, tf=1536):
    bsz, seq, d = x.shape
    w = lru.shape[-1]
    ff = wg.shape[1]
    tf = min(tf, ff)
    final_norm = final_g is not None
    row = lambda k: pl.BlockSpec((None, None, 1, d), lambda b, i, j: (b, k, 0, 0))
    tok = lambda n: pl.BlockSpec((None, tm, n), lambda b, i, j: (b, i, 0))
    in_specs = [tok(w), tok(w), tok(d), row(2), row(4), row(3),
                pl.BlockSpec((1, d), lambda b, i, j: (0, 0)),
                pl.BlockSpec((d, d), lambda b, i, j: (0, 0)),
                pl.BlockSpec((d, tf), lambda b, i, j: (0, j)),
                pl.BlockSpec((d, tf), lambda b, i, j: (0, j)),
                pl.BlockSpec((tf, d), lambda b, i, j: (j, 0)),
                row(5)]
    args = [lru, att, x, mod_l, mod_l, mod_l, ln_g.reshape(1, d), w_out_b, wg, wu, wd, mod_l]
    if final_norm:
        in_specs.append(pl.BlockSpec((1, d), lambda b, i, j: (0, 0)))
        args.append(final_g.reshape(1, d))
    return pl.pallas_call(
        functools.partial(_ffn_kernel, w=w, final_norm=final_norm),
        grid=(bsz, seq // tm, ff // tf),
        in_specs=in_specs,
        out_specs=tok(d),
        out_shape=jax.ShapeDtypeStruct((bsz, seq, d), F32),
        scratch_shapes=[pltpu.VMEM((tm, d), F32), pltpu.VMEM((tm, d), BF16),
                        pltpu.VMEM((tm, d), F32)],
        compiler_params=_cparams(("arbitrary",) * 3),
        name="outproj_ffn",
    )(*args)


MOE_TB = 512
MOE_SEG = 16
MOE_TF = 512
MOE_LR = 2 * MOE_TB + N_EXPERTS * MOE_SEG


def _moe_plan(cnt, *, m_tok):
    i32 = jnp.int32
    nb = m_tok // MOE_TB
    n_e = N_EXPERTS
    rows = -(-(2 * m_tok + nb * n_e * MOE_SEG + n_e * MOE_TF) // MOE_TF) * MOE_TF
    ntf = rows // MOE_TF
    n = cnt.reshape(nb, -1, LANES)[:, :n_e, 0].astype(i32)
    seg_n = (n + MOE_SEG - 1) // MOE_SEG
    seg_src = jnp.cumsum(seg_n, axis=1) - seg_n
    used = jnp.sum(seg_n, axis=0)
    per_tile = MOE_TF // MOE_SEG
    gsz = (used + per_tile - 1) // per_tile * per_tile
    gend = jnp.cumsum(gsz)
    goff = gend - gsz
    seg_dst = goff[None, :] + jnp.cumsum(seg_n, axis=0) - seg_n
    total_tiles = gend[-1] // per_tile
    tile = jnp.arange(ntf, dtype=i32)
    f_valid = (tile < total_tiles).astype(i32)
    f_exp = jnp.minimum(
        jnp.sum(gend[None, :] <= (jnp.minimum(tile, total_tiles - 1) * per_tile)[:, None],
                axis=1).astype(i32), n_e - 1)
    tail_dst = jnp.concatenate([goff + used, gend[-1:]])
    tail_n = jnp.concatenate([gsz - used, rows // MOE_SEG - gend[-1:]])
    return dict(seg_n=seg_n.reshape(-1), seg_src=seg_src.reshape(-1),
                seg_dst=seg_dst.reshape(-1), blk_n=jnp.sum(seg_n, axis=1),
                tail_dst=tail_dst, tail_n=tail_n,
                f_exp=f_exp, f_valid=f_valid, rows=rows, ntf=ntf, nb=nb)


def _seg_rows(unit):
    return pl.ds(pl.multiple_of(unit * MOE_SEG, MOE_SEG), MOE_SEG)


def _wait_segments(sem, buf_ref, n):
    def body(_, carry):
        pltpu.make_async_copy(buf_ref.at[pl.ds(0, MOE_SEG)], buf_ref.at[pl.ds(0, MOE_SEG)],
                              sem).wait()
        return carry
    lax.fori_loop(0, n, body, 0)


def _moe_scatter_kernel(seg_n_ref, seg_src_ref, seg_dst_ref, tail_dst_ref, tail_n_ref,
                        h_ref, route_ref, xs_hbm, buf_ref, zero_ref, sem, tail_sem,
                        issued_ref):
    tb = pl.program_id(0)
    nb = pl.num_programs(0)
    slot = tb % 2
    buf = buf_ref.at[slot]

    @pl.when(tb >= 2)
    def _():
        _wait_segments(sem.at[slot], buf, issued_ref[slot])

    rows = lax.broadcasted_iota(jnp.int32, (MOE_LR, MOE_TB), 0)
    p1 = route_ref[2:3, :].astype(jnp.int32)
    p2 = route_ref[3:4, :].astype(jnp.int32)
    sel = jnp.where(p1 == rows, 1.0, jnp.where(p2 == rows, 1.0, 0.0))
    buf[...] = jnp.dot(sel.astype(BF16), h_ref[...],
                       preferred_element_type=F32).astype(buf_ref.dtype)

    issued = 0
    for e in range(N_EXPERTS):
        k = tb * N_EXPERTS + e
        n, src, dst = seg_n_ref[k], seg_src_ref[k], seg_dst_ref[k]

        def copy_seg(g, carry):
            pltpu.make_async_copy(buf.at[_seg_rows(src + g)], xs_hbm.at[_seg_rows(dst + g)],
                                  sem.at[slot]).start()
            return carry
        lax.fori_loop(0, n, copy_seg, 0)
        issued = issued + n
    issued_ref[slot] = issued

    @pl.when(tb == nb - 1)
    def _():
        zero_ref[...] = jnp.zeros_like(zero_ref)
        n_tail = 0
        for e in range(N_EXPERTS + 1):
            n, dst = tail_n_ref[e], tail_dst_ref[e]

            def zero_seg(g, carry):
                pltpu.make_async_copy(zero_ref, xs_hbm.at[_seg_rows(dst + g)], tail_sem).start()
                return carry
            lax.fori_loop(0, n, zero_seg, 0)
            n_tail = n_tail + n
        _wait_segments(tail_sem, zero_ref, n_tail)
        _wait_segments(sem.at[slot], buf, issued_ref[slot])

        @pl.when(nb >= 2)
        def _():
            _wait_segments(sem.at[1 - slot], buf, issued_ref[1 - slot])


def _moe_scatter(h2, plan, route_t):
    m_tok, d = h2.shape
    nb = plan['nb']
    grid_spec = pltpu.PrefetchScalarGridSpec(
        num_scalar_prefetch=5,
        grid=(nb,),
        in_specs=[pl.BlockSpec((MOE_TB, d), lambda t, *_: (t, 0)),
                  pl.BlockSpec((None, SUBLANES, MOE_TB), lambda t, *_: (t, 0, 0))],
        out_specs=pl.BlockSpec(memory_space=pl.ANY),
        scratch_shapes=[pltpu.VMEM((2, MOE_LR, d), BF16), pltpu.VMEM((MOE_SEG, d), BF16),
                        pltpu.SemaphoreType.DMA((2,)), pltpu.SemaphoreType.DMA(()),
                        pltpu.SMEM((2,), jnp.int32)],
    )
    return pl.pallas_call(
        _moe_scatter_kernel,
        grid_spec=grid_spec,
        out_shape=jax.ShapeDtypeStruct((plan['rows'], d), BF16),
        compiler_params=_cparams(("arbitrary",)),
        name="moe_scatter",
    )(plan['seg_n'], plan['seg_src'], plan['seg_dst'], plan['tail_dst'], plan['tail_n'],
      h2, route_t)


def _moe_ffn_kernel(exp_ref, valid_ref, xs_ref, wg_ref, wu_ref, wd_ref, o_ref, acc_ref):
    n = pl.program_id(0)
    j = pl.program_id(1)

    @pl.when(valid_ref[n] == 0)
    def _():
        o_ref[...] = jnp.zeros_like(o_ref)

    @pl.when(valid_ref[n] != 0)
    def _():
        @pl.when(j == 0)
        def _():
            acc_ref[...] = jnp.zeros_like(acc_ref)

        h = xs_ref[...]
        act = (jax.nn.silu(jnp.dot(h, wg_ref[...], preferred_element_type=F32))
               * jnp.dot(h, wu_ref[...], preferred_element_type=F32))
        acc_ref[...] += jnp.dot(act.astype(BF16), wd_ref[...], preferred_element_type=F32)

        @pl.when(j == pl.num_programs(1) - 1)
        def _():
            o_ref[...] = acc_ref[...].astype(o_ref.dtype)


def _moe_ffn(xs, wg, wu, wd, plan, *, tf=1536):
    rows, d = xs.shape
    ff = wg.shape[2]
    tf = min(tf, ff)
    nj = ff // tf

    def ff_tile(n, j, v):
        return j * v[n] + (nj - 1) * (1 - v[n])

    grid_spec = pltpu.PrefetchScalarGridSpec(
        num_scalar_prefetch=2,
        grid=(plan['ntf'], nj),
        in_specs=[pl.BlockSpec((MOE_TF, d), lambda n, j, e, v: (n, 0)),
                  pl.BlockSpec((None, d, tf), lambda n, j, e, v: (e[n], 0, ff_tile(n, j, v))),
                  pl.BlockSpec((None, d, tf), lambda n, j, e, v: (e[n], 0, ff_tile(n, j, v))),
                  pl.BlockSpec((None, tf, d), lambda n, j, e, v: (e[n], ff_tile(n, j, v), 0))],
        out_specs=pl.BlockSpec((MOE_TF, d), lambda n, j, e, v: (n, 0)),
        scratch_shapes=[pltpu.VMEM((MOE_TF, d), F32)],
    )
    return pl.pallas_call(
        _moe_ffn_kernel,
        grid_spec=grid_spec,
        out_shape=jax.ShapeDtypeStruct((rows, d), BF16),
        compiler_params=_cparams(("arbitrary", "arbitrary")),
        name="moe_ffn",
    )(plan['f_exp'], plan['f_valid'], xs, wg, wu, wd)


def _token_columns(route, field):
    pad = jnp.zeros_like(route)
    pick = jnp.where(lax.broadcasted_iota(jnp.int32, (2 * SUBLANES, LANES), 0) == field,
                     1.0, 0.0).astype(BF16)
    out = None
    rest = route
    for _ in range(3):
        piece = rest.astype(BF16)
        rest = rest - piece.astype(F32)
        term = lax.dot_general(jnp.concatenate([piece, pad.astype(BF16)], axis=0), pick,
                               (((0,), (0,)), ((), ())), preferred_element_type=F32)
        out = term if out is None else out + term
    return out


def _moe_combine_kernel(*refs, final_norm):
    if final_norm:
        (seg_n_ref, seg_src_ref, seg_dst_ref, blk_n_ref, ye_hbm, route_ref,
         x1_ref, g2_ref, fg_ref, o_ref, buf_ref, sem) = refs
    else:
        (seg_n_ref, seg_src_ref, seg_dst_ref, blk_n_ref, ye_hbm, route_ref,
         x1_ref, g2_ref, o_ref, buf_ref, sem) = refs
    tb = pl.program_id(0)
    nb = pl.num_programs(0)
    slot = tb % 2

    def fetch(block, into):
        for e in range(N_EXPERTS):
            k = block * N_EXPERTS + e
            n, src, dst = seg_n_ref[k], seg_src_ref[k], seg_dst_ref[k]

            def copy_seg(g, carry):
                pltpu.make_async_copy(ye_hbm.at[_seg_rows(dst + g)],
                                      buf_ref.at[into, _seg_rows(src + g)], sem.at[into]).start()
                return carry
            lax.fori_loop(0, n, copy_seg, 0)

    @pl.when(tb == 0)
    def _():
        fetch(tb, slot)

    @pl.when(tb + 1 < nb)
    def _():
        fetch(tb + 1, 1 - slot)

    buf = buf_ref.at[slot]
    _wait_segments(sem.at[slot], buf, blk_n_ref[tb])

    def clear(g, carry):
        buf[_seg_rows(g), :] = jnp.zeros((MOE_SEG, buf.shape[1]), buf.dtype)
        return carry
    lax.fori_loop(blk_n_ref[tb], MOE_LR // MOE_SEG, clear, 0)

    ye = buf[...]
    reps = MOE_LR // LANES
    cols = lax.broadcasted_iota(jnp.int32, (MOE_TB, MOE_LR), 1)

    route = route_ref[...]

    def unsort(field):
        pos = _token_columns(route, field).astype(jnp.int32)
        hit = jnp.tile(pos, (1, reps)) == cols
        return jnp.dot(jnp.where(hit, 1.0, 0.0).astype(BF16), ye, preferred_element_type=F32)

    lanes = x1_ref.shape[1] // LANES
    y = (jnp.tile(_token_columns(route, 4), (1, lanes)) * unsort(2)
         + jnp.tile(_token_columns(route, 5), (1, lanes)) * unsort(3))
    out = x1_ref[...] + g2_ref[...] * y
    if final_norm:
        out = _rmsnorm(out, fg_ref[...])
    o_ref[...] = out


def _moe_combine(ye, plan, route_t, x1, mod_l, final_g=None):
    bsz, seq, d = x1.shape
    m_tok = bsz * seq
    nb = plan['nb']
    blocks_per_seq = seq // MOE_TB
    final_norm = final_g is not None
    in_specs = [pl.BlockSpec(memory_space=pl.ANY),
                pl.BlockSpec((None, SUBLANES, MOE_TB), lambda t, *_: (t, 0, 0)),
                pl.BlockSpec((MOE_TB, d), lambda t, *_: (t, 0)),
                pl.BlockSpec((None, None, 1, d), lambda t, *_: (t // blocks_per_seq, 5, 0, 0))]
    args = [ye, route_t, x1.reshape(m_tok, d), mod_l]
    if final_norm:
        in_specs.append(pl.BlockSpec((1, d), lambda t, *_: (0, 0)))
        args.append(final_g.reshape(1, d))
    grid_spec = pltpu.PrefetchScalarGridSpec(
        num_scalar_prefetch=4,
        grid=(nb,),
        in_specs=in_specs,
        out_specs=pl.BlockSpec((MOE_TB, d), lambda t, *_: (t, 0)),
        scratch_shapes=[pltpu.VMEM((2, MOE_LR, d), BF16), pltpu.SemaphoreType.DMA((2,))],
    )
    out = pl.pallas_call(
        functools.partial(_moe_combine_kernel, final_norm=final_norm),
        grid_spec=grid_spec,
        out_shape=jax.ShapeDtypeStruct((m_tok, d), F32),
        compiler_params=_cparams(("arbitrary",)),
        name="moe_combine",
    )(plan['seg_n'], plan['seg_src'], plan['seg_dst'], plan['blk_n'], *args)
    return out.reshape(bsz, seq, d)


def _moe(h2, route_t, cnt, x1, mod_l, wg, wu, wd, final_g=None):
    bsz, seq, d = x1.shape
    m_tok = bsz * seq
    plan = _moe_plan(cnt, m_tok=m_tok)
    xs = _moe_scatter(h2.reshape(m_tok, d), plan, route_t)
    ye = _moe_ffn(xs, wg, wu, wd, plan)
    return _moe_combine(ye, plan, route_t, x1, mod_l, final_g)


def kernel(x, c, positions, ada_w, ada_b, ln1_g, ln2_g, w_in, conv_w, conv_b, gate_a_w, gate_a_b, gate_x_w, gate_x_b, lru_lambda, lam_q1, lam_k1, lam_q2, lam_k2, subln_g, w_out, ffn_w_gate, ffn_w_up, ffn_w_down, moe_router, moe_w_gate, moe_w_up, moe_w_down, final_g):
    depth = ada_w.shape[0]
    vd = subln_g.shape[-1]
    dh = vd // 2
    attn_w = DIFF_HEADS * vd

    mod = _modulation(c, ada_w, ada_b)
    cos_t, sin_t = _rope_tables(positions, dh)
    for l in range(depth):
        lambda_init = 0.8 - 0.6 * math.exp(-0.3 * l)
        mod_l = mod[l]
        lru, q, k, v = _inproj(x, mod_l, ln1_g[l], w_in[l].astype(BF16), cos_t, sin_t,
                               conv_w[l], conv_b[l], gate_a_w[l], gate_a_b[l], gate_x_w[l],
                               gate_x_b[l], lru_lambda[l], attn_w=attn_w, dh=dh)
        att = _attention(q, k, v, positions, lam_q1[l], lam_k1[l], lam_q2[l], lam_k2[l],
                         subln_g[l], lambda_init, dh=dh)
        fg = final_g if l == depth - 1 else None
        j = l // 2
        if l % 2 == 0:
            x = _ffn(lru, att, x, mod_l, ln2_g[l], w_out[l].astype(BF16),
                     ffn_w_gate[j].astype(BF16), ffn_w_up[j].astype(BF16),
                     ffn_w_down[j].astype(BF16), final_g=fg)
        else:
            x1, h2, route, cnt = _outproj_router(lru, att, x, mod_l, ln2_g[l],
                                                 w_out[l].astype(BF16), moe_router[j])
            x = _moe(h2, route, cnt, x1, mod_l, moe_w_gate[j].astype(BF16),
                     moe_w_up[j].astype(BF16), moe_w_down[j].astype(BF16), final_g=fg)
    return x
```

```python
import functools
import math

import jax
import jax.numpy as jnp
from jax import lax
from jax.experimental import pallas as pl
from jax.experimental.pallas import tpu as pltpu

F32 = jnp.float32
BF16 = jnp.bfloat16
HIGHEST = lax.Precision.HIGHEST

CHUNK = 64
LRU_BLOCKS = 8
CONV_W = 4
RG_C = 8.0
DIFF_HEADS = 4
ROPE_THETA = 10000.0
N_EXPERTS = 8
EPS = 1e-6
LANES = 128
SUBLANES = 8
VMEM_LIMIT = 56 * 1024 * 1024
MASK_VALUE = -0.5 * float(jnp.finfo(jnp.float32).max)
MASK_BIAS = 2.0 ** 100


def _cparams(sem):
    return pltpu.CompilerParams(dimension_semantics=sem, vmem_limit_bytes=VMEM_LIMIT)


def _rmsnorm(x, g):
    return x * lax.rsqrt(jnp.mean(x * x, axis=-1, keepdims=True) + EPS) * g


def _mod_kernel(c_ref, w_ref, b_ref, o_ref):
    c = c_ref[...]
    s = c * jax.nn.sigmoid(c)
    o_ref[...] = jnp.dot(s, w_ref[...], precision=HIGHEST,
                         preferred_element_type=F32) + b_ref[...]


def _modulation(c, ada_w, ada_b, tn=1024):
    depth, d, n = ada_w.shape
    bsz = c.shape[0]
    rows = -(-bsz // SUBLANES) * SUBLANES
    c_pad = jnp.zeros((rows, d), F32).at[:bsz].set(c)
    out = pl.pallas_call(
        _mod_kernel,
        grid=(depth, n // tn),
        in_specs=[
            pl.BlockSpec((rows, d), lambda l, j: (0, 0)),
            pl.BlockSpec((None, d, tn), lambda l, j: (l, 0, j)),
            pl.BlockSpec((None, 1, tn), lambda l, j: (l, 0, j)),
        ],
        out_specs=pl.BlockSpec((None, rows, tn), lambda l, j: (l, 0, j)),
        out_shape=jax.ShapeDtypeStruct((depth, rows, n), F32),
        compiler_params=_cparams(("arbitrary", "arbitrary")),
        name="adaln_mod",
    )(c_pad, ada_w, ada_b.reshape(depth, 1, n))
    return out[:, :bsz].reshape(depth, bsz, 6, 1, d)


def _rope_table_kernel(pos_ref, inv_ref, cos_ref, sin_ref, *, n_freq):
    ang = pos_ref[...].astype(F32) * inv_ref[...]
    groups = LANES // n_freq
    row = lax.broadcasted_iota(jnp.int32, (LANES, groups * LANES), 0)
    col = lax.broadcasted_iota(jnp.int32, (LANES, groups * LANES), 1)
    hit = row == (col // LANES) * n_freq + col % n_freq
    spread_cos = jnp.where(hit, 1.0, 0.0).astype(BF16)
    first_half = col % (2 * n_freq) < n_freq
    spread_sin = jnp.where(hit, jnp.where(first_half, -1.0, 1.0), 0.0).astype(BF16)

    def spread(t, e):
        out = None
        rest = t
        for _ in range(3):
            piece = rest.astype(BF16)
            rest = rest - piece.astype(F32)
            term = jnp.dot(piece, e, preferred_element_type=F32)
            out = term if out is None else out + term
        return out

    c = spread(jnp.cos(ang), spread_cos)
    s = spread(jnp.sin(ang), spread_sin)
    for g in range(groups):
        cos_ref[g] = c[:, g * LANES:(g + 1) * LANES]
        sin_ref[g] = s[:, g * LANES:(g + 1) * LANES]


def _rope_tables(positions, dh):
    n_freq = dh // 2
    groups = LANES // n_freq
    tok = positions.size
    rows = tok // groups
    inv = ROPE_THETA ** (-jnp.arange(0, dh, 2, dtype=F32) / dh)
    pos_x = jnp.repeat(positions.reshape(groups, rows).T, n_freq, axis=1)
    inv_x = jnp.tile(inv, groups).reshape(1, LANES)
    tr = min(rows, 1024)
    cos, sin = pl.pallas_call(
        functools.partial(_rope_table_kernel, n_freq=n_freq),
        grid=(rows // tr,),
        in_specs=[pl.BlockSpec((tr, LANES), lambda i: (i, 0)),
                  pl.BlockSpec((1, LANES), lambda i: (0, 0))],
        out_specs=[pl.BlockSpec((groups, tr, LANES), lambda i: (0, i, 0))] * 2,
        out_shape=[jax.ShapeDtypeStruct((groups, rows, LANES), F32)] * 2,
        compiler_params=_cparams(("arbitrary",)),
        name="rope_tables",
    )(pos_x, inv_x)
    return cos.reshape(tok, LANES), sin.reshape(tok, LANES)


def _inproj_kernel(x_ref, sc_ref, sh_ref, g_ref, w_ref, cos_ref, sin_ref,
                   cw_ref, cb_ref, wg_ref, bg_ref, lam_ref,
                   lru_ref, q_ref, k_ref, v_ref, xpad_ref, h_ref, *, lru_w, attn_w, dh):
    @pl.when(pl.program_id(1) == 0)
    def _():
        xpad_ref[0:SUBLANES, :] = jnp.zeros((SUBLANES, lru_w), F32)
        h_ref[...] = jnp.zeros_like(h_ref)

    h = _rmsnorm(x_ref[...], g_ref[...]) * (1.0 + sc_ref[...]) + sh_ref[...]
    hb = h.astype(BF16)
    lru2 = 2 * lru_w
    xy = jnp.dot(hb, w_ref[:, :lru2], preferred_element_type=F32)

    reps = attn_w // LANES
    cos = jnp.tile(cos_ref[...], (1, reps))
    sin = jnp.tile(sin_ref[...], (1, reps))
    lane = lax.broadcasted_iota(jnp.int32, cos.shape, 1)
    first_half = (lane % dh) < (dh // 2)

    def rope(t):
        fwd = pltpu.roll(t, attn_w - dh // 2, axis=1)
        bwd = pltpu.roll(t, dh // 2, axis=1)
        return t * cos + jnp.where(first_half, fwd, bwd) * sin

    q = jnp.dot(hb, w_ref[:, lru2:lru2 + attn_w], preferred_element_type=F32)
    q_ref[...] = (rope(q) * (dh ** -0.5 * math.log2(math.e))).astype(BF16)
    k = jnp.dot(hb, w_ref[:, lru2 + attn_w:lru2 + 2 * attn_w], preferred_element_type=F32)
    k_ref[...] = rope(k).astype(BF16)
    v = jnp.dot(hb, w_ref[:, lru2 + 2 * attn_w:], preferred_element_type=F32)
    v_ref[...] = v.astype(BF16)

    u, gates = _lru_conv_gates(xy[:, :lru_w], cw_ref, cb_ref, wg_ref, bg_ref, xpad_ref)
    a, bt = _lru_coeffs(u, gates, lam_ref)
    lru_ref[...] = _lru_scan(a, bt, xy[:, lru_w:], h_ref).astype(lru_ref.dtype)


def _inproj(x, mod_l, ln_g, w_in_b, cos_t, sin_t, conv_w, conv_b, wa, ba, wx, bx, lam, *,
            attn_w, dh, tm=512):
    bsz, seq, d = x.shape
    nt = seq // tm
    d_in = w_in_b.shape[1]
    lru_w = conv_w.shape[-1]
    wg = jnp.concatenate([_block_diag(wa), _block_diag(wx)], axis=1).astype(BF16)
    bg = jnp.concatenate([ba, bx]).reshape(1, 2 * lru_w)
    row = lambda k: pl.BlockSpec((None, None, 1, d), lambda b, i: (b, k, 0, 0))
    tok = lambda w: pl.BlockSpec((None, tm, w), lambda b, i: (b, i, 0))
    const = lambda shape: pl.BlockSpec(shape, lambda b, i: (0,) * len(shape))
    return pl.pallas_call(
        functools.partial(_inproj_kernel, lru_w=lru_w, attn_w=attn_w, dh=dh),
        grid=(bsz, nt),
        in_specs=[
            tok(d), row(1), row(0), const((1, d)), const((d, d_in)),
            pl.BlockSpec((tm, LANES), lambda b, i: (b * nt + i, 0)),
            pl.BlockSpec((tm, LANES), lambda b, i: (b * nt + i, 0)),
            const((CONV_W, lru_w)), const((1, lru_w)), const((lru_w, 2 * lru_w)),
            const((1, 2 * lru_w)), const((1, lru_w)),
        ],
        out_specs=[tok(lru_w), tok(attn_w), tok(attn_w), tok(attn_w)],
        out_shape=[jax.ShapeDtypeStruct((bsz, seq, lru_w), BF16),
                   jax.ShapeDtypeStruct((bsz, seq, attn_w), BF16),
                   jax.ShapeDtypeStruct((bsz, seq, attn_w), BF16),
                   jax.ShapeDtypeStruct((bsz, seq, attn_w), BF16)],
        scratch_shapes=[pltpu.VMEM((tm + SUBLANES, lru_w), F32), pltpu.VMEM((1, lru_w), F32)],
        compiler_params=_cparams(("arbitrary", "arbitrary")),
        name="inproj_lru",
    )(x, mod_l, mod_l, ln_g.reshape(1, d), w_in_b, cos_t, sin_t,
      conv_w, conv_b.reshape(1, lru_w), wg, bg, lam.reshape(1, lru_w))


def _gelu_tanh(x):
    return 0.5 * x * (1.0 + jnp.tanh(math.sqrt(2.0 / math.pi) * (x + 0.044715 * (x * x * x))))


def _lru_conv_gates(xr, cw_ref, cb_ref, wg_ref, bg_ref, xpad_ref):
    t, w = xr.shape
    xpad_ref[SUBLANES:SUBLANES + t, :] = xr
    u = cb_ref[...]
    for j in range(CONV_W):
        off = SUBLANES - (CONV_W - 1) + j
        u = u + cw_ref[j:j + 1, :] * xpad_ref[off:off + t, :]
    xpad_ref[0:SUBLANES, :] = xpad_ref[t:t + SUBLANES, :]
    gates = jnp.dot(u.astype(BF16), wg_ref[...], preferred_element_type=F32) + bg_ref[...]
    return u, gates


def _lru_coeffs(u, gates, lam_ref):
    w = u.shape[1]
    r = jax.nn.sigmoid(gates[:, :w])
    ig = jax.nn.sigmoid(gates[:, w:])
    neg_lam = -lam_ref[...]
    softplus = jnp.maximum(neg_lam, 0.0) + jnp.log1p(jnp.exp(-jnp.abs(neg_lam)))
    log_a = (-RG_C) * r * softplus
    a = jnp.exp(log_a)
    return a, jnp.sqrt(1.0 - a * a) * (ig * u)


def _lru_scan(a, bt, yr, h_ref):
    t, w = a.shape
    groups = t // SUBLANES
    a = a.reshape(groups, SUBLANES, w)
    bt = bt.reshape(groups, SUBLANES, w)
    sub = lax.broadcasted_iota(jnp.int32, a.shape, 1)
    shift = 1
    while shift < SUBLANES:
        keep = sub >= shift
        a_prev = jnp.where(keep, pltpu.roll(a, shift, axis=1), 1.0)
        b_prev = jnp.where(keep, pltpu.roll(bt, shift, axis=1), 0.0)
        bt = a * b_prev + bt
        a = a * a_prev
        shift *= 2
    carry = h_ref[...]
    rows = []
    for g in range(groups):
        hg = a[g] * carry + bt[g]
        rows.append(hg)
        carry = hg[SUBLANES - 1:SUBLANES, :]
    h_ref[...] = carry
    return jnp.concatenate(rows, axis=0) * _gelu_tanh(yr)


def _block_diag(wb):
    n, bw, _ = wb.shape
    eye = jnp.eye(n, dtype=wb.dtype)
    return jnp.einsum('nhk,nm->nhmk', wb, eye).reshape(n * bw, n * bw)


def _attn_kernel(qmin_ref, qmax_ref, kmin_ref, kmax_ref,
                 q_ref, k_ref, v_ref, cq_ref, ck_ref, ckcol_ref, lq1_ref, lk1_ref, lq2_ref, lk2_ref,
                 g_ref, o_ref, m_ref, l_ref, acc_ref, *, tq, tk, nk, dh, lambda_init):
    b = pl.program_id(0)
    i = pl.program_id(2)
    m_ref[...] = jnp.full(m_ref.shape, -jnp.inf, F32)
    l_ref[...] = jnp.zeros(l_ref.shape, F32)
    acc_ref[...] = jnp.zeros(acc_ref.shape, F32)

    q = q_ref[...]
    lane = lax.broadcasted_iota(jnp.int32, q.shape, 1)
    qc = (jnp.where(lane < dh, q, jnp.zeros_like(q)), jnp.where(lane >= dh, q, jnp.zeros_like(q)))
    q_lo = qmin_ref[b, i]
    q_hi = qmax_ref[b, i]

    def process(j, mode):
        start = pl.multiple_of(j * tk, tk)
        kb = k_ref[pl.ds(start, tk), :]
        vb = v_ref[pl.ds(start, tk), :]
        lhs = qc
        if mode == "select":
            ck = ck_ref[:, pl.ds(start, tk)]
            visible = ck <= jnp.tile(cq_ref[...], (1, tk // LANES))
        elif mode == "folded":
            c0 = kmin_ref[b, j]
            lane_k = lax.broadcasted_iota(jnp.int32, (tk, LANES), 1)
            k_chunk = jnp.where(ckcol_ref[pl.ds(start, tk), :] - c0 == lane_k, 1.0, 0.0)
            kb = jnp.concatenate([kb, k_chunk.astype(BF16)], axis=1)
            lane_q = lax.broadcasted_iota(jnp.int32, (tq, LANES), 1)
            q_bias = jnp.where(cq_ref[...] - c0 < lane_q, -MASK_BIAS, 0.0).astype(BF16)
            lhs = tuple(jnp.concatenate([t, q_bias], axis=1) for t in qc)
        for c in range(2):
            s = lax.dot_general(lhs[c], kb, (((1,), (1,)), ((), ())),
                                preferred_element_type=F32)
            if mode == "select":
                s = jnp.where(visible, s, MASK_VALUE)
            m_prev = m_ref[c]
            m_new = jnp.maximum(m_prev, jnp.max(s, axis=-1, keepdims=True))
            alpha = jnp.exp2(m_prev - m_new)
            p = jnp.exp2(s - jnp.tile(m_new, (1, tk // LANES)))
            p_lanes = p[:, :LANES]
            for t in range(1, tk // LANES):
                p_lanes = p_lanes + p[:, t * LANES:(t + 1) * LANES]
            l_ref[c] = alpha * l_ref[c] + p_lanes
            acc_ref[c] = alpha * acc_ref[c] + jnp.dot(p.astype(BF16), vb,
                                                      preferred_element_type=F32)
            m_ref[c] = m_new

    def body(j, carry):
        k_lo = kmin_ref[b, j]
        k_hi = kmax_ref[b, j]
        needed = k_lo <= q_hi
        needs_mask = jnp.logical_and(needed, k_hi > q_lo)
        foldable = k_hi - k_lo < LANES

        @pl.when(jnp.logical_and(needs_mask, foldable))
        def _():
            process(j, "folded")

        @pl.when(jnp.logical_and(needs_mask, jnp.logical_not(foldable)))
        def _():
            process(j, "select")

        @pl.when(jnp.logical_and(needed, jnp.logical_not(needs_mask)))
        def _():
            process(j, "plain")

        return carry

    lax.fori_loop(0, nk, body, 0)

    lam = (jnp.exp(jnp.sum(lq1_ref[...] * lk1_ref[...], keepdims=True))
           - jnp.exp(jnp.sum(lq2_ref[...] * lk2_ref[...], keepdims=True)) + lambda_init)
    l0 = jnp.sum(l_ref[0], axis=-1, keepdims=True)
    l1 = jnp.sum(l_ref[1], axis=-1, keepdims=True)
    o = acc_ref[0] / l0 - lam * (acc_ref[1] / l1)
    o_ref[...] = (_rmsnorm(o, g_ref[...]) * (1.0 - lambda_init)).astype(o_ref.dtype)


def _attention(q, k, v, positions, lq1, lk1, lq2, lk2, subln_g, lambda_init, *,
               dh, tq=1024, tk=1024):
    bsz, seq, aw = q.shape
    vd = 2 * dh
    heads = aw // vd
    nq, nk = seq // tq, seq // tk
    chunk = positions // CHUNK
    qmin = chunk.reshape(bsz, nq, tq).min(-1)
    qmax = chunk.reshape(bsz, nq, tq).max(-1)
    kmin = chunk.reshape(bsz, nk, tk).min(-1)
    kmax = chunk.reshape(bsz, nk, tk).max(-1)
    cq = jnp.broadcast_to(chunk[:, :, None], (bsz, seq, LANES))
    ck = chunk.reshape(bsz, 1, seq)
    vec = lambda n: pl.BlockSpec((1, n), lambda b, h, i, *_: (0, 0))
    grid_spec = pltpu.PrefetchScalarGridSpec(
        num_scalar_prefetch=4,
        grid=(bsz, heads, nq),
        in_specs=[
            pl.BlockSpec((None, tq, vd), lambda b, h, i, *_: (b, i, h)),
            pl.BlockSpec((None, seq, vd), lambda b, h, i, *_: (b, 0, h)),
            pl.BlockSpec((None, seq, vd), lambda b, h, i, *_: (b, 0, h)),
            pl.BlockSpec((None, tq, LANES), lambda b, h, i, *_: (b, i, 0)),
            pl.BlockSpec((None, 1, seq), lambda b, h, i, *_: (b, 0, 0)),
            pl.BlockSpec((None, seq, LANES), lambda b, h, i, *_: (b, 0, 0)),
            vec(dh), vec(dh), vec(dh), vec(dh), vec(vd),
        ],
        out_specs=pl.BlockSpec((None, tq, vd), lambda b, h, i, *_: (b, i, h)),
        scratch_shapes=[pltpu.VMEM((2, tq, LANES), F32), pltpu.VMEM((2, tq, LANES), F32),
                        pltpu.VMEM((2, tq, vd), F32)],
    )
    return pl.pallas_call(
        functools.partial(_attn_kernel, tq=tq, tk=tk, nk=nk, dh=dh, lambda_init=lambda_init),
        grid_spec=grid_spec,
        out_shape=jax.ShapeDtypeStruct((bsz, seq, aw), BF16),
        compiler_params=_cparams(("arbitrary", "arbitrary", "arbitrary")),
        name="diff_attn",
    )(qmin, qmax, kmin, kmax, q, k, v, cq, ck, cq,
      lq1.reshape(1, dh), lk1.reshape(1, dh), lq2.reshape(1, dh), lk2.reshape(1, dh),
      subln_g.reshape(1, vd))


def _outproj_router_kernel(lru_ref, att_ref, x_ref, g1_ref, sc_ref, sh_ref, ln_ref, wo_ref,
                           rt_ref, x1_ref, h2_ref, route_ref, cnt_ref, *, w):
    y = (jnp.dot(lru_ref[...], wo_ref[:w, :], preferred_element_type=F32)
         + jnp.dot(att_ref[...], wo_ref[w:, :], preferred_element_type=F32))
    x1 = x_ref[...] + g1_ref[...] * y
    x1_ref[...] = x1
    h2 = _rmsnorm(x1, ln_ref[...]) * (1.0 + sc_ref[...]) + sh_ref[...]
    h2_ref[...] = h2.astype(BF16)
    _route(h2, rt_ref, route_ref, cnt_ref)


def _route(h2, rt_ref, route_ref, cnt_ref):
    def split(v):
        hi = v.astype(BF16)
        return hi, (v - hi.astype(F32)).astype(BF16)

    h_hi, h_lo = split(h2)
    r_hi, r_lo = split(rt_ref[...])
    nt_dims = (((1,), (1,)), ((), ()))
    dot_nt = lambda a, b: lax.dot_general(a, b, nt_dims, preferred_element_type=F32)
    logits = dot_nt(r_hi, h_hi) + (dot_nt(r_hi, h_lo) + dot_nt(r_lo, h_hi))
    n_rows, tm = logits.shape
    row = lax.broadcasted_iota(jnp.int32, logits.shape, 0)
    lg = jnp.where(row < N_EXPERTS, logits, -jnp.inf)
    m1 = jnp.max(lg, axis=0, keepdims=True)
    i1 = jnp.min(jnp.where(lg == m1, row, n_rows), axis=0, keepdims=True)
    lg2 = jnp.where(row == i1, -jnp.inf, lg)
    m2 = jnp.max(lg2, axis=0, keepdims=True)
    i2 = jnp.min(jnp.where(lg2 == m2, row, n_rows), axis=0, keepdims=True)
    e2 = jnp.exp(m2 - m1)
    w1 = 1.0 / (1.0 + e2)
    w2 = e2 / (1.0 + e2)

    onehot = jnp.where(row == i1, 1.0, jnp.where(row == i2, 1.0, 0.0))
    tri = (lax.broadcasted_iota(jnp.int32, (tm, tm), 0)
           < lax.broadcasted_iota(jnp.int32, (tm, tm), 1))
    prefix = jnp.dot(onehot.astype(BF16), jnp.where(tri, 1.0, 0.0).astype(BF16),
                     preferred_element_type=F32)
    count = jnp.sum(onehot, axis=1, keepdims=True)
    seg_len = jnp.floor((count + (MOE_SEG - 1)) * (1.0 / MOE_SEG)) * MOE_SEG
    seg_off = jnp.zeros_like(seg_len)
    for e in range(N_EXPERTS - 1):
        seg_off = seg_off + jnp.where(row[:, :1] > e, seg_len[e:e + 1, :], 0.0)
    local = prefix + seg_off
    pos1 = jnp.sum(jnp.where(row == i1, local, 0.0), axis=0, keepdims=True)
    pos2 = jnp.sum(jnp.where(row == i2, local, 0.0), axis=0, keepdims=True)
    cnt_ref[...] = jnp.broadcast_to(count, cnt_ref.shape)
    fields = (i1.astype(F32), i2.astype(F32), pos1, pos2, w1, w2)
    field_row = lax.broadcasted_iota(jnp.int32, route_ref.shape, 0)
    route = jnp.zeros(route_ref.shape, F32)
    for n, val in enumerate(fields):
        route = jnp.where(field_row == n, val, route)
    route_ref[...] = route


def _outproj_router(lru, att, x, mod_l, ln_g, w_out_b, router):
    bsz, seq, d = x.shape
    w = lru.shape[-1]
    tm = MOE_TB
    nt = seq // tm
    e_rows = 2 * SUBLANES
    rt = jnp.zeros((e_rows, d), F32).at[:N_EXPERTS].set(router.T)
    row = lambda k: pl.BlockSpec((None, None, 1, d), lambda b, i: (b, k, 0, 0))
    tok = lambda n: pl.BlockSpec((None, tm, n), lambda b, i: (b, i, 0))
    return pl.pallas_call(
        functools.partial(_outproj_router_kernel, w=w),
        grid=(bsz, nt),
        in_specs=[tok(w), tok(w), tok(d), row(2), row(4), row(3),
                  pl.BlockSpec((1, d), lambda b, i: (0, 0)),
                  pl.BlockSpec((d, d), lambda b, i: (0, 0)),
                  pl.BlockSpec((e_rows, d), lambda b, i: (0, 0))],
        out_specs=[
            tok(d), tok(d),
            pl.BlockSpec((None, SUBLANES, tm), lambda b, i: (b * nt + i, 0, 0)),
            pl.BlockSpec((e_rows, LANES), lambda b, i: (b * nt + i, 0))],
        out_shape=[jax.ShapeDtypeStruct((bsz, seq, d), F32),
                   jax.ShapeDtypeStruct((bsz, seq, d), BF16),
                   jax.ShapeDtypeStruct((bsz * nt, SUBLANES, tm), F32),
                   jax.ShapeDtypeStruct((bsz * nt * e_rows, LANES), F32)],
        compiler_params=_cparams(("arbitrary", "arbitrary")),
        name="outproj_router",
    )(lru, att, x, mod_l, mod_l, mod_l, ln_g.reshape(1, d), w_out_b, rt)


def _ffn_kernel(*refs, w, final_norm):
    (lru_ref, att_ref, x_ref, g1_ref, sc_ref, sh_ref, ln_ref, wo_ref,
     wg_ref, wu_ref, wd_ref, g2_ref) = refs[:12]
    fg_ref = refs[12] if final_norm else None
    o_ref, x1_ref, h2_ref, acc_ref = refs[-4:]
    j = pl.program_id(2)

    @pl.when(j == 0)
    def _():
        y = (jnp.dot(lru_ref[...], wo_ref[:w, :], preferred_element_type=F32)
             + jnp.dot(att_ref[...], wo_ref[w:, :], preferred_element_type=F32))
        x1 = x_ref[...] + g1_ref[...] * y
        x1_ref[...] = x1
        h2 = _rmsnorm(x1, ln_ref[...]) * (1.0 + sc_ref[...]) + sh_ref[...]
        h2_ref[...] = h2.astype(BF16)
        acc_ref[...] = jnp.zeros_like(acc_ref)

    h = h2_ref[...]
    act = (jax.nn.silu(jnp.dot(h, wg_ref[...], preferred_element_type=F32))
           * jnp.dot(h, wu_ref[...], preferred_element_type=F32))
    acc_ref[...] += jnp.dot(act.astype(BF16), wd_ref[...], preferred_element_type=F32)

    @pl.when(j == pl.num_programs(2) - 1)
    def _():
        out = x1_ref[...] + g2_ref[...] * acc_ref[...]
        if final_norm:
            out = _rmsnorm(out, fg_ref[...])
        o_ref[...] = out


def _ffn(lru, att, x, mod_l, ln_g, w_out_b, wg, wu, wd, final_g=None, *, tm=512, tf=1536):
    bsz, seq, d = x.shape
    w = lru.shape[-1]
    ff = wg.shape[1]
    tf = min(tf, ff)
    final_norm = final_g is not None
    row = lambda k: pl.BlockSpec((None, None, 1, d), lambda b, i, j: (b, k, 0, 0))
    tok = lambda n: pl.BlockSpec((None, tm, n), lambda b, i, j: (b, i, 0))
    in_specs = [tok(w), tok(w), tok(d), row(2), row(4), row(3),
                pl.BlockSpec((1, d), lambda b, i, j: (0, 0)),
                pl.BlockSpec((d, d), lambda b, i, j: (0, 0)),
                pl.BlockSpec((d, tf), lambda b, i, j: (0, j)),
                pl.BlockSpec((d, tf), lambda b, i, j: (0, j)),
                pl.BlockSpec((tf, d), lambda b, i, j: (j, 0)),
                row(5)]
    args = [lru, att, x, mod_l, mod_l, mod_l, ln_g.reshape(1, d), w_out_b, wg, wu, wd, mod_l]
    if final_norm:
        in_specs.append(pl.BlockSpec((1, d), lambda b, i, j: (0, 0)))
        args.append(final_g.reshape(1, d))
    return pl.pallas_call(
        functools.partial(_ffn_kernel, w=w, final_norm=final_norm),
        grid=(bsz, seq // tm, ff // tf),
        in_specs=in_specs,
        out_specs=tok(d),
        out_shape=jax.ShapeDtypeStruct((bsz, seq, d), F32),
        scratch_shapes=[pltpu.VMEM((tm, d), F32), pltpu.VMEM((tm, d), BF16),
                        pltpu.VMEM((tm, d), F32)],
        compiler_params=_cparams(("arbitrary",) * 3),
        name="outproj_ffn",
    )(*args)


MOE_TB = 512
MOE_SEG = 16
MOE_TF = 512
MOE_LR = 2 * MOE_TB + N_EXPERTS * MOE_SEG


def _moe_plan(cnt, *, m_tok):
    i32 = jnp.int32
    nb = m_tok // MOE_TB
    n_e = N_EXPERTS
    rows = -(-(2 * m_tok + nb * n_e * MOE_SEG + n_e * MOE_TF) // MOE_TF) * MOE_TF
    ntf = rows // MOE_TF
    n = cnt.reshape(nb, -1, LANES)[:, :n_e, 0].astype(i32)
    seg_n = (n + MOE_SEG - 1) // MOE_SEG
    seg_src = jnp.cumsum(seg_n, axis=1) - seg_n
    used = jnp.sum(seg_n, axis=0)
    per_tile = MOE_TF // MOE_SEG
    gsz = (used + per_tile - 1) // per_tile * per_tile
    gend = jnp.cumsum(gsz)
    goff = gend - gsz
    seg_dst = goff[None, :] + jnp.cumsum(seg_n, axis=0) - seg_n
    total_tiles = gend[-1] // per_tile
    tile = jnp.arange(ntf, dtype=i32)
    f_valid = (tile < total_tiles).astype(i32)
    f_exp = jnp.minimum(
        jnp.sum(gend[None, :] <= (jnp.minimum(tile, total_tiles - 1) * per_tile)[:, None],
                axis=1).astype(i32), n_e - 1)
    tail_dst = jnp.concatenate([goff + used, gend[-1:]])
    tail_n = jnp.concatenate([gsz - used, rows // MOE_SEG - gend[-1:]])
    return dict(seg_n=seg_n.reshape(-1), seg_src=seg_src.reshape(-1),
                seg_dst=seg_dst.reshape(-1), blk_n=jnp.sum(seg_n, axis=1),
                tail_dst=tail_dst, tail_n=tail_n,
                f_exp=f_exp, f_valid=f_valid, rows=rows, ntf=ntf, nb=nb)


def _seg_rows(unit):
    return pl.ds(pl.multiple_of(unit * MOE_SEG, MOE_SEG), MOE_SEG)


def _wait_segments(sem, buf_ref, n):
    def body(_, carry):
        pltpu.make_async_copy(buf_ref.at[pl.ds(0, MOE_SEG)], buf_ref.at[pl.ds(0, MOE_SEG)],
                              sem).wait()
        return carry
    lax.fori_loop(0, n, body, 0)


def _moe_scatter_kernel(seg_n_ref, seg_src_ref, seg_dst_ref, tail_dst_ref, tail_n_ref,
                        h_ref, route_ref, xs_hbm, buf_ref, zero_ref, sem, tail_sem,
                        issued_ref):
    tb = pl.program_id(0)
    nb = pl.num_programs(0)
    slot = tb % 2
    buf = buf_ref.at[slot]

    @pl.when(tb >= 2)
    def _():
        _wait_segments(sem.at[slot], buf, issued_ref[slot])

    rows = lax.broadcasted_iota(jnp.int32, (MOE_LR, MOE_TB), 0)
    p1 = route_ref[2:3, :].astype(jnp.int32)
    p2 = route_ref[3:4, :].astype(jnp.int32)
    sel = jnp.where(p1 == rows, 1.0, jnp.where(p2 == rows, 1.0, 0.0))
    buf[...] = jnp.dot(sel.astype(BF16), h_ref[...],
                       preferred_element_type=F32).astype(buf_ref.dtype)

    issued = 0
    for e in range(N_EXPERTS):
        k = tb * N_EXPERTS + e
        n, src, dst = seg_n_ref[k], seg_src_ref[k], seg_dst_ref[k]

        def copy_seg(g, carry):
            pltpu.make_async_copy(buf.at[_seg_rows(src + g)], xs_hbm.at[_seg_rows(dst + g)],
                                  sem.at[slot]).start()
            return carry
        lax.fori_loop(0, n, copy_seg, 0)
        issued = issued + n
    issued_ref[slot] = issued

    @pl.when(tb == nb - 1)
    def _():
        zero_ref[...] = jnp.zeros_like(zero_ref)
        n_tail = 0
        for e in range(N_EXPERTS + 1):
            n, dst = tail_n_ref[e], tail_dst_ref[e]

            def zero_seg(g, carry):
                pltpu.make_async_copy(zero_ref, xs_hbm.at[_seg_rows(dst + g)], tail_sem).start()
                return carry
            lax.fori_loop(0, n, zero_seg, 0)
            n_tail = n_tail + n
        _wait_segments(tail_sem, zero_ref, n_tail)
        _wait_segments(sem.at[slot], buf, issued_ref[slot])

        @pl.when(nb >= 2)
        def _():
            _wait_segments(sem.at[1 - slot], buf, issued_ref[1 - slot])


def _moe_scatter(h2, plan, route_t):
    m_tok, d = h2.shape
    nb = plan['nb']
    grid_spec = pltpu.PrefetchScalarGridSpec(
        num_scalar_prefetch=5,
        grid=(nb,),
        in_specs=[pl.BlockSpec((MOE_TB, d), lambda t, *_: (t, 0)),
                  pl.BlockSpec((None, SUBLANES, MOE_TB), lambda t, *_: (t, 0, 0))],
        out_specs=pl.BlockSpec(memory_space=pl.ANY),
        scratch_shapes=[pltpu.VMEM((2, MOE_LR, d), BF16), pltpu.VMEM((MOE_SEG, d), BF16),
                        pltpu.SemaphoreType.DMA((2,)), pltpu.SemaphoreType.DMA(()),
                        pltpu.SMEM((2,), jnp.int32)],
    )
    return pl.pallas_call(
        _moe_scatter_kernel,
        grid_spec=grid_spec,
        out_shape=jax.ShapeDtypeStruct((plan['rows'], d), BF16),
        compiler_params=_cparams(("arbitrary",)),
        name="moe_scatter",
    )(plan['seg_n'], plan['seg_src'], plan['seg_dst'], plan['tail_dst'], plan['tail_n'],
      h2, route_t)


def _moe_ffn_kernel(exp_ref, valid_ref, xs_ref, wg_ref, wu_ref, wd_ref, o_ref, acc_ref):
    n = pl.program_id(0)
    j = pl.program_id(1)

    @pl.when(valid_ref[n] == 0)
    def _():
        o_ref[...] = jnp.zeros_like(o_ref)

    @pl.when(valid_ref[n] != 0)
    def _():
        @pl.when(j == 0)
        def _():
            acc_ref[...] = jnp.zeros_like(acc_ref)

        h = xs_ref[...]
        act = (jax.nn.silu(jnp.dot(h, wg_ref[...], preferred_element_type=F32))
               * jnp.dot(h, wu_ref[...], preferred_element_type=F32))
        acc_ref[...] += jnp.dot(act.astype(BF16), wd_ref[...], preferred_element_type=F32)

        @pl.when(j == pl.num_programs(1) - 1)
        def _():
            o_ref[...] = acc_ref[...].astype(o_ref.dtype)


def _moe_ffn(xs, wg, wu, wd, plan, *, tf=1536):
    rows, d = xs.shape
    ff = wg.shape[2]
    tf = min(tf, ff)
    nj = ff // tf

    def ff_tile(n, j, v):
        return j * v[n] + (nj - 1) * (1 - v[n])

    grid_spec = pltpu.PrefetchScalarGridSpec(
        num_scalar_prefetch=2,
        grid=(plan['ntf'], nj),
        in_specs=[pl.BlockSpec((MOE_TF, d), lambda n, j, e, v: (n, 0)),
                  pl.BlockSpec((None, d, tf), lambda n, j, e, v: (e[n], 0, ff_tile(n, j, v))),
                  pl.BlockSpec((None, d, tf), lambda n, j, e, v: (e[n], 0, ff_tile(n, j, v))),
                  pl.BlockSpec((None, tf, d), lambda n, j, e, v: (e[n], ff_tile(n, j, v), 0))],
        out_specs=pl.BlockSpec((MOE_TF, d), lambda n, j, e, v: (n, 0)),
        scratch_shapes=[pltpu.VMEM((MOE_TF, d), F32)],
    )
    return pl.pallas_call(
        _moe_ffn_kernel,
        grid_spec=grid_spec,
        out_shape=jax.ShapeDtypeStruct((rows, d), BF16),
        compiler_params=_cparams(("arbitrary", "arbitrary")),
        name="moe_ffn",
    )(plan['f_exp'], plan['f_valid'], xs, wg, wu, wd)


def _token_columns(route, field):
    pad = jnp.zeros_like(route)
    pick = jnp.where(lax.broadcasted_iota(jnp.int32, (2 * SUBLANES, LANES), 0) == field,
                     1.0, 0.0).astype(BF16)
    out = None
    rest = route
    for _ in range(3):
        piece = rest.astype(BF16)
        rest = rest - piece.astype(F32)
        term = lax.dot_general(jnp.concatenate([piece, pad.astype(BF16)], axis=0), pick,
                               (((0,), (0,)), ((), ())), preferred_element_type=F32)
        out = term if out is None else out + term
    return out


def _moe_combine_kernel(*refs, final_norm):
    if final_norm:
        (seg_n_ref, seg_src_ref, seg_dst_ref, blk_n_ref, ye_hbm, route_ref,
         x1_ref, g2_ref, fg_ref, o_ref, buf_ref, sem) = refs
    else:
        (seg_n_ref, seg_src_ref, seg_dst_ref, blk_n_ref, ye_hbm, route_ref,
         x1_ref, g2_ref, o_ref, buf_ref, sem) = refs
    tb = pl.program_id(0)
    nb = pl.num_programs(0)
    slot = tb % 2

    def fetch(block, into):
        for e in range(N_EXPERTS):
            k = block * N_EXPERTS + e
            n, src, dst = seg_n_ref[k], seg_src_ref[k], seg_dst_ref[k]

            def copy_seg(g, carry):
                pltpu.make_async_copy(ye_hbm.at[_seg_rows(dst + g)],
                                      buf_ref.at[into, _seg_rows(src + g)], sem.at[into]).start()
                return carry
            lax.fori_loop(0, n, copy_seg, 0)

    @pl.when(tb == 0)
    def _():
        fetch(tb, slot)

    @pl.when(tb + 1 < nb)
    def _():
        fetch(tb + 1, 1 - slot)

    buf = buf_ref.at[slot]
    _wait_segments(sem.at[slot], buf, blk_n_ref[tb])

    def clear(g, carry):
        buf[_seg_rows(g), :] = jnp.zeros((MOE_SEG, buf.shape[1]), buf.dtype)
        return carry
    lax.fori_loop(blk_n_ref[tb], MOE_LR // MOE_SEG, clear, 0)

    ye = buf[...]
    reps = MOE_LR // LANES
    cols = lax.broadcasted_iota(jnp.int32, (MOE_TB, MOE_LR), 1)

    route = route_ref[...]

    def unsort(field):
        pos = _token_columns(route, field).astype(jnp.int32)
        hit = jnp.tile(pos, (1, reps)) == cols
        return jnp.dot(jnp.where(hit, 1.0, 0.0).astype(BF16), ye, preferred_element_type=F32)

    lanes = x1_ref.shape[1] // LANES
    y = (jnp.tile(_token_columns(route, 4), (1, lanes)) * unsort(2)
         + jnp.tile(_token_columns(route, 5), (1, lanes)) * unsort(3))
    out = x1_ref[...] + g2_ref[...] * y
    if final_norm:
        out = _rmsnorm(out, fg_ref[...])
    o_ref[...] = out


def _moe_combine(ye, plan, route_t, x1, mod_l, final_g=None):
    bsz, seq, d = x1.shape
    m_tok = bsz * seq
    nb = plan['nb']
    blocks_per_seq = seq // MOE_TB
    final_norm = final_g is not None
    in_specs = [pl.BlockSpec(memory_space=pl.ANY),
                pl.BlockSpec((None, SUBLANES, MOE_TB), lambda t, *_: (t, 0, 0)),
                pl.BlockSpec((MOE_TB, d), lambda t, *_: (t, 0)),
                pl.BlockSpec((None, None, 1, d), lambda t, *_: (t // blocks_per_seq, 5, 0, 0))]
    args = [ye, route_t, x1.reshape(m_tok, d), mod_l]
    if final_norm:
        in_specs.append(pl.BlockSpec((1, d), lambda t, *_: (0, 0)))
        args.append(final_g.reshape(1, d))
    grid_spec = pltpu.PrefetchScalarGridSpec(
        num_scalar_prefetch=4,
        grid=(nb,),
        in_specs=in_specs,
        out_specs=pl.BlockSpec((MOE_TB, d), lambda t, *_: (t, 0)),
        scratch_shapes=[pltpu.VMEM((2, MOE_LR, d), BF16), pltpu.SemaphoreType.DMA((2,))],
    )
    out = pl.pallas_call(
        functools.partial(_moe_combine_kernel, final_norm=final_norm),
        grid_spec=grid_spec,
        out_shape=jax.ShapeDtypeStruct((m_tok, d), F32),
        compiler_params=_cparams(("arbitrary",)),
        name="moe_combine",
    )(plan['seg_n'], plan['seg_src'], plan['seg_dst'], plan['blk_n'], *args)
    return out.reshape(bsz, seq, d)


def _moe(h2, route_t, cnt, x1, mod_l, wg, wu, wd, final_g=None):
    bsz, seq, d = x1.shape
    m_tok = bsz * seq
    plan = _moe_plan(cnt, m_tok=m_tok)
    xs = _moe_scatter(h2.reshape(m_tok, d), plan, route_t)
    ye = _moe_ffn(xs, wg, wu, wd, plan)
    return _moe_combine(ye, plan, route_t, x1, mod_l, final_g)


def kernel(x, c, positions, ada_w, ada_b, ln1_g, ln2_g, w_in, conv_w, conv_b, gate_a_w, gate_a_b, gate_x_w, gate_x_b, lru_lambda, lam_q1, lam_k1, lam_q2, lam_k2, subln_g, w_out, ffn_w_gate, ffn_w_up, ffn_w_down, moe_router, moe_w_gate, moe_w_up, moe_w_down, final_g):
    depth = ada_w.shape[0]
    vd = subln_g.shape[-1]
    dh = vd // 2
    attn_w = DIFF_HEADS * vd

    mod = _modulation(c, ada_w, ada_b)
    cos_t, sin_t = _rope_tables(positions, dh)
    for l in range(depth):
        lambda_init = 0.8 - 0.6 * math.exp(-0.3 * l)
        mod_l = mod[l]
        lru, q, k, v = _inproj(x, mod_l, ln1_g[l], w_in[l].astype(BF16), cos_t, sin_t,
                               conv_w[l], conv_b[l], gate_a_w[l], gate_a_b[l], gate_x_w[l],
                               gate_x_b[l], lru_lambda[l], attn_w=attn_w, dh=dh)
        att = _attention(q, k, v, positions, lam_q1[l], lam_k1[l], lam_q2[l], lam_k2[l],
                         subln_g[l], lambda_init, dh=dh)
        fg = final_g if l == depth - 1 else None
        j = l // 2
        if l % 2 == 0:
            x = _ffn(lru, att, x, mod_l, ln2_g[l], w_out[l].astype(BF16),
                     ffn_w_gate[j].astype(BF16), ffn_w_up[j].astype(BF16),
                     ffn_w_down[j].astype(BF16), final_g=fg)
        else:
            x1, h2, route, cnt = _outproj_router(lru, att, x, mod_l, ln2_g[l],
                                                 w_out[l].astype(BF16), moe_router[j])
            x = _moe(h2, route, cnt, x1, mod_l, moe_w_gate[j].astype(BF16),
                     moe_w_up[j].astype(BF16), moe_w_down[j].astype(BF16), final_g=fg)
    return x
```

```python
import functools
import math

import jax
import jax.numpy as jnp
from jax import lax
from jax.experimental import pallas as pl
from jax.experimental.pallas import tpu as pltpu

F32 = jnp.float32
BF16 = jnp.bfloat16
HIGHEST = lax.Precision.HIGHEST

CHUNK = 64
LRU_BLOCKS = 8
CONV_W = 4
RG_C = 8.0
DIFF_HEADS = 4
ROPE_THETA = 10000.0
N_EXPERTS = 8
EPS = 1e-6
LANES = 128
SUBLANES = 8
VMEM_LIMIT = 56 * 1024 * 1024
MASK_VALUE = -0.5 * float(jnp.finfo(jnp.float32).max)
MASK_BIAS = 2.0 ** 100


def _cparams(sem):
    return pltpu.CompilerParams(dimension_semantics=sem, vmem_limit_bytes=VMEM_LIMIT)


def _rmsnorm(x, g):
    return x * lax.rsqrt(jnp.mean(x * x, axis=-1, keepdims=True) + EPS) * g


def _mod_kernel(c_ref, w_ref, b_ref, o_ref):
    c = c_ref[...]
    s = c * jax.nn.sigmoid(c)
    o_ref[...] = jnp.dot(s, w_ref[...], precision=HIGHEST,
                         preferred_element_type=F32) + b_ref[...]


def _modulation(c, ada_w, ada_b, tn=1024):
    depth, d, n = ada_w.shape
    bsz = c.shape[0]
    rows = -(-bsz // SUBLANES) * SUBLANES
    c_pad = jnp.zeros((rows, d), F32).at[:bsz].set(c)
    out = pl.pallas_call(
        _mod_kernel,
        grid=(depth, n // tn),
        in_specs=[
            pl.BlockSpec((rows, d), lambda l, j: (0, 0)),
            pl.BlockSpec((None, d, tn), lambda l, j: (l, 0, j)),
            pl.BlockSpec((None, 1, tn), lambda l, j: (l, 0, j)),
        ],
        out_specs=pl.BlockSpec((None, rows, tn), lambda l, j: (l, 0, j)),
        out_shape=jax.ShapeDtypeStruct((depth, rows, n), F32),
        compiler_params=_cparams(("arbitrary", "arbitrary")),
        name="adaln_mod",
    )(c_pad, ada_w, ada_b.reshape(depth, 1, n))
    return out[:, :bsz].reshape(depth, bsz, 6, 1, d)


def _rope_table_kernel(pos_ref, inv_ref, cos_ref, sin_ref, *, n_freq):
    ang = pos_ref[...].astype(F32) * inv_ref[...]
    groups = LANES // n_freq
    row = lax.broadcasted_iota(jnp.int32, (LANES, groups * LANES), 0)
    col = lax.broadcasted_iota(jnp.int32, (LANES, groups * LANES), 1)
    hit = row == (col // LANES) * n_freq + col % n_freq
    spread_cos = jnp.where(hit, 1.0, 0.0).astype(BF16)
    first_half = col % (2 * n_freq) < n_freq
    spread_sin = jnp.where(hit, jnp.where(first_half, -1.0, 1.0), 0.0).astype(BF16)

    def spread(t, e):
        out = None
        rest = t
        for _ in range(3):
            piece = rest.astype(BF16)
            rest = rest - piece.astype(F32)
            term = jnp.dot(piece, e, preferred_element_type=F32)
            out = term if out is None else out + term
        return out

    c = spread(jnp.cos(ang), spread_cos)
    s = spread(jnp.sin(ang), spread_sin)
    for g in range(groups):
        cos_ref[g] = c[:, g * LANES:(g + 1) * LANES]
        sin_ref[g] = s[:, g * LANES:(g + 1) * LANES]


def _rope_tables(positions, dh):
    n_freq = dh // 2
    groups = LANES // n_freq
    tok = positions.size
    rows = tok // groups
    inv = ROPE_THETA ** (-jnp.arange(0, dh, 2, dtype=F32) / dh)
    pos_x = jnp.repeat(positions.reshape(groups, rows).T, n_freq, axis=1)
    inv_x = jnp.tile(inv, groups).reshape(1, LANES)
    tr = min(rows, 1024)
    cos, sin = pl.pallas_call(
        functools.partial(_rope_table_kernel, n_freq=n_freq),
        grid=(rows // tr,),
        in_specs=[pl.BlockSpec((tr, LANES), lambda i: (i, 0)),
                  pl.BlockSpec((1, LANES), lambda i: (0, 0))],
        out_specs=[pl.BlockSpec((groups, tr, LANES), lambda i: (0, i, 0))] * 2,
        out_shape=[jax.ShapeDtypeStruct((groups, rows, LANES), F32)] * 2,
        compiler_params=_cparams(("arbitrary",)),
        name="rope_tables",
    )(pos_x, inv_x)
    return cos.reshape(tok, LANES), sin.reshape(tok, LANES)


def _inproj_kernel(x_ref, sc_ref, sh_ref, g_ref, w_ref, cos_ref, sin_ref,
                   cw_ref, cb_ref, wg_ref, bg_ref, lam_ref,
                   lru_ref, q_ref, k_ref, v_ref, xpad_ref, h_ref, *, lru_w, attn_w, dh):
    @pl.when(pl.program_id(1) == 0)
    def _():
        xpad_ref[0:SUBLANES, :] = jnp.zeros((SUBLANES, lru_w), F32)
        h_ref[...] = jnp.zeros_like(h_ref)

    h = _rmsnorm(x_ref[...], g_ref[...]) * (1.0 + sc_ref[...]) + sh_ref[...]
    hb = h.astype(BF16)
    lru2 = 2 * lru_w
    xy = jnp.dot(hb, w_ref[:, :lru2], preferred_element_type=F32)

    reps = attn_w // LANES
    cos = jnp.tile(cos_ref[...], (1, reps))
    sin = jnp.tile(sin_ref[...], (1, reps))
    lane = lax.broadcasted_iota(jnp.int32, cos.shape, 1)
    first_half = (lane % dh) < (dh // 2)

    def rope(t):
        fwd = pltpu.roll(t, attn_w - dh // 2, axis=1)
        bwd = pltpu.roll(t, dh // 2, axis=1)
        return t * cos + jnp.where(first_half, fwd, bwd) * sin

    q = jnp.dot(hb, w_ref[:, lru2:lru2 + attn_w], preferred_element_type=F32)
    q_ref[...] = (rope(q) * (dh ** -0.5 * math.log2(math.e))).astype(BF16)
    k = jnp.dot(hb, w_ref[:, lru2 + attn_w:lru2 + 2 * attn_w], preferred_element_type=F32)
    k_ref[...] = rope(k).astype(BF16)
    v = jnp.dot(hb, w_ref[:, lru2 + 2 * attn_w:], preferred_element_type=F32)
    v_ref[...] = v.astype(BF16)

    u, gates = _lru_conv_gates(xy[:, :lru_w], cw_ref, cb_ref, wg_ref, bg_ref, xpad_ref)
    a, bt = _lru_coeffs(u, gates, lam_ref)
    lru_ref[...] = _lru_scan(a, bt, xy[:, lru_w:], h_ref).astype(lru_ref.dtype)


def _inproj(x, mod_l, ln_g, w_in_b, cos_t, sin_t, conv_w, conv_b, wa, ba, wx, bx, lam, *,
            attn_w, dh, tm=512):
    bsz, seq, d = x.shape
    nt = seq // tm
    d_in = w_in_b.shape[1]
    lru_w = conv_w.shape[-1]
    wg = jnp.concatenate([_block_diag(wa), _block_diag(wx)], axis=1).astype(BF16)
    bg = jnp.concatenate([ba, bx]).reshape(1, 2 * lru_w)
    row = lambda k: pl.BlockSpec((None, None, 1, d), lambda b, i: (b, k, 0, 0))
    tok = lambda w: pl.BlockSpec((None, tm, w), lambda b, i: (b, i, 0))
    const = lambda shape: pl.BlockSpec(shape, lambda b, i: (0,) * len(shape))
    return pl.pallas_call(
        functools.partial(_inproj_kernel, lru_w=lru_w, attn_w=attn_w, dh=dh),
        grid=(bsz, nt),
        in_specs=[
            tok(d), row(1), row(0), const((1, d)), const((d, d_in)),
            pl.BlockSpec((tm, LANES), lambda b, i: (b * nt + i, 0)),
            pl.BlockSpec((tm, LANES), lambda b, i: (b * nt + i, 0)),
            const((CONV_W, lru_w)), const((1, lru_w)), const((lru_w, 2 * lru_w)),
            const((1, 2 * lru_w)), const((1, lru_w)),
        ],
        out_specs=[tok(lru_w), tok(attn_w), tok(attn_w), tok(attn_w)],
        out_shape=[jax.ShapeDtypeStruct((bsz, seq, lru_w), BF16),
                   jax.ShapeDtypeStruct((bsz, seq, attn_w), BF16),
                   jax.ShapeDtypeStruct((bsz, seq, attn_w), BF16),
                   jax.ShapeDtypeStruct((bsz, seq, attn_w), BF16)],
        scratch_shapes=[pltpu.VMEM((tm + SUBLANES, lru_w), F32), pltpu.VMEM((1, lru_w), F32)],
        compiler_params=_cparams(("arbitrary", "arbitrary")),
        name="inproj_lru",
    )(x, mod_l, mod_l, ln_g.reshape(1, d), w_in_b, cos_t, sin_t,
      conv_w, conv_b.reshape(1, lru_w), wg, bg, lam.reshape(1, lru_w))


def _gelu_tanh(x):
    return 0.5 * x * (1.0 + jnp.tanh(math.sqrt(2.0 / math.pi) * (x + 0.044715 * (x * x * x))))


def _lru_conv_gates(xr, cw_ref, cb_ref, wg_ref, bg_ref, xpad_ref):
    t, w = xr.shape
    xpad_ref[SUBLANES:SUBLANES + t, :] = xr
    u = cb_ref[...]
    for j in range(CONV_W):
        off = SUBLANES - (CONV_W - 1) + j
        u = u + cw_ref[j:j + 1, :] * xpad_ref[off:off + t, :]
    xpad_ref[0:SUBLANES, :] = xpad_ref[t:t + SUBLANES, :]
    gates = jnp.dot(u.astype(BF16), wg_ref[...], preferred_element_type=F32) + bg_ref[...]
    return u, gates


def _lru_coeffs(u, gates, lam_ref):
    w = u.shape[1]
    r = jax.nn.sigmoid(gates[:, :w])
    ig = jax.nn.sigmoid(gates[:, w:])
    neg_lam = -lam_ref[...]
    softplus = jnp.maximum(neg_lam, 0.0) + jnp.log1p(jnp.exp(-jnp.abs(neg_lam)))
    log_a = (-RG_C) * r * softplus
    a = jnp.exp(log_a)
    return a, jnp.sqrt(1.0 - a * a) * (ig * u)


def _lru_scan(a, bt, yr, h_ref):
    t, w = a.shape
    groups = t // SUBLANES
    a = a.reshape(groups, SUBLANES, w)
    bt = bt.reshape(groups, SUBLANES, w)
    sub = lax.broadcasted_iota(jnp.int32, a.shape, 1)
    shift = 1
    while shift < SUBLANES:
        keep = sub >= shift
        a_prev = jnp.where(keep, pltpu.roll(a, shift, axis=1), 1.0)
        b_prev = jnp.where(keep, pltpu.roll(bt, shift, axis=1), 0.0)
        bt = a * b_prev + bt
        a = a * a_prev
        shift *= 2
    carry = h_ref[...]
    rows = []
    for g in range(groups):
        hg = a[g] * carry + bt[g]
        rows.append(hg)
        carry = hg[SUBLANES - 1:SUBLANES, :]
    h_ref[...] = carry
    return jnp.concatenate(rows, axis=0) * _gelu_tanh(yr)


def _block_diag(wb):
    n, bw, _ = wb.shape
    eye = jnp.eye(n, dtype=wb.dtype)
    return jnp.einsum('nhk,nm->nhmk', wb, eye).reshape(n * bw, n * bw)


def _attn_kernel(qmin_ref, qmax_ref, kmin_ref, kmax_ref, qhmax_ref, khmin_ref,
                 q_ref, k_ref, v_ref, cq_ref, ck_ref, ckcol_ref, lq1_ref, lk1_ref, lq2_ref, lk2_ref,
                 g_ref, o_ref, m_ref, l_ref, acc_ref, *, tq, tk, nk, dh, lambda_init):
    b = pl.program_id(0)
    i = pl.program_id(2)
    m_ref[...] = jnp.full(m_ref.shape, -jnp.inf, F32)
    l_ref[...] = jnp.zeros(l_ref.shape, F32)
    acc_ref[...] = jnp.zeros(acc_ref.shape, F32)

    q = q_ref[...]
    lane = lax.broadcasted_iota(jnp.int32, q.shape, 1)
    qc = (jnp.where(lane < dh, q, jnp.zeros_like(q)), jnp.where(lane >= dh, q, jnp.zeros_like(q)))
    q_lo = qmin_ref[b, i]
    q_hi = qmax_ref[b, i]

    def process(j, mode, r0=0, rn=tq, k0=0, kn=tk):
        start = pl.multiple_of(j * tk, tk) + k0
        rows = slice(r0, r0 + rn)
        kb = k_ref[pl.ds(start, kn), :]
        vb = v_ref[pl.ds(start, kn), :]
        lhs = tuple(t[rows] for t in qc)
        if mode == "select":
            ck = ck_ref[:, pl.ds(start, kn)]
            visible = ck <= jnp.tile(cq_ref[rows, :], (1, kn // LANES))
        elif mode == "folded":
            c0 = kmin_ref[b, j]
            lane_k = lax.broadcasted_iota(jnp.int32, (kn, LANES), 1)
            k_chunk = jnp.where(ckcol_ref[pl.ds(start, kn), :] - c0 == lane_k, 1.0, 0.0)
            kb = jnp.concatenate([kb, k_chunk.astype(BF16)], axis=1)
            lane_q = lax.broadcasted_iota(jnp.int32, (rn, LANES), 1)
            q_bias = jnp.where(cq_ref[rows, :] - c0 < lane_q, -MASK_BIAS, 0.0).astype(BF16)
            lhs = tuple(jnp.concatenate([t, q_bias], axis=1) for t in lhs)
        for c in range(2):
            s = lax.dot_general(lhs[c], kb, (((1,), (1,)), ((), ())),
                                preferred_element_type=F32)
            if mode == "select":
                s = jnp.where(visible, s, MASK_VALUE)
            m_prev = m_ref[c, rows]
            m_new = jnp.maximum(m_prev, jnp.max(s, axis=-1, keepdims=True))
            alpha = jnp.exp2(m_prev - m_new)
            p = jnp.exp2(s - jnp.tile(m_new, (1, kn // LANES)))
            p_lanes = p[:, :LANES]
            for t in range(1, kn // LANES):
                p_lanes = p_lanes + p[:, t * LANES:(t + 1) * LANES]
            l_ref[c, rows] = alpha * l_ref[c, rows] + p_lanes
            acc_ref[c, rows] = alpha * acc_ref[c, rows] + jnp.dot(
                p.astype(BF16), vb, preferred_element_type=F32)
            m_ref[c, rows] = m_new

    def body(j, carry):
        k_lo = kmin_ref[b, j]
        k_hi = kmax_ref[b, j]
        needed = k_lo <= q_hi
        needs_mask = jnp.logical_and(needed, k_hi > q_lo)
        foldable = k_hi - k_lo < LANES
        fold = jnp.logical_and(needs_mask, foldable)
        corner_hidden = khmin_ref[b, 2 * j + 1] > qhmax_ref[b, 2 * i]

        @pl.when(jnp.logical_and(fold, corner_hidden))
        def _():
            process(j, "folded", 0, tq // 2, 0, tk // 2)
            process(j, "folded", tq // 2, tq // 2, 0, tk)

        @pl.when(jnp.logical_and(fold, jnp.logical_not(corner_hidden)))
        def _():
            process(j, "folded")

        @pl.when(jnp.logical_and(needs_mask, jnp.logical_not(foldable)))
        def _():
            process(j, "select")

        @pl.when(jnp.logical_and(needed, jnp.logical_not(needs_mask)))
        def _():
            process(j, "plain")

        return carry

    lax.fori_loop(0, nk, body, 0)

    lam = (jnp.exp(jnp.sum(lq1_ref[...] * lk1_ref[...], keepdims=True))
           - jnp.exp(jnp.sum(lq2_ref[...] * lk2_ref[...], keepdims=True)) + lambda_init)
    l0 = jnp.sum(l_ref[0], axis=-1, keepdims=True)
    l1 = jnp.sum(l_ref[1], axis=-1, keepdims=True)
    o = acc_ref[0] / l0 - lam * (acc_ref[1] / l1)
    o_ref[...] = (_rmsnorm(o, g_ref[...]) * (1.0 - lambda_init)).astype(o_ref.dtype)


def _attention(q, k, v, positions, lq1, lk1, lq2, lk2, subln_g, lambda_init, *,
               dh, tq=1024, tk=1024):
    bsz, seq, aw = q.shape
    vd = 2 * dh
    heads = aw // vd
    nq, nk = seq // tq, seq // tk
    chunk = positions // CHUNK
    qmin = chunk.reshape(bsz, nq, tq).min(-1)
    qmax = chunk.reshape(bsz, nq, tq).max(-1)
    kmin = chunk.reshape(bsz, nk, tk).min(-1)
    kmax = chunk.reshape(bsz, nk, tk).max(-1)
    qhmax = chunk.reshape(bsz, 2 * nq, tq // 2).max(-1)
    khmin = chunk.reshape(bsz, 2 * nk, tk // 2).min(-1)
    cq = jnp.broadcast_to(chunk[:, :, None], (bsz, seq, LANES))
    ck = chunk.reshape(bsz, 1, seq)
    vec = lambda n: pl.BlockSpec((1, n), lambda b, h, i, *_: (0, 0))
    grid_spec = pltpu.PrefetchScalarGridSpec(
        num_scalar_prefetch=6,
        grid=(bsz, heads, nq),
        in_specs=[
            pl.BlockSpec((None, tq, vd), lambda b, h, i, *_: (b, i, h)),
            pl.BlockSpec((None, seq, vd), lambda b, h, i, *_: (b, 0, h)),
            pl.BlockSpec((None, seq, vd), lambda b, h, i, *_: (b, 0, h)),
            pl.BlockSpec((None, tq, LANES), lambda b, h, i, *_: (b, i, 0)),
            pl.BlockSpec((None, 1, seq), lambda b, h, i, *_: (b, 0, 0)),
            pl.BlockSpec((None, seq, LANES), lambda b, h, i, *_: (b, 0, 0)),
            vec(dh), vec(dh), vec(dh), vec(dh), vec(vd),
        ],
        out_specs=pl.BlockSpec((None, tq, vd), lambda b, h, i, *_: (b, i, h)),
        scratch_shapes=[pltpu.VMEM((2, tq, LANES), F32), pltpu.VMEM((2, tq, LANES), F32),
                        pltpu.VMEM((2, tq, vd), F32)],
    )
    return pl.pallas_call(
        functools.partial(_attn_kernel, tq=tq, tk=tk, nk=nk, dh=dh, lambda_init=lambda_init),
        grid_spec=grid_spec,
        out_shape=jax.ShapeDtypeStruct((bsz, seq, aw), BF16),
        compiler_params=_cparams(("arbitrary", "arbitrary", "arbitrary")),
        name="diff_attn",
    )(qmin, qmax, kmin, kmax, qhmax, khmin, q, k, v, cq, ck, cq,
      lq1.reshape(1, dh), lk1.reshape(1, dh), lq2.reshape(1, dh), lk2.reshape(1, dh),
      subln_g.reshape(1, vd))


def _outproj_router_kernel(lru_ref, att_ref, x_ref, g1_ref, sc_ref, sh_ref, ln_ref, wo_ref,
                           rt_ref, x1_ref, h2_ref, route_ref, cnt_ref, *, w):
    y = (jnp.dot(lru_ref[...], wo_ref[:w, :], preferred_element_type=F32)
         + jnp.dot(att_ref[...], wo_ref[w:, :], preferred_element_type=F32))
    x1 = x_ref[...] + g1_ref[...] * y
    x1_ref[...] = x1
    h2 = _rmsnorm(x1, ln_ref[...]) * (1.0 + sc_ref[...]) + sh_ref[...]
    h2_ref[...] = h2.astype(BF16)
    _route(h2, rt_ref, route_ref, cnt_ref)


def _route(h2, rt_ref, route_ref, cnt_ref):
    def split(v):
        hi = v.astype(BF16)
        return hi, (v - hi.astype(F32)).astype(BF16)

    h_hi, h_lo = split(h2)
    r_hi, r_lo = split(rt_ref[...])
    nt_dims = (((1,), (1,)), ((), ()))
    dot_nt = lambda a, b: lax.dot_general(a, b, nt_dims, preferred_element_type=F32)
    logits = dot_nt(r_hi, h_hi) + (dot_nt(r_hi, h_lo) + dot_nt(r_lo, h_hi))
    n_rows, tm = logits.shape
    row = lax.broadcasted_iota(jnp.int32, logits.shape, 0)
    lg = jnp.where(row < N_EXPERTS, logits, -jnp.inf)
    m1 = jnp.max(lg, axis=0, keepdims=True)
    i1 = jnp.min(jnp.where(lg == m1, row, n_rows), axis=0, keepdims=True)
    lg2 = jnp.where(row == i1, -jnp.inf, lg)
    m2 = jnp.max(lg2, axis=0, keepdims=True)
    i2 = jnp.min(jnp.where(lg2 == m2, row, n_rows), axis=0, keepdims=True)
    e2 = jnp.exp(m2 - m1)
    w1 = 1.0 / (1.0 + e2)
    w2 = e2 / (1.0 + e2)

    onehot = jnp.where(row == i1, 1.0, jnp.where(row == i2, 1.0, 0.0))
    tri = (lax.broadcasted_iota(jnp.int32, (tm, tm), 0)
           < lax.broadcasted_iota(jnp.int32, (tm, tm), 1))
    prefix = jnp.dot(onehot.astype(BF16), jnp.where(tri, 1.0, 0.0).astype(BF16),
                     preferred_element_type=F32)
    count = jnp.sum(onehot, axis=1, keepdims=True)
    seg_len = jnp.floor((count + (MOE_SEG - 1)) * (1.0 / MOE_SEG)) * MOE_SEG
    seg_off = jnp.zeros_like(seg_len)
    for e in range(N_EXPERTS - 1):
        seg_off = seg_off + jnp.where(row[:, :1] > e, seg_len[e:e + 1, :], 0.0)
    local = prefix + seg_off
    pos1 = jnp.sum(jnp.where(row == i1, local, 0.0), axis=0, keepdims=True)
    pos2 = jnp.sum(jnp.where(row == i2, local, 0.0), axis=0, keepdims=True)
    cnt_ref[...] = jnp.broadcast_to(count, cnt_ref.shape)
    fields = (i1.astype(F32), i2.astype(F32), pos1, pos2, w1, w2)
    field_row = lax.broadcasted_iota(jnp.int32, route_ref.shape, 0)
    route = jnp.zeros(route_ref.shape, F32)
    for n, val in enumerate(fields):
        route = jnp.where(field_row == n, val, route)
    route_ref[...] = route


def _outproj_router(lru, att, x, mod_l, ln_g, w_out_b, router):
    bsz, seq, d = x.shape
    w = lru.shape[-1]
    tm = MOE_TB
    nt = seq // tm
    e_rows = 2 * SUBLANES
    rt = jnp.zeros((e_rows, d), F32).at[:N_EXPERTS].set(router.T)
    row = lambda k: pl.BlockSpec((None, None, 1, d), lambda b, i: (b, k, 0, 0))
    tok = lambda n: pl.BlockSpec((None, tm, n), lambda b, i: (b, i, 0))
    return pl.pallas_call(
        functools.partial(_outproj_router_kernel, w=w),
        grid=(bsz, nt),
        in_specs=[tok(w), tok(w), tok(d), row(2), row(4), row(3),
                  pl.BlockSpec((1, d), lambda b, i: (0, 0)),
                  pl.BlockSpec((d, d), lambda b, i: (0, 0)),
                  pl.BlockSpec((e_rows, d), lambda b, i: (0, 0))],
        out_specs=[
            tok(d), tok(d),
            pl.BlockSpec((None, SUBLANES, tm), lambda b, i: (b * nt + i, 0, 0)),
            pl.BlockSpec((e_rows, LANES), lambda b, i: (b * nt + i, 0))],
        out_shape=[jax.ShapeDtypeStruct((bsz, seq, d), F32),
                   jax.ShapeDtypeStruct((bsz, seq, d), BF16),
                   jax.ShapeDtypeStruct((bsz * nt, SUBLANES, tm), F32),
                   jax.ShapeDtypeStruct((bsz * nt * e_rows, LANES), F32)],
        compiler_params=_cparams(("arbitrary", "arbitrary")),
        name="outproj_router",
    )(lru, att, x, mod_l, mod_l, mod_l, ln_g.reshape(1, d), w_out_b, rt)


def _ffn_kernel(*refs, w, final_norm):
    (lru_ref, att_ref, x_ref, g1_ref, sc_ref, sh_ref, ln_ref, wo_ref,
     wg_ref, wu_ref, wd_ref, g2_ref) = refs[:12]
    fg_ref = refs[12] if final_norm else None
    o_ref, x1_ref, h2_ref, acc_ref = refs[-4:]
    j = pl.program_id(2)

    @pl.when(j == 0)
    def _():
        y = (jnp.dot(lru_ref[...], wo_ref[:w, :], preferred_element_type=F32)
             + jnp.dot(att_ref[...], wo_ref[w:, :], preferred_element_type=F32))
        x1 = x_ref[...] + g1_ref[...] * y
        x1_ref[...] = x1
        h2 = _rmsnorm(x1, ln_ref[...]) * (1.0 + sc_ref[...]) + sh_ref[...]
        h2_ref[...] = h2.astype(BF16)
        acc_ref[...] = jnp.zeros_like(acc_ref)

    h = h2_ref[...]
    act = (jax.nn.silu(jnp.dot(h, wg_ref[...], preferred_element_type=F32))
           * jnp.dot(h, wu_ref[...], preferred_element_type=F32))
    acc_ref[...] += jnp.dot(act.astype(BF16), wd_ref[...], preferred_element_type=F32)

    @pl.when(j == pl.num_programs(2) - 1)
    def _():
        out = x1_ref[...] + g2_ref[...] * acc_ref[...]
        if final_norm:
            out = _rmsnorm(out, fg_ref[...])
        o_ref[...] = out


def _ffn(lru, att, x, mod_l, ln_g, w_out_b, wg, wu, wd, final_g=None, *, tm=512, tf=1536):
    bsz, seq, d = x.shape
    w = lru.shape[-1]
    ff = wg.shape[1]
    tf = min(tf, ff)
    final_norm = final_g is not None
    row = lambda k: pl.BlockSpec((None, None, 1, d), lambda b, i, j: (b, k, 0, 0))
    tok = lambda n: pl.BlockSpec((None, tm, n), lambda b, i, j: (b, i, 0))
    in_specs = [tok(w), tok(w), tok(d), row(2), row(4), row(3),
                pl.BlockSpec((1, d), lambda b, i, j: (0, 0)),
                pl.BlockSpec((d, d), lambda b, i, j: (0, 0)),
                pl.BlockSpec((d, tf), lambda b, i, j: (0, j)),
                pl.BlockSpec((d, tf), lambda b, i, j: (0, j)),
                pl.BlockSpec((tf, d), lambda b, i, j: (j, 0)),
                row(5)]
    args = [lru, att, x, mod_l, mod_l, mod_l, ln_g.reshape(1, d), w_out_b, wg, wu, wd, mod_l]
    if final_norm:
        in_specs.append(pl.BlockSpec((1, d), lambda b, i, j: (0, 0)))
        args.append(final_g.reshape(1, d))
    return pl.pallas_call(
        functools.partial(_ffn_kernel, w=w, final_norm=final_norm),
        grid=(bsz, seq // tm, ff // tf),
        in_specs=in_specs,
        out_specs=tok(d),
        out_shape=jax.ShapeDtypeStruct((bsz, seq, d), F32),
        scratch_shapes=[pltpu.VMEM((tm, d), F32), pltpu.VMEM((tm, d), BF16),
                        pltpu.VMEM((tm, d), F32)],
        compiler_params=_cparams(("arbitrary",) * 3),
        name="outproj_ffn",
    )(*args)


MOE_TB = 512
MOE_SEG = 16
MOE_TF = 512
MOE_LR = 2 * MOE_TB + N_EXPERTS * MOE_SEG


def _moe_plan(cnt, *, m_tok):
    i32 = jnp.int32
    nb = m_tok // MOE_TB
    n_e = N_EXPERTS
    rows = -(-(2 * m_tok + nb * n_e * MOE_SEG + n_e * MOE_TF) // MOE_TF) * MOE_TF
    ntf = rows // MOE_TF
    n = cnt.reshape(nb, -1, LANES)[:, :n_e, 0].astype(i32)
    seg_n = (n + MOE_SEG - 1) // MOE_SEG
    seg_src = jnp.cumsum(seg_n, axis=1) - seg_n
    used = jnp.sum(seg_n, axis=0)
    per_tile = MOE_TF // MOE_SEG
    gsz = (used + per_tile - 1) // per_tile * per_tile
    gend = jnp.cumsum(gsz)
    goff = gend - gsz
    seg_dst = goff[None, :] + jnp.cumsum(seg_n, axis=0) - seg_n
    total_tiles = gend[-1] // per_tile
    tile = jnp.arange(ntf, dtype=i32)
    f_valid = (tile < total_tiles).astype(i32)
    f_exp = jnp.minimum(
        jnp.sum(gend[None, :] <= (jnp.minimum(tile, total_tiles - 1) * per_tile)[:, None],
                axis=1).astype(i32), n_e - 1)
    tail_dst = jnp.concatenate([goff + used, gend[-1:]])
    tail_n = jnp.concatenate([gsz - used, rows // MOE_SEG - gend[-1:]])
    return dict(seg_n=seg_n.reshape(-1), seg_src=seg_src.reshape(-1),
                seg_dst=seg_dst.reshape(-1), blk_n=jnp.sum(seg_n, axis=1),
                tail_dst=tail_dst, tail_n=tail_n,
                f_exp=f_exp, f_valid=f_valid, rows=rows, ntf=ntf, nb=nb)


def _seg_rows(unit):
    return pl.ds(pl.multiple_of(unit * MOE_SEG, MOE_SEG), MOE_SEG)


def _wait_segments(sem, buf_ref, n):
    def body(_, carry):
        pltpu.make_async_copy(buf_ref.at[pl.ds(0, MOE_SEG)], buf_ref.at[pl.ds(0, MOE_SEG)],
                              sem).wait()
        return carry
    lax.fori_loop(0, n, body, 0)


def _moe_scatter_kernel(seg_n_ref, seg_src_ref, seg_dst_ref, tail_dst_ref, tail_n_ref,
                        h_ref, route_ref, xs_hbm, buf_ref, zero_ref, sem, tail_sem,
                        issued_ref):
    tb = pl.program_id(0)
    nb = pl.num_programs(0)
    slot = tb % 2
    buf = buf_ref.at[slot]

    @pl.when(tb >= 2)
    def _():
        _wait_segments(sem.at[slot], buf, issued_ref[slot])

    rows = lax.broadcasted_iota(jnp.int32, (MOE_LR, MOE_TB), 0)
    p1 = route_ref[2:3, :].astype(jnp.int32)
    p2 = route_ref[3:4, :].astype(jnp.int32)
    sel = jnp.where(p1 == rows, 1.0, jnp.where(p2 == rows, 1.0, 0.0))
    buf[...] = jnp.dot(sel.astype(BF16), h_ref[...],
                       preferred_element_type=F32).astype(buf_ref.dtype)

    issued = 0
    for e in range(N_EXPERTS):
        k = tb * N_EXPERTS + e
        n, src, dst = seg_n_ref[k], seg_src_ref[k], seg_dst_ref[k]

        def copy_seg(g, carry):
            pltpu.make_async_copy(buf.at[_seg_rows(src + g)], xs_hbm.at[_seg_rows(dst + g)],
                                  sem.at[slot]).start()
            return carry
        lax.fori_loop(0, n, copy_seg, 0)
        issued = issued + n
    issued_ref[slot] = issued

    @pl.when(tb == nb - 1)
    def _():
        zero_ref[...] = jnp.zeros_like(zero_ref)
        n_tail = 0
        for e in range(N_EXPERTS + 1):
            n, dst = tail_n_ref[e], tail_dst_ref[e]

            def zero_seg(g, carry):
                pltpu.make_async_copy(zero_ref, xs_hbm.at[_seg_rows(dst + g)], tail_sem).start()
                return carry
            lax.fori_loop(0, n, zero_seg, 0)
            n_tail = n_tail + n
        _wait_segments(tail_sem, zero_ref, n_tail)
        _wait_segments(sem.at[slot], buf, issued_ref[slot])

        @pl.when(nb >= 2)
        def _():
            _wait_segments(sem.at[1 - slot], buf, issued_ref[1 - slot])


def _moe_scatter(h2, plan, route_t):
    m_tok, d = h2.shape
    nb = plan['nb']
    grid_spec = pltpu.PrefetchScalarGridSpec(
        num_scalar_prefetch=5,
        grid=(nb,),
        in_specs=[pl.BlockSpec((MOE_TB, d), lambda t, *_: (t, 0)),
                  pl.BlockSpec((None, SUBLANES, MOE_TB), lambda t, *_: (t, 0, 0))],
        out_specs=pl.BlockSpec(memory_space=pl.ANY),
        scratch_shapes=[pltpu.VMEM((2, MOE_LR, d), BF16), pltpu.VMEM((MOE_SEG, d), BF16),
                        pltpu.SemaphoreType.DMA((2,)), pltpu.SemaphoreType.DMA(()),
                        pltpu.SMEM((2,), jnp.int32)],
    )
    return pl.pallas_call(
        _moe_scatter_kernel,
        grid_spec=grid_spec,
        out_shape=jax.ShapeDtypeStruct((plan['rows'], d), BF16),
        compiler_params=_cparams(("arbitrary",)),
        name="moe_scatter",
    )(plan['seg_n'], plan['seg_src'], plan['seg_dst'], plan['tail_dst'], plan['tail_n'],
      h2, route_t)


def _moe_ffn_kernel(exp_ref, valid_ref, xs_ref, wg_ref, wu_ref, wd_ref, o_ref, acc_ref):
    n = pl.program_id(0)
    j = pl.program_id(1)

    @pl.when(valid_ref[n] == 0)
    def _():
        o_ref[...] = jnp.zeros_like(o_ref)

    @pl.when(valid_ref[n] != 0)
    def _():
        @pl.when(j == 0)
        def _():
            acc_ref[...] = jnp.zeros_like(acc_ref)

        h = xs_ref[...]
        act = (jax.nn.silu(jnp.dot(h, wg_ref[...], preferred_element_type=F32))
               * jnp.dot(h, wu_ref[...], preferred_element_type=F32))
        acc_ref[...] += jnp.dot(act.astype(BF16), wd_ref[...], preferred_element_type=F32)

        @pl.when(j == pl.num_programs(1) - 1)
        def _():
            o_ref[...] = acc_ref[...].astype(o_ref.dtype)


def _moe_ffn(xs, wg, wu, wd, plan, *, tf=1536):
    rows, d = xs.shape
    ff = wg.shape[2]
    tf = min(tf, ff)
    nj = ff // tf

    def ff_tile(n, j, v):
        return j * v[n] + (nj - 1) * (1 - v[n])

    grid_spec = pltpu.PrefetchScalarGridSpec(
        num_scalar_prefetch=2,
        grid=(plan['ntf'], nj),
        in_specs=[pl.BlockSpec((MOE_TF, d), lambda n, j, e, v: (n, 0)),
                  pl.BlockSpec((None, d, tf), lambda n, j, e, v: (e[n], 0, ff_tile(n, j, v))),
                  pl.BlockSpec((None, d, tf), lambda n, j, e, v: (e[n], 0, ff_tile(n, j, v))),
                  pl.BlockSpec((None, tf, d), lambda n, j, e, v: (e[n], ff_tile(n, j, v), 0))],
        out_specs=pl.BlockSpec((MOE_TF, d), lambda n, j, e, v: (n, 0)),
        scratch_shapes=[pltpu.VMEM((MOE_TF, d), F32)],
    )
    return pl.pallas_call(
        _moe_ffn_kernel,
        grid_spec=grid_spec,
        out_shape=jax.ShapeDtypeStruct((rows, d), BF16),
        compiler_params=_cparams(("arbitrary", "arbitrary")),
        name="moe_ffn",
    )(plan['f_exp'], plan['f_valid'], xs, wg, wu, wd)


def _token_columns(route, field):
    pad = jnp.zeros_like(route)
    pick = jnp.where(lax.broadcasted_iota(jnp.int32, (2 * SUBLANES, LANES), 0) == field,
                     1.0, 0.0).astype(BF16)
    out = None
    rest = route
    for _ in range(3):
        piece = rest.astype(BF16)
        rest = rest - piece.astype(F32)
        term = lax.dot_general(jnp.concatenate([piece, pad.astype(BF16)], axis=0), pick,
                               (((0,), (0,)), ((), ())), preferred_element_type=F32)
        out = term if out is None else out + term
    return out


def _moe_combine_kernel(*refs, final_norm):
    if final_norm:
        (seg_n_ref, seg_src_ref, seg_dst_ref, blk_n_ref, ye_hbm, route_ref,
         x1_ref, g2_ref, fg_ref, o_ref, buf_ref, sem) = refs
    else:
        (seg_n_ref, seg_src_ref, seg_dst_ref, blk_n_ref, ye_hbm, route_ref,
         x1_ref, g2_ref, o_ref, buf_ref, sem) = refs
    tb = pl.program_id(0)
    nb = pl.num_programs(0)
    slot = tb % 2

    def fetch(block, into):
        for e in range(N_EXPERTS):
            k = block * N_EXPERTS + e
            n, src, dst = seg_n_ref[k], seg_src_ref[k], seg_dst_ref[k]

            def copy_seg(g, carry):
                pltpu.make_async_copy(ye_hbm.at[_seg_rows(dst + g)],
                                      buf_ref.at[into, _seg_rows(src + g)], sem.at[into]).start()
                return carry
            lax.fori_loop(0, n, copy_seg, 0)

    @pl.when(tb == 0)
    def _():
        fetch(tb, slot)

    @pl.when(tb + 1 < nb)
    def _():
        fetch(tb + 1, 1 - slot)

    buf = buf_ref.at[slot]
    _wait_segments(sem.at[slot], buf, blk_n_ref[tb])

    def clear(g, carry):
        buf[_seg_rows(g), :] = jnp.zeros((MOE_SEG, buf.shape[1]), buf.dtype)
        return carry
    lax.fori_loop(blk_n_ref[tb], MOE_LR // MOE_SEG, clear, 0)

    ye = buf[...]
    reps = MOE_LR // LANES
    cols = lax.broadcasted_iota(jnp.int32, (MOE_TB, MOE_LR), 1)

    route = route_ref[...]

    def unsort(field):
        pos = _token_columns(route, field).astype(jnp.int32)
        hit = jnp.tile(pos, (1, reps)) == cols
        return jnp.dot(jnp.where(hit, 1.0, 0.0).astype(BF16), ye, preferred_element_type=F32)

    lanes = x1_ref.shape[1] // LANES
    y = (jnp.tile(_token_columns(route, 4), (1, lanes)) * unsort(2)
         + jnp.tile(_token_columns(route, 5), (1, lanes)) * unsort(3))
    out = x1_ref[...] + g2_ref[...] * y
    if final_norm:
        out = _rmsnorm(out, fg_ref[...])
    o_ref[...] = out


def _moe_combine(ye, plan, route_t, x1, mod_l, final_g=None):
    bsz, seq, d = x1.shape
    m_tok = bsz * seq
    nb = plan['nb']
    blocks_per_seq = seq // MOE_TB
    final_norm = final_g is not None
    in_specs = [pl.BlockSpec(memory_space=pl.ANY),
                pl.BlockSpec((None, SUBLANES, MOE_TB), lambda t, *_: (t, 0, 0)),
                pl.BlockSpec((MOE_TB, d), lambda t, *_: (t, 0)),
                pl.BlockSpec((None, None, 1, d), lambda t, *_: (t // blocks_per_seq, 5, 0, 0))]
    args = [ye, route_t, x1.reshape(m_tok, d), mod_l]
    if final_norm:
        in_specs.append(pl.BlockSpec((1, d), lambda t, *_: (0, 0)))
        args.append(final_g.reshape(1, d))
    grid_spec = pltpu.PrefetchScalarGridSpec(
        num_scalar_prefetch=4,
        grid=(nb,),
        in_specs=in_specs,
        out_specs=pl.BlockSpec((MOE_TB, d), lambda t, *_: (t, 0)),
        scratch_shapes=[pltpu.VMEM((2, MOE_LR, d), BF16), pltpu.SemaphoreType.DMA((2,))],
    )
    out = pl.pallas_call(
        functools.partial(_moe_combine_kernel, final_norm=final_norm),
        grid_spec=grid_spec,
        out_shape=jax.ShapeDtypeStruct((m_tok, d), F32),
        compiler_params=_cparams(("arbitrary",)),
        name="moe_combine",
    )(plan['seg_n'], plan['seg_src'], plan['seg_dst'], plan['blk_n'], *args)
    return out.reshape(bsz, seq, d)


def _moe(h2, route_t, cnt, x1, mod_l, wg, wu, wd, final_g=None):
    bsz, seq, d = x1.shape
    m_tok = bsz * seq
    plan = _moe_plan(cnt, m_tok=m_tok)
    xs = _moe_scatter(h2.reshape(m_tok, d), plan, route_t)
    ye = _moe_ffn(xs, wg, wu, wd, plan)
    return _moe_combine(ye, plan, route_t, x1, mod_l, final_g)


def kernel(x, c, positions, ada_w, ada_b, ln1_g, ln2_g, w_in, conv_w, conv_b, gate_a_w, gate_a_b, gate_x_w, gate_x_b, lru_lambda, lam_q1, lam_k1, lam_q2, lam_k2, subln_g, w_out, ffn_w_gate, ffn_w_up, ffn_w_down, moe_router, moe_w_gate, moe_w_up, moe_w_down, final_g):
    depth = ada_w.shape[0]
    vd = subln_g.shape[-1]
    dh = vd // 2
    attn_w = DIFF_HEADS * vd

    mod = _modulation(c, ada_w, ada_b)
    cos_t, sin_t = _rope_tables(positions, dh)
    for l in range(depth):
        lambda_init = 0.8 - 0.6 * math.exp(-0.3 * l)
        mod_l = mod[l]
        lru, q, k, v = _inproj(x, mod_l, ln1_g[l], w_in[l].astype(BF16), cos_t, sin_t,
                               conv_w[l], conv_b[l], gate_a_w[l], gate_a_b[l], gate_x_w[l],
                               gate_x_b[l], lru_lambda[l], attn_w=attn_w, dh=dh)
        att = _attention(q, k, v, positions, lam_q1[l], lam_k1[l], lam_q2[l], lam_k2[l],
                         subln_g[l], lambda_init, dh=dh)
        fg = final_g if l == depth - 1 else None
        j = l // 2
        if l % 2 == 0:
            x = _ffn(lru, att, x, mod_l, ln2_g[l], w_out[l].astype(BF16),
                     ffn_w_gate[j].astype(BF16), ffn_w_up[j].astype(BF16),
                     ffn_w_down[j].astype(BF16), final_g=fg)
        else:
            x1, h2, route, cnt = _outproj_router(lru, att, x, mod_l, ln2_g[l],
                                                 w_out[l].astype(BF16), moe_router[j])
            x = _moe(h2, route, cnt, x1, mod_l, moe_w_gate[j].astype(BF16),
                     moe_w_up[j].astype(BF16), moe_w_down[j].astype(BF16), final_g=fg)
    return x
```

```python
import functools
import math

import jax
import jax.numpy as jnp
from jax import lax
from jax.experimental import pallas as pl
from jax.experimental.pallas import tpu as pltpu

F32 = jnp.float32
BF16 = jnp.bfloat16
HIGHEST = lax.Precision.HIGHEST

CHUNK = 64
LRU_BLOCKS = 8
CONV_W = 4
RG_C = 8.0
DIFF_HEADS = 4
ROPE_THETA = 10000.0
N_EXPERTS = 8
EPS = 1e-6
LANES = 128
SUBLANES = 8
VMEM_LIMIT = 56 * 1024 * 1024
MASK_VALUE = -0.5 * float(jnp.finfo(jnp.float32).max)
MASK_BIAS = 2.0 ** 100


def _cparams(sem):
    return pltpu.CompilerParams(dimension_semantics=sem, vmem_limit_bytes=VMEM_LIMIT)


def _rmsnorm(x, g):
    return x * lax.rsqrt(jnp.mean(x * x, axis=-1, keepdims=True) + EPS) * g


def _mod_kernel(c_ref, w_ref, b_ref, o_ref):
    c = c_ref[...]
    s = c * jax.nn.sigmoid(c)
    o_ref[...] = jnp.dot(s, w_ref[...], precision=HIGHEST,
                         preferred_element_type=F32) + b_ref[...]


def _modulation(c, ada_w, ada_b, tn=1024):
    depth, d, n = ada_w.shape
    bsz = c.shape[0]
    rows = -(-bsz // SUBLANES) * SUBLANES
    c_pad = jnp.zeros((rows, d), F32).at[:bsz].set(c)
    out = pl.pallas_call(
        _mod_kernel,
        grid=(depth, n // tn),
        in_specs=[
            pl.BlockSpec((rows, d), lambda l, j: (0, 0)),
            pl.BlockSpec((None, d, tn), lambda l, j: (l, 0, j)),
            pl.BlockSpec((None, 1, tn), lambda l, j: (l, 0, j)),
        ],
        out_specs=pl.BlockSpec((None, rows, tn), lambda l, j: (l, 0, j)),
        out_shape=jax.ShapeDtypeStruct((depth, rows, n), F32),
        compiler_params=_cparams(("arbitrary", "arbitrary")),
        name="adaln_mod",
    )(c_pad, ada_w, ada_b.reshape(depth, 1, n))
    return out[:, :bsz].reshape(depth, bsz, 6, 1, d)


def _rope_table_kernel(pos_ref, inv_ref, cos_ref, sin_ref, *, n_freq):
    ang = pos_ref[...].astype(F32) * inv_ref[...]
    groups = LANES // n_freq
    row = lax.broadcasted_iota(jnp.int32, (LANES, groups * LANES), 0)
    col = lax.broadcasted_iota(jnp.int32, (LANES, groups * LANES), 1)
    hit = row == (col // LANES) * n_freq + col % n_freq
    spread_cos = jnp.where(hit, 1.0, 0.0).astype(BF16)
    first_half = col % (2 * n_freq) < n_freq
    spread_sin = jnp.where(hit, jnp.where(first_half, -1.0, 1.0), 0.0).astype(BF16)

    def spread(t, e):
        out = None
        rest = t
        for _ in range(3):
            piece = rest.astype(BF16)
            rest = rest - piece.astype(F32)
            term = jnp.dot(piece, e, preferred_element_type=F32)
            out = term if out is None else out + term
        return out

    c = spread(jnp.cos(ang), spread_cos)
    s = spread(jnp.sin(ang), spread_sin)
    for g in range(groups):
        cos_ref[g] = c[:, g * LANES:(g + 1) * LANES]
        sin_ref[g] = s[:, g * LANES:(g + 1) * LANES]


def _rope_tables(positions, dh):
    n_freq = dh // 2
    groups = LANES // n_freq
    tok = positions.size
    rows = tok // groups
    inv = ROPE_THETA ** (-jnp.arange(0, dh, 2, dtype=F32) / dh)
    pos_x = jnp.repeat(positions.reshape(groups, rows).T, n_freq, axis=1)
    inv_x = jnp.tile(inv, groups).reshape(1, LANES)
    tr = min(rows, 1024)
    cos, sin = pl.pallas_call(
        functools.partial(_rope_table_kernel, n_freq=n_freq),
        grid=(rows // tr,),
        in_specs=[pl.BlockSpec((tr, LANES), lambda i: (i, 0)),
                  pl.BlockSpec((1, LANES), lambda i: (0, 0))],
        out_specs=[pl.BlockSpec((groups, tr, LANES), lambda i: (0, i, 0))] * 2,
        out_shape=[jax.ShapeDtypeStruct((groups, rows, LANES), F32)] * 2,
        compiler_params=_cparams(("arbitrary",)),
        name="rope_tables",
    )(pos_x, inv_x)
    return cos.reshape(tok, LANES), sin.reshape(tok, LANES)


def _inproj_kernel(x_ref, sc_ref, sh_ref, g_ref, w_ref, cos_ref, sin_ref,
                   cw_ref, cb_ref, wg_ref, bg_ref, lam_ref,
                   lru_ref, q_ref, k_ref, v_ref, xpad_ref, h_ref, *, lru_w, attn_w, dh):
    @pl.when(pl.program_id(1) == 0)
    def _():
        xpad_ref[0:SUBLANES, :] = jnp.zeros((SUBLANES, lru_w), F32)
        h_ref[...] = jnp.zeros_like(h_ref)

    h = _rmsnorm(x_ref[...], g_ref[...]) * (1.0 + sc_ref[...]) + sh_ref[...]
    hb = h.astype(BF16)
    lru2 = 2 * lru_w
    xy = jnp.dot(hb, w_ref[:, :lru2], preferred_element_type=F32)

    reps = attn_w // LANES
    cos = jnp.tile(cos_ref[...], (1, reps))
    sin = jnp.tile(sin_ref[...], (1, reps))
    lane = lax.broadcasted_iota(jnp.int32, cos.shape, 1)
    first_half = (lane % dh) < (dh // 2)

    def rope(t):
        fwd = pltpu.roll(t, attn_w - dh // 2, axis=1)
        bwd = pltpu.roll(t, dh // 2, axis=1)
        return t * cos + jnp.where(first_half, fwd, bwd) * sin

    q = jnp.dot(hb, w_ref[:, lru2:lru2 + attn_w], preferred_element_type=F32)
    q_ref[...] = (rope(q) * (dh ** -0.5 * math.log2(math.e))).astype(BF16)
    k = jnp.dot(hb, w_ref[:, lru2 + attn_w:lru2 + 2 * attn_w], preferred_element_type=F32)
    k_ref[...] = rope(k).astype(BF16)
    v = jnp.dot(hb, w_ref[:, lru2 + 2 * attn_w:], preferred_element_type=F32)
    v_ref[...] = v.astype(BF16)

    u, gates = _lru_conv_gates(xy[:, :lru_w], cw_ref, cb_ref, wg_ref, bg_ref, xpad_ref)
    a, bt = _lru_coeffs(u, gates, lam_ref)
    lru_ref[...] = _lru_scan(a, bt, xy[:, lru_w:], h_ref).astype(lru_ref.dtype)


def _inproj(x, mod_l, ln_g, w_in_b, cos_t, sin_t, conv_w, conv_b, wa, ba, wx, bx, lam, *,
            attn_w, dh, tm=512):
    bsz, seq, d = x.shape
    nt = seq // tm
    d_in = w_in_b.shape[1]
    lru_w = conv_w.shape[-1]
    wg = jnp.concatenate([_block_diag(wa), _block_diag(wx)], axis=1).astype(BF16)
    bg = jnp.concatenate([ba, bx]).reshape(1, 2 * lru_w)
    row = lambda k: pl.BlockSpec((None, None, 1, d), lambda b, i: (b, k, 0, 0))
    tok = lambda w: pl.BlockSpec((None, tm, w), lambda b, i: (b, i, 0))
    const = lambda shape: pl.BlockSpec(shape, lambda b, i: (0,) * len(shape))
    return pl.pallas_call(
        functools.partial(_inproj_kernel, lru_w=lru_w, attn_w=attn_w, dh=dh),
        grid=(bsz, nt),
        in_specs=[
            tok(d), row(1), row(0), const((1, d)), const((d, d_in)),
            pl.BlockSpec((tm, LANES), lambda b, i: (b * nt + i, 0)),
            pl.BlockSpec((tm, LANES), lambda b, i: (b * nt + i, 0)),
            const((CONV_W, lru_w)), const((1, lru_w)), const((lru_w, 2 * lru_w)),
            const((1, 2 * lru_w)), const((1, lru_w)),
        ],
        out_specs=[tok(lru_w), tok(attn_w), tok(attn_w), tok(attn_w)],
        out_shape=[jax.ShapeDtypeStruct((bsz, seq, lru_w), BF16),
                   jax.ShapeDtypeStruct((bsz, seq, attn_w), BF16),
                   jax.ShapeDtypeStruct((bsz, seq, attn_w), BF16),
                   jax.ShapeDtypeStruct((bsz, seq, attn_w), BF16)],
        scratch_shapes=[pltpu.VMEM((tm + SUBLANES, lru_w), F32), pltpu.VMEM((1, lru_w), F32)],
        compiler_params=_cparams(("arbitrary", "arbitrary")),
        name="inproj_lru",
    )(x, mod_l, mod_l, ln_g.reshape(1, d), w_in_b, cos_t, sin_t,
      conv_w, conv_b.reshape(1, lru_w), wg, bg, lam.reshape(1, lru_w))


def _gelu_tanh(x):
    return 0.5 * x * (1.0 + jnp.tanh(math.sqrt(2.0 / math.pi) * (x + 0.044715 * (x * x * x))))


def _lru_conv_gates(xr, cw_ref, cb_ref, wg_ref, bg_ref, xpad_ref):
    t, w = xr.shape
    xpad_ref[SUBLANES:SUBLANES + t, :] = xr
    u = cb_ref[...]
    for j in range(CONV_W):
        off = SUBLANES - (CONV_W - 1) + j
        u = u + cw_ref[j:j + 1, :] * xpad_ref[off:off + t, :]
    xpad_ref[0:SUBLANES, :] = xpad_ref[t:t + SUBLANES, :]
    gates = jnp.dot(u.astype(BF16), wg_ref[...], preferred_element_type=F32) + bg_ref[...]
    return u, gates


def _lru_coeffs(u, gates, lam_ref):
    w = u.shape[1]
    r = jax.nn.sigmoid(gates[:, :w])
    ig = jax.nn.sigmoid(gates[:, w:])
    neg_lam = -lam_ref[...]
    softplus = jnp.maximum(neg_lam, 0.0) + jnp.log1p(jnp.exp(-jnp.abs(neg_lam)))
    log_a = (-RG_C) * r * softplus
    a = jnp.exp(log_a)
    return a, jnp.sqrt(1.0 - a * a) * (ig * u)


def _lru_scan(a, bt, yr, h_ref):
    t, w = a.shape
    groups = t // SUBLANES
    a = a.reshape(groups, SUBLANES, w)
    bt = bt.reshape(groups, SUBLANES, w)
    sub = lax.broadcasted_iota(jnp.int32, a.shape, 1)
    shift = 1
    while shift < SUBLANES:
        keep = sub >= shift
        a_prev = jnp.where(keep, pltpu.roll(a, shift, axis=1), 1.0)
        b_prev = jnp.where(keep, pltpu.roll(bt, shift, axis=1), 0.0)
        bt = a * b_prev + bt
        a = a * a_prev
        shift *= 2
    carry = h_ref[...]
    rows = []
    for g in range(groups):
        hg = a[g] * carry + bt[g]
        rows.append(hg)
        carry = hg[SUBLANES - 1:SUBLANES, :]
    h_ref[...] = carry
    return jnp.concatenate(rows, axis=0) * _gelu_tanh(yr)


def _block_diag(wb):
    n, bw, _ = wb.shape
    eye = jnp.eye(n, dtype=wb.dtype)
    return jnp.einsum('nhk,nm->nhmk', wb, eye).reshape(n * bw, n * bw)


def _attn_kernel(qmin_ref, qmax_ref, kmin_ref, kmax_ref, qhmax_ref, khmin_ref,
                 q_ref, k_ref, v_ref, cq_ref, ck_ref, ckcol_ref, lq1_ref, lk1_ref, lq2_ref, lk2_ref,
                 g_ref, o_ref, m_ref, l_ref, acc_ref, *, tq, tk, nk, dh, lambda_init):
    b = pl.program_id(0)
    i = pl.program_id(2)
    m_ref[...] = jnp.full(m_ref.shape, -jnp.inf, F32)
    l_ref[...] = jnp.zeros(l_ref.shape, F32)
    acc_ref[...] = jnp.zeros(acc_ref.shape, F32)

    q = q_ref[...]
    lane = lax.broadcasted_iota(jnp.int32, q.shape, 1)
    qc = (jnp.where(lane < dh, q, jnp.zeros_like(q)), jnp.where(lane >= dh, q, jnp.zeros_like(q)))
    q_lo = qmin_ref[b, i]
    q_hi = qmax_ref[b, i]

    def process(j, mode, r0=0, rn=tq, k0=0, kn=tk):
        start = pl.multiple_of(j * tk, tk) + k0
        rows = slice(r0, r0 + rn)
        kb = k_ref[pl.ds(start, kn), :]
        vb = v_ref[pl.ds(start, kn), :]
        lhs = tuple(t[rows] for t in qc)
        if mode == "select":
            ck = ck_ref[:, pl.ds(start, kn)]
            visible = ck <= jnp.tile(cq_ref[rows, :], (1, kn // LANES))
        elif mode == "folded":
            c0 = kmin_ref[b, j]
            lane_k = lax.broadcasted_iota(jnp.int32, (kn, LANES), 1)
            k_chunk = jnp.where(ckcol_ref[pl.ds(start, kn), :] - c0 == lane_k, 1.0, 0.0)
            kb = jnp.concatenate([kb, k_chunk.astype(BF16)], axis=1)
            lane_q = lax.broadcasted_iota(jnp.int32, (rn, LANES), 1)
            q_bias = jnp.where(cq_ref[rows, :] - c0 < lane_q, -MASK_BIAS, 0.0).astype(BF16)
            lhs = tuple(jnp.concatenate([t, q_bias], axis=1) for t in lhs)
        for c in range(2):
            s = lax.dot_general(lhs[c], kb, (((1,), (1,)), ((), ())),
                                preferred_element_type=F32)
            if mode == "select":
                s = jnp.where(visible, s, MASK_VALUE)
            m_prev = m_ref[c, rows]
            m_new = jnp.maximum(m_prev, jnp.max(s, axis=-1, keepdims=True))
            alpha = jnp.exp2(m_prev - m_new)
            p = jnp.exp2(s - jnp.tile(m_new, (1, kn // LANES)))
            p_lanes = p[:, :LANES]
            for t in range(1, kn // LANES):
                p_lanes = p_lanes + p[:, t * LANES:(t + 1) * LANES]
            l_ref[c, rows] = alpha * l_ref[c, rows] + p_lanes
            acc_ref[c, rows] = alpha * acc_ref[c, rows] + jnp.dot(
                p.astype(BF16), vb, preferred_element_type=F32)
            m_ref[c, rows] = m_new

    def is_plain(j):
        return jnp.logical_and(kmin_ref[b, j] <= q_hi, kmax_ref[b, j] <= q_lo)

    def one_block(j, carry):
        k_lo = kmin_ref[b, j]
        k_hi = kmax_ref[b, j]
        needed = k_lo <= q_hi
        needs_mask = jnp.logical_and(needed, k_hi > q_lo)
        foldable = k_hi - k_lo < LANES
        fold = jnp.logical_and(needs_mask, foldable)
        corner_hidden = khmin_ref[b, 2 * j + 1] > qhmax_ref[b, 2 * i]

        @pl.when(jnp.logical_and(fold, corner_hidden))
        def _():
            process(j, "folded", 0, tq // 2, 0, tk // 2)
            process(j, "folded", tq // 2, tq // 2, 0, tk)

        @pl.when(jnp.logical_and(fold, jnp.logical_not(corner_hidden)))
        def _():
            process(j, "folded")

        @pl.when(jnp.logical_and(needs_mask, jnp.logical_not(foldable)))
        def _():
            process(j, "select")

        @pl.when(jnp.logical_and(needed, jnp.logical_not(needs_mask)))
        def _():
            process(j, "plain")

        return carry

    def two_blocks(jp, carry):
        j0 = 2 * jp
        both_plain = jnp.logical_and(is_plain(j0), is_plain(j0 + 1))

        @pl.when(both_plain)
        def _():
            process(j0, "plain", 0, tq, 0, 2 * tk)

        @pl.when(jnp.logical_not(both_plain))
        def _():
            lax.fori_loop(j0, j0 + 2, one_block, 0)

        return carry

    lax.fori_loop(0, nk // 2, two_blocks, 0)
    if nk % 2:
        one_block(nk - 1, 0)

    lam = (jnp.exp(jnp.sum(lq1_ref[...] * lk1_ref[...], keepdims=True))
           - jnp.exp(jnp.sum(lq2_ref[...] * lk2_ref[...], keepdims=True)) + lambda_init)
    l0 = jnp.sum(l_ref[0], axis=-1, keepdims=True)
    l1 = jnp.sum(l_ref[1], axis=-1, keepdims=True)
    o = acc_ref[0] / l0 - lam * (acc_ref[1] / l1)
    o_ref[...] = (_rmsnorm(o, g_ref[...]) * (1.0 - lambda_init)).astype(o_ref.dtype)


def _attention(q, k, v, positions, lq1, lk1, lq2, lk2, subln_g, lambda_init, *,
               dh, tq=1024, tk=1024):
    bsz, seq, aw = q.shape
    vd = 2 * dh
    heads = aw // vd
    nq, nk = seq // tq, seq // tk
    chunk = positions // CHUNK
    qmin = chunk.reshape(bsz, nq, tq).min(-1)
    qmax = chunk.reshape(bsz, nq, tq).max(-1)
    kmin = chunk.reshape(bsz, nk, tk).min(-1)
    kmax = chunk.reshape(bsz, nk, tk).max(-1)
    qhmax = chunk.reshape(bsz, 2 * nq, tq // 2).max(-1)
    khmin = chunk.reshape(bsz, 2 * nk, tk // 2).min(-1)
    cq = jnp.broadcast_to(chunk[:, :, None], (bsz, seq, LANES))
    ck = chunk.reshape(bsz, 1, seq)
    vec = lambda n: pl.BlockSpec((1, n), lambda b, h, i, *_: (0, 0))
    grid_spec = pltpu.PrefetchScalarGridSpec(
        num_scalar_prefetch=6,
        grid=(bsz, heads, nq),
        in_specs=[
            pl.BlockSpec((None, tq, vd), lambda b, h, i, *_: (b, i, h)),
            pl.BlockSpec((None, seq, vd), lambda b, h, i, *_: (b, 0, h)),
            pl.BlockSpec((None, seq, vd), lambda b, h, i, *_: (b, 0, h)),
            pl.BlockSpec((None, tq, LANES), lambda b, h, i, *_: (b, i, 0)),
            pl.BlockSpec((None, 1, seq), lambda b, h, i, *_: (b, 0, 0)),
            pl.BlockSpec((None, seq, LANES), lambda b, h, i, *_: (b, 0, 0)),
            vec(dh), vec(dh), vec(dh), vec(dh), vec(vd),
        ],
        out_specs=pl.BlockSpec((None, tq, vd), lambda b, h, i, *_: (b, i, h)),
        scratch_shapes=[pltpu.VMEM((2, tq, LANES), F32), pltpu.VMEM((2, tq, LANES), F32),
                        pltpu.VMEM((2, tq, vd), F32)],
    )
    return pl.pallas_call(
        functools.partial(_attn_kernel, tq=tq, tk=tk, nk=nk, dh=dh, lambda_init=lambda_init),
        grid_spec=grid_spec,
        out_shape=jax.ShapeDtypeStruct((bsz, seq, aw), BF16),
        compiler_params=_cparams(("arbitrary", "arbitrary", "arbitrary")),
        name="diff_attn",
    )(qmin, qmax, kmin, kmax, qhmax, khmin, q, k, v, cq, ck, cq,
      lq1.reshape(1, dh), lk1.reshape(1, dh), lq2.reshape(1, dh), lk2.reshape(1, dh),
      subln_g.reshape(1, vd))


def _outproj_router_kernel(lru_ref, att_ref, x_ref, g1_ref, sc_ref, sh_ref, ln_ref, wo_ref,
                           rt_ref, x1_ref, h2_ref, route_ref, cnt_ref, *, w):
    y = (jnp.dot(lru_ref[...], wo_ref[:w, :], preferred_element_type=F32)
         + jnp.dot(att_ref[...], wo_ref[w:, :], preferred_element_type=F32))
    x1 = x_ref[...] + g1_ref[...] * y
    x1_ref[...] = x1
    h2 = _rmsnorm(x1, ln_ref[...]) * (1.0 + sc_ref[...]) + sh_ref[...]
    h2_ref[...] = h2.astype(BF16)
    _route(h2, rt_ref, route_ref, cnt_ref)


def _route(h2, rt_ref, route_ref, cnt_ref):
    def split(v):
        hi = v.astype(BF16)
        return hi, (v - hi.astype(F32)).astype(BF16)

    h_hi, h_lo = split(h2)
    r_hi, r_lo = split(rt_ref[...])
    nt_dims = (((1,), (1,)), ((), ()))
    dot_nt = lambda a, b: lax.dot_general(a, b, nt_dims, preferred_element_type=F32)
    logits = dot_nt(r_hi, h_hi) + (dot_nt(r_hi, h_lo) + dot_nt(r_lo, h_hi))
    n_rows, tm = logits.shape
    row = lax.broadcasted_iota(jnp.int32, logits.shape, 0)
    lg = jnp.where(row < N_EXPERTS, logits, -jnp.inf)
    m1 = jnp.max(lg, axis=0, keepdims=True)
    i1 = jnp.min(jnp.where(lg == m1, row, n_rows), axis=0, keepdims=True)
    lg2 = jnp.where(row == i1, -jnp.inf, lg)
    m2 = jnp.max(lg2, axis=0, keepdims=True)
    i2 = jnp.min(jnp.where(lg2 == m2, row, n_rows), axis=0, keepdims=True)
    e2 = jnp.exp(m2 - m1)
    w1 = 1.0 / (1.0 + e2)
    w2 = e2 / (1.0 + e2)

    onehot = jnp.where(row == i1, 1.0, jnp.where(row == i2, 1.0, 0.0))
    tri = (lax.broadcasted_iota(jnp.int32, (tm, tm), 0)
           < lax.broadcasted_iota(jnp.int32, (tm, tm), 1))
    prefix = jnp.dot(onehot.astype(BF16), jnp.where(tri, 1.0, 0.0).astype(BF16),
                     preferred_element_type=F32)
    count = jnp.sum(onehot, axis=1, keepdims=True)
    seg_len = jnp.floor((count + (MOE_SEG - 1)) * (1.0 / MOE_SEG)) * MOE_SEG
    seg_off = jnp.zeros_like(seg_len)
    for e in range(N_EXPERTS - 1):
        seg_off = seg_off + jnp.where(row[:, :1] > e, seg_len[e:e + 1, :], 0.0)
    local = prefix + seg_off
    pos1 = jnp.sum(jnp.where(row == i1, local, 0.0), axis=0, keepdims=True)
    pos2 = jnp.sum(jnp.where(row == i2, local, 0.0), axis=0, keepdims=True)
    cnt_ref[...] = jnp.broadcast_to(count, cnt_ref.shape)
    fields = (i1.astype(F32), i2.astype(F32), pos1, pos2, w1, w2)
    field_row = lax.broadcasted_iota(jnp.int32, route_ref.shape, 0)
    route = jnp.zeros(route_ref.shape, F32)
    for n, val in enumerate(fields):
        route = jnp.where(field_row == n, val, route)
    route_ref[...] = route


def _outproj_router(lru, att, x, mod_l, ln_g, w_out_b, router):
    bsz, seq, d = x.shape
    w = lru.shape[-1]
    tm = MOE_TB
    nt = seq // tm
    e_rows = 2 * SUBLANES
    rt = jnp.zeros((e_rows, d), F32).at[:N_EXPERTS].set(router.T)
    row = lambda k: pl.BlockSpec((None, None, 1, d), lambda b, i: (b, k, 0, 0))
    tok = lambda n: pl.BlockSpec((None, tm, n), lambda b, i: (b, i, 0))
    return pl.pallas_call(
        functools.partial(_outproj_router_kernel, w=w),
        grid=(bsz, nt),
        in_specs=[tok(w), tok(w), tok(d), row(2), row(4), row(3),
                  pl.BlockSpec((1, d), lambda b, i: (0, 0)),
                  pl.BlockSpec((d, d), lambda b, i: (0, 0)),
                  pl.BlockSpec((e_rows, d), lambda b, i: (0, 0))],
        out_specs=[
            tok(d), tok(d),
            pl.BlockSpec((None, SUBLANES, tm), lambda b, i: (b * nt + i, 0, 0)),
            pl.BlockSpec((e_rows, LANES), lambda b, i: (b * nt + i, 0))],
        out_shape=[jax.ShapeDtypeStruct((bsz, seq, d), F32),
                   jax.ShapeDtypeStruct((bsz, seq, d), BF16),
                   jax.ShapeDtypeStruct((bsz * nt, SUBLANES, tm), F32),
                   jax.ShapeDtypeStruct((bsz * nt * e_rows, LANES), F32)],
        compiler_params=_cparams(("arbitrary", "arbitrary")),
        name="outproj_router",
    )(lru, att, x, mod_l, mod_l, mod_l, ln_g.reshape(1, d), w_out_b, rt)


def _ffn_kernel(*refs, w, final_norm):
    (lru_ref, att_ref, x_ref, g1_ref, sc_ref, sh_ref, ln_ref, wo_ref,
     wg_ref, wu_ref, wd_ref, g2_ref) = refs[:12]
    fg_ref = refs[12] if final_norm else None
    o_ref, x1_ref, h2_ref, acc_ref = refs[-4:]
    j = pl.program_id(2)

    @pl.when(j == 0)
    def _():
        y = (jnp.dot(lru_ref[...], wo_ref[:w, :], preferred_element_type=F32)
             + jnp.dot(att_ref[...], wo_ref[w:, :], preferred_element_type=F32))
        x1 = x_ref[...] + g1_ref[...] * y
        x1_ref[...] = x1
        h2 = _rmsnorm(x1, ln_ref[...]) * (1.0 + sc_ref[...]) + sh_ref[...]
        h2_ref[...] = h2.astype(BF16)
        acc_ref[...] = jnp.zeros_like(acc_ref)

    h = h2_ref[...]
    act = (jax.nn.silu(jnp.dot(h, wg_ref[...], preferred_element_type=F32))
           * jnp.dot(h, wu_ref[...], preferred_element_type=F32))
    acc_ref[...] += jnp.dot(act.astype(BF16), wd_ref[...], preferred_element_type=F32)

    @pl.when(j == pl.num_programs(2) - 1)
    def _():
        out = x1_ref[...] + g2_ref[...] * acc_ref[...]
        if final_norm:
            out = _rmsnorm(out, fg_ref[...])
        o_ref[...] = out


def _ffn(lru, att, x, mod_l, ln_g, w_out_b, wg, wu, wd, final_g=None, *, tm=512, tf=1536):
    bsz, seq, d = x.shape
    w = lru.shape[-1]
    ff = wg.shape[1]
    tf = min(tf, ff)
    final_norm = final_g is not None
    row = lambda k: pl.BlockSpec((None, None, 1, d), lambda b, i, j: (b, k, 0, 0))
    tok = lambda n: pl.BlockSpec((None, tm, n), lambda b, i, j: (b, i, 0))
    in_specs = [tok(w), tok(w), tok(d), row(2), row(4), row(3),
                pl.BlockSpec((1, d), lambda b, i, j: (0, 0)),
                pl.BlockSpec((d, d), lambda b, i, j: (0, 0)),
                pl.BlockSpec((d, tf), lambda b, i, j: (0, j)),
                pl.BlockSpec((d, tf), lambda b, i, j: (0, j)),
                pl.BlockSpec((tf, d), lambda b, i, j: (j, 0)),
                row(5)]
    args = [lru, att, x, mod_l, mod_l, mod_l, ln_g.reshape(1, d), w_out_b, wg, wu, wd, mod_l]
    if final_norm:
        in_specs.append(pl.BlockSpec((1, d), lambda b, i, j: (0, 0)))
        args.append(final_g.reshape(1, d))
    return pl.pallas_call(
        functools.partial(_ffn_kernel, w=w, final_norm=final_norm),
        grid=(bsz, seq // tm, ff // tf),
        in_specs=in_specs,
        out_specs=tok(d),
        out_shape=jax.ShapeDtypeStruct((bsz, seq, d), F32),
        scratch_shapes=[pltpu.VMEM((tm, d), F32), pltpu.VMEM((tm, d), BF16),
                        pltpu.VMEM((tm, d), F32)],
        compiler_params=_cparams(("arbitrary",) * 3),
        name="outproj_ffn",
    )(*args)


MOE_TB = 512
MOE_SEG = 16
MOE_TF = 512
MOE_LR = 2 * MOE_TB + N_EXPERTS * MOE_SEG


def _moe_plan(cnt, *, m_tok):
    i32 = jnp.int32
    nb = m_tok // MOE_TB
    n_e = N_EXPERTS
    rows = -(-(2 * m_tok + nb * n_e * MOE_SEG + n_e * MOE_TF) // MOE_TF) * MOE_TF
    ntf = rows // MOE_TF
    n = cnt.reshape(nb, -1, LANES)[:, :n_e, 0].astype(i32)
    seg_n = (n + MOE_SEG - 1) // MOE_SEG
    seg_src = jnp.cumsum(seg_n, axis=1) - seg_n
    used = jnp.sum(seg_n, axis=0)
    per_tile = MOE_TF // MOE_SEG
    gsz = (used + per_tile - 1) // per_tile * per_tile
    gend = jnp.cumsum(gsz)
    goff = gend - gsz
    seg_dst = goff[None, :] + jnp.cumsum(seg_n, axis=0) - seg_n
    total_tiles = gend[-1] // per_tile
    tile = jnp.arange(ntf, dtype=i32)
    f_valid = (tile < total_tiles).astype(i32)
    f_exp = jnp.minimum(
        jnp.sum(gend[None, :] <= (jnp.minimum(tile, total_tiles - 1) * per_tile)[:, None],
                axis=1).astype(i32), n_e - 1)
    tail_dst = jnp.concatenate([goff + used, gend[-1:]])
    tail_n = jnp.concatenate([gsz - used, rows // MOE_SEG - gend[-1:]])
    return dict(seg_n=seg_n.reshape(-1), seg_src=seg_src.reshape(-1),
                seg_dst=seg_dst.reshape(-1), blk_n=jnp.sum(seg_n, axis=1),
                tail_dst=tail_dst, tail_n=tail_n,
                f_exp=f_exp, f_valid=f_valid, rows=rows, ntf=ntf, nb=nb)


def _seg_rows(unit):
    return pl.ds(pl.multiple_of(unit * MOE_SEG, MOE_SEG), MOE_SEG)


def _wait_segments(sem, buf_ref, n):
    def body(_, carry):
        pltpu.make_async_copy(buf_ref.at[pl.ds(0, MOE_SEG)], buf_ref.at[pl.ds(0, MOE_SEG)],
                              sem).wait()
        return carry
    lax.fori_loop(0, n, body, 0)


def _moe_scatter_kernel(seg_n_ref, seg_src_ref, seg_dst_ref, tail_dst_ref, tail_n_ref,
                        h_ref, route_ref, xs_hbm, buf_ref, zero_ref, sem, tail_sem,
                        issued_ref):
    tb = pl.program_id(0)
    nb = pl.num_programs(0)
    slot = tb % 2
    buf = buf_ref.at[slot]

    @pl.when(tb >= 2)
    def _():
        _wait_segments(sem.at[slot], buf, issued_ref[slot])

    rows = lax.broadcasted_iota(jnp.int32, (MOE_LR, MOE_TB), 0)
    p1 = route_ref[2:3, :].astype(jnp.int32)
    p2 = route_ref[3:4, :].astype(jnp.int32)
    sel = jnp.where(p1 == rows, 1.0, jnp.where(p2 == rows, 1.0, 0.0))
    buf[...] = jnp.dot(sel.astype(BF16), h_ref[...],
                       preferred_element_type=F32).astype(buf_ref.dtype)

    issued = 0
    for e in range(N_EXPERTS):
        k = tb * N_EXPERTS + e
        n, src, dst = seg_n_ref[k], seg_src_ref[k], seg_dst_ref[k]

        def copy_seg(g, carry):
            pltpu.make_async_copy(buf.at[_seg_rows(src + g)], xs_hbm.at[_seg_rows(dst + g)],
                                  sem.at[slot]).start()
            return carry
        lax.fori_loop(0, n, copy_seg, 0)
        issued = issued + n
    issued_ref[slot] = issued

    @pl.when(tb == nb - 1)
    def _():
        zero_ref[...] = jnp.zeros_like(zero_ref)
        n_tail = 0
        for e in range(N_EXPERTS + 1):
            n, dst = tail_n_ref[e], tail_dst_ref[e]

            def zero_seg(g, carry):
                pltpu.make_async_copy(zero_ref, xs_hbm.at[_seg_rows(dst + g)], tail_sem).start()
                return carry
            lax.fori_loop(0, n, zero_seg, 0)
            n_tail = n_tail + n
        _wait_segments(tail_sem, zero_ref, n_tail)
        _wait_segments(sem.at[slot], buf, issued_ref[slot])

        @pl.when(nb >= 2)
        def _():
            _wait_segments(sem.at[1 - slot], buf, issued_ref[1 - slot])


def _moe_scatter(h2, plan, route_t):
    m_tok, d = h2.shape
    nb = plan['nb']
    grid_spec = pltpu.PrefetchScalarGridSpec(
        num_scalar_prefetch=5,
        grid=(nb,),
        in_specs=[pl.BlockSpec((MOE_TB, d), lambda t, *_: (t, 0)),
                  pl.BlockSpec((None, SUBLANES, MOE_TB), lambda t, *_: (t, 0, 0))],
        out_specs=pl.BlockSpec(memory_space=pl.ANY),
        scratch_shapes=[pltpu.VMEM((2, MOE_LR, d), BF16), pltpu.VMEM((MOE_SEG, d), BF16),
                        pltpu.SemaphoreType.DMA((2,)), pltpu.SemaphoreType.DMA(()),
                        pltpu.SMEM((2,), jnp.int32)],
    )
    return pl.pallas_call(
        _moe_scatter_kernel,
        grid_spec=grid_spec,
        out_shape=jax.ShapeDtypeStruct((plan['rows'], d), BF16),
        compiler_params=_cparams(("arbitrary",)),
        name="moe_scatter",
    )(plan['seg_n'], plan['seg_src'], plan['seg_dst'], plan['tail_dst'], plan['tail_n'],
      h2, route_t)


def _moe_ffn_kernel(exp_ref, valid_ref, xs_ref, wg_ref, wu_ref, wd_ref, o_ref, acc_ref):
    n = pl.program_id(0)
    j = pl.program_id(1)

    @pl.when(valid_ref[n] == 0)
    def _():
        o_ref[...] = jnp.zeros_like(o_ref)

    @pl.when(valid_ref[n] != 0)
    def _():
        @pl.when(j == 0)
        def _():
            acc_ref[...] = jnp.zeros_like(acc_ref)

        h = xs_ref[...]
        act = (jax.nn.silu(jnp.dot(h, wg_ref[...], preferred_element_type=F32))
               * jnp.dot(h, wu_ref[...], preferred_element_type=F32))
        acc_ref[...] += jnp.dot(act.astype(BF16), wd_ref[...], preferred_element_type=F32)

        @pl.when(j == pl.num_programs(1) - 1)
        def _():
            o_ref[...] = acc_ref[...].astype(o_ref.dtype)


def _moe_ffn(xs, wg, wu, wd, plan, *, tf=1536):
    rows, d = xs.shape
    ff = wg.shape[2]
    tf = min(tf, ff)
    nj = ff // tf

    def ff_tile(n, j, v):
        return j * v[n] + (nj - 1) * (1 - v[n])

    grid_spec = pltpu.PrefetchScalarGridSpec(
        num_scalar_prefetch=2,
        grid=(plan['ntf'], nj),
        in_specs=[pl.BlockSpec((MOE_TF, d), lambda n, j, e, v: (n, 0)),
                  pl.BlockSpec((None, d, tf), lambda n, j, e, v: (e[n], 0, ff_tile(n, j, v))),
                  pl.BlockSpec((None, d, tf), lambda n, j, e, v: (e[n], 0, ff_tile(n, j, v))),
                  pl.BlockSpec((None, tf, d), lambda n, j, e, v: (e[n], ff_tile(n, j, v), 0))],
        out_specs=pl.BlockSpec((MOE_TF, d), lambda n, j, e, v: (n, 0)),
        scratch_shapes=[pltpu.VMEM((MOE_TF, d), F32)],
    )
    return pl.pallas_call(
        _moe_ffn_kernel,
        grid_spec=grid_spec,
        out_shape=jax.ShapeDtypeStruct((rows, d), BF16),
        compiler_params=_cparams(("arbitrary", "arbitrary")),
        name="moe_ffn",
    )(plan['f_exp'], plan['f_valid'], xs, wg, wu, wd)


def _token_columns(route, field):
    pad = jnp.zeros_like(route)
    pick = jnp.where(lax.broadcasted_iota(jnp.int32, (2 * SUBLANES, LANES), 0) == field,
                     1.0, 0.0).astype(BF16)
    out = None
    rest = route
    for _ in range(3):
        piece = rest.astype(BF16)
        rest = rest - piece.astype(F32)
        term = lax.dot_general(jnp.concatenate([piece, pad.astype(BF16)], axis=0), pick,
                               (((0,), (0,)), ((), ())), preferred_element_type=F32)
        out = term if out is None else out + term
    return out


def _moe_combine_kernel(*refs, final_norm):
    if final_norm:
        (seg_n_ref, seg_src_ref, seg_dst_ref, blk_n_ref, ye_hbm, route_ref,
         x1_ref, g2_ref, fg_ref, o_ref, buf_ref, sem) = refs
    else:
        (seg_n_ref, seg_src_ref, seg_dst_ref, blk_n_ref, ye_hbm, route_ref,
         x1_ref, g2_ref, o_ref, buf_ref, sem) = refs
    tb = pl.program_id(0)
    nb = pl.num_programs(0)
    slot = tb % 2

    def fetch(block, into):
        for e in range(N_EXPERTS):
            k = block * N_EXPERTS + e
            n, src, dst = seg_n_ref[k], seg_src_ref[k], seg_dst_ref[k]

            def copy_seg(g, carry):
                pltpu.make_async_copy(ye_hbm.at[_seg_rows(dst + g)],
                                      buf_ref.at[into, _seg_rows(src + g)], sem.at[into]).start()
                return carry
            lax.fori_loop(0, n, copy_seg, 0)

    @pl.when(tb == 0)
    def _():
        fetch(tb, slot)

    @pl.when(tb + 1 < nb)
    def _():
        fetch(tb + 1, 1 - slot)

    buf = buf_ref.at[slot]
    _wait_segments(sem.at[slot], buf, blk_n_ref[tb])

    def clear(g, carry):
        buf[_seg_rows(g), :] = jnp.zeros((MOE_SEG, buf.shape[1]), buf.dtype)
        return carry
    lax.fori_loop(blk_n_ref[tb], MOE_LR // MOE_SEG, clear, 0)

    ye = buf[...]
    reps = MOE_LR // LANES
    cols = lax.broadcasted_iota(jnp.int32, (MOE_TB, MOE_LR), 1)

    route = route_ref[...]

    def unsort(field):
        pos = _token_columns(route, field).astype(jnp.int32)
        hit = jnp.tile(pos, (1, reps)) == cols
        return jnp.dot(jnp.where(hit, 1.0, 0.0).astype(BF16), ye, preferred_element_type=F32)

    lanes = x1_ref.shape[1] // LANES
    y = (jnp.tile(_token_columns(route, 4), (1, lanes)) * unsort(2)
         + jnp.tile(_token_columns(route, 5), (1, lanes)) * unsort(3))
    out = x1_ref[...] + g2_ref[...] * y
    if final_norm:
        out = _rmsnorm(out, fg_ref[...])
    o_ref[...] = out


def _moe_combine(ye, plan, route_t, x1, mod_l, final_g=None):
    bsz, seq, d = x1.shape
    m_tok = bsz * seq
    nb = plan['nb']
    blocks_per_seq = seq // MOE_TB
    final_norm = final_g is not None
    in_specs = [pl.BlockSpec(memory_space=pl.ANY),
                pl.BlockSpec((None, SUBLANES, MOE_TB), lambda t, *_: (t, 0, 0)),
                pl.BlockSpec((MOE_TB, d), lambda t, *_: (t, 0)),
                pl.BlockSpec((None, None, 1, d), lambda t, *_: (t // blocks_per_seq, 5, 0, 0))]
    args = [ye, route_t, x1.reshape(m_tok, d), mod_l]
    if final_norm:
        in_specs.append(pl.BlockSpec((1, d), lambda t, *_: (0, 0)))
        args.append(final_g.reshape(1, d))
    grid_spec = pltpu.PrefetchScalarGridSpec(
        num_scalar_prefetch=4,
        grid=(nb,),
        in_specs=in_specs,
        out_specs=pl.BlockSpec((MOE_TB, d), lambda t, *_: (t, 0)),
        scratch_shapes=[pltpu.VMEM((2, MOE_LR, d), BF16), pltpu.SemaphoreType.DMA((2,))],
    )
    out = pl.pallas_call(
        functools.partial(_moe_combine_kernel, final_norm=final_norm),
        grid_spec=grid_spec,
        out_shape=jax.ShapeDtypeStruct((m_tok, d), F32),
        compiler_params=_cparams(("arbitrary",)),
        name="moe_combine",
    )(plan['seg_n'], plan['seg_src'], plan['seg_dst'], plan['blk_n'], *args)
    return out.reshape(bsz, seq, d)


def _moe(h2, route_t, cnt, x1, mod_l, wg, wu, wd, final_g=None):
    bsz, seq, d = x1.shape
    m_tok = bsz * seq
    plan = _moe_plan(cnt, m_tok=m_tok)
    xs = _moe_scatter(h2.reshape(m_tok, d), plan, route_t)
    ye = _moe_ffn(xs, wg, wu, wd, plan)
    return _moe_combine(ye, plan, route_t, x1, mod_l, final_g)


def kernel(x, c, positions, ada_w, ada_b, ln1_g, ln2_g, w_in, conv_w, conv_b, gate_a_w, gate_a_b, gate_x_w, gate_x_b, lru_lambda, lam_q1, lam_k1, lam_q2, lam_k2, subln_g, w_out, ffn_w_gate, ffn_w_up, ffn_w_down, moe_router, moe_w_gate, moe_w_up, moe_w_down, final_g):
    depth = ada_w.shape[0]
    vd = subln_g.shape[-1]
    dh = vd // 2
    attn_w = DIFF_HEADS * vd

    mod = _modulation(c, ada_w, ada_b)
    cos_t, sin_t = _rope_tables(positions, dh)
    for l in range(depth):
        lambda_init = 0.8 - 0.6 * math.exp(-0.3 * l)
        mod_l = mod[l]
        lru, q, k, v = _inproj(x, mod_l, ln1_g[l], w_in[l].astype(BF16), cos_t, sin_t,
                               conv_w[l], conv_b[l], gate_a_w[l], gate_a_b[l], gate_x_w[l],
                               gate_x_b[l], lru_lambda[l], attn_w=attn_w, dh=dh)
        att = _attention(q, k, v, positions, lam_q1[l], lam_k1[l], lam_q2[l], lam_k2[l],
                         subln_g[l], lambda_init, dh=dh)
        fg = final_g if l == depth - 1 else None
        j = l // 2
        if l % 2 == 0:
            x = _ffn(lru, att, x, mod_l, ln2_g[l], w_out[l].astype(BF16),
                     ffn_w_gate[j].astype(BF16), ffn_w_up[j].astype(BF16),
                     ffn_w_down[j].astype(BF16), final_g=fg)
        else:
            x1, h2, route, cnt = _outproj_router(lru, att, x, mod_l, ln2_g[l],
                                                 w_out[l].astype(BF16), moe_router[j])
            x = _moe(h2, route, cnt, x1, mod_l, moe_w_gate[j].astype(BF16),
                     moe_w_up[j].astype(BF16), moe_w_down[j].astype(BF16), final_g=fg)
    return x
```

```python
import functools
import math

import jax
import jax.numpy as jnp
from jax import lax
from jax.experimental import pallas as pl
from jax.experimental.pallas import tpu as pltpu

F32 = jnp.float32
BF16 = jnp.bfloat16
HIGHEST = lax.Precision.HIGHEST

CHUNK = 64
CONV_W = 4
RG_C = 8.0
DIFF_HEADS = 4
ROPE_THETA = 10000.0
N_EXPERTS = 8
EPS = 1e-6
LANES = 128
SUBLANES = 8
VMEM_LIMIT = 56 * 1024 * 1024
MASK_VALUE = -0.5 * float(jnp.finfo(jnp.float32).max)
MASK_BIAS = 2.0 ** 100


def _cparams(sem):
    return pltpu.CompilerParams(dimension_semantics=sem, vmem_limit_bytes=VMEM_LIMIT)


def _rmsnorm(x, g):
    return x * lax.rsqrt(jnp.mean(x * x, axis=-1, keepdims=True) + EPS) * g


def _mod_kernel(c_ref, w_ref, b_ref, o_ref):
    c = c_ref[...]
    s = c * jax.nn.sigmoid(c)
    o_ref[...] = jnp.dot(s, w_ref[...], precision=HIGHEST,
                         preferred_element_type=F32) + b_ref[...]


def _modulation(c, ada_w, ada_b, tn=1024):
    depth, d, n = ada_w.shape
    bsz = c.shape[0]
    rows = -(-bsz // SUBLANES) * SUBLANES
    c_pad = jnp.zeros((rows, d), F32).at[:bsz].set(c)
    out = pl.pallas_call(
        _mod_kernel,
        grid=(depth, n // tn),
        in_specs=[
            pl.BlockSpec((rows, d), lambda l, j: (0, 0)),
            pl.BlockSpec((None, d, tn), lambda l, j: (l, 0, j)),
            pl.BlockSpec((None, 1, tn), lambda l, j: (l, 0, j)),
        ],
        out_specs=pl.BlockSpec((None, rows, tn), lambda l, j: (l, 0, j)),
        out_shape=jax.ShapeDtypeStruct((depth, rows, n), F32),
        compiler_params=_cparams(("arbitrary", "arbitrary")),
        name="adaln_mod",
    )(c_pad, ada_w, ada_b.reshape(depth, 1, n))
    return out[:, :bsz].reshape(depth, bsz, 6, 1, d)


def _rope_table_kernel(pos_ref, inv_ref, cos_ref, sin_ref, *, n_freq):
    ang = pos_ref[...].astype(F32) * inv_ref[...]
    groups = LANES // n_freq
    row = lax.broadcasted_iota(jnp.int32, (LANES, groups * LANES), 0)
    col = lax.broadcasted_iota(jnp.int32, (LANES, groups * LANES), 1)
    hit = row == (col // LANES) * n_freq + col % n_freq
    spread_cos = jnp.where(hit, 1.0, 0.0).astype(BF16)
    first_half = col % (2 * n_freq) < n_freq
    spread_sin = jnp.where(hit, jnp.where(first_half, -1.0, 1.0), 0.0).astype(BF16)

    def spread(t, e):
        out = None
        rest = t
        for _ in range(3):
            piece = rest.astype(BF16)
            rest = rest - piece.astype(F32)
            term = jnp.dot(piece, e, preferred_element_type=F32)
            out = term if out is None else out + term
        return out

    c = spread(jnp.cos(ang), spread_cos)
    s = spread(jnp.sin(ang), spread_sin)
    for g in range(groups):
        cos_ref[g] = c[:, g * LANES:(g + 1) * LANES]
        sin_ref[g] = s[:, g * LANES:(g + 1) * LANES]


def _rope_tables(positions, dh):
    n_freq = dh // 2
    groups = LANES // n_freq
    tok = positions.size
    rows = tok // groups
    inv = ROPE_THETA ** (-jnp.arange(0, dh, 2, dtype=F32) / dh)
    pos_x = jnp.repeat(positions.reshape(groups, rows).T, n_freq, axis=1)
    inv_x = jnp.tile(inv, groups).reshape(1, LANES)
    tr = min(rows, 1024)
    cos, sin = pl.pallas_call(
        functools.partial(_rope_table_kernel, n_freq=n_freq),
        grid=(rows // tr,),
        in_specs=[pl.BlockSpec((tr, LANES), lambda i: (i, 0)),
                  pl.BlockSpec((1, LANES), lambda i: (0, 0))],
        out_specs=[pl.BlockSpec((groups, tr, LANES), lambda i: (0, i, 0))] * 2,
        out_shape=[jax.ShapeDtypeStruct((groups, rows, LANES), F32)] * 2,
        compiler_params=_cparams(("arbitrary",)),
        name="rope_tables",
    )(pos_x, inv_x)
    return cos.reshape(tok, LANES), sin.reshape(tok, LANES)


def _inproj_kernel(x_ref, sc_ref, sh_ref, g_ref, w_ref, cos_ref, sin_ref,
                   cw_ref, cb_ref, wg_ref, bg_ref, lam_ref,
                   lru_ref, q_ref, k_ref, v_ref, xpad_ref, h_ref, *, lru_w, attn_w, dh):
    @pl.when(pl.program_id(1) == 0)
    def _():
        xpad_ref[0:SUBLANES, :] = jnp.zeros((SUBLANES, lru_w), F32)
        h_ref[...] = jnp.zeros_like(h_ref)

    h = _rmsnorm(x_ref[...], g_ref[...]) * (1.0 + sc_ref[...]) + sh_ref[...]
    hb = h.astype(BF16)
    lru2 = 2 * lru_w
    xy = jnp.dot(hb, w_ref[:, :lru2], preferred_element_type=F32)

    reps = attn_w // LANES
    cos = jnp.tile(cos_ref[...], (1, reps))
    sin = jnp.tile(sin_ref[...], (1, reps))
    lane = lax.broadcasted_iota(jnp.int32, cos.shape, 1)
    first_half = (lane % dh) < (dh // 2)

    def rope(t):
        fwd = pltpu.roll(t, attn_w - dh // 2, axis=1)
        bwd = pltpu.roll(t, dh // 2, axis=1)
        return t * cos + jnp.where(first_half, fwd, bwd) * sin

    q = jnp.dot(hb, w_ref[:, lru2:lru2 + attn_w], preferred_element_type=F32)
    q_ref[...] = (rope(q) * (dh ** -0.5 * math.log2(math.e))).astype(BF16)
    k = jnp.dot(hb, w_ref[:, lru2 + attn_w:lru2 + 2 * attn_w], preferred_element_type=F32)
    k_ref[...] = rope(k).astype(BF16)
    v = jnp.dot(hb, w_ref[:, lru2 + 2 * attn_w:], preferred_element_type=F32)
    v_ref[...] = v.astype(BF16)

    u, gates = _lru_conv_gates(xy[:, :lru_w], cw_ref, cb_ref, wg_ref, bg_ref, xpad_ref)
    a, bt = _lru_coeffs(u, gates, lam_ref)
    lru_ref[...] = _lru_scan(a, bt, xy[:, lru_w:], h_ref).astype(lru_ref.dtype)


def _inproj(x, mod_l, ln_g, w_in_b, cos_t, sin_t, conv_w, conv_b, wa, ba, wx, bx, lam, *,
            attn_w, dh, tm=512):
    bsz, seq, d = x.shape
    nt = seq // tm
    d_in = w_in_b.shape[1]
    lru_w = conv_w.shape[-1]
    wg = jnp.concatenate([_block_diag(wa), _block_diag(wx)], axis=1).astype(BF16)
    bg = jnp.concatenate([ba, bx]).reshape(1, 2 * lru_w)
    row = lambda k: pl.BlockSpec((None, None, 1, d), lambda b, i: (b, k, 0, 0))
    tok = lambda w: pl.BlockSpec((None, tm, w), lambda b, i: (b, i, 0))
    const = lambda shape: pl.BlockSpec(shape, lambda b, i: (0,) * len(shape))
    return pl.pallas_call(
        functools.partial(_inproj_kernel, lru_w=lru_w, attn_w=attn_w, dh=dh),
        grid=(bsz, nt),
        in_specs=[
            tok(d), row(1), row(0), const((1, d)), const((d, d_in)),
            pl.BlockSpec((tm, LANES), lambda b, i: (b * nt + i, 0)),
            pl.BlockSpec((tm, LANES), lambda b, i: (b * nt + i, 0)),
            const((CONV_W, lru_w)), const((1, lru_w)), const((lru_w, 2 * lru_w)),
            const((1, 2 * lru_w)), const((1, lru_w)),
        ],
        out_specs=[tok(lru_w), tok(attn_w), tok(attn_w), tok(attn_w)],
        out_shape=[jax.ShapeDtypeStruct((bsz, seq, lru_w), BF16),
                   jax.ShapeDtypeStruct((bsz, seq, attn_w), BF16),
                   jax.ShapeDtypeStruct((bsz, seq, attn_w), BF16),
                   jax.ShapeDtypeStruct((bsz, seq, attn_w), BF16)],
        scratch_shapes=[pltpu.VMEM((tm + SUBLANES, lru_w), F32), pltpu.VMEM((1, lru_w), F32)],
        compiler_params=_cparams(("arbitrary", "arbitrary")),
        name="inproj_lru",
    )(x, mod_l, mod_l, ln_g.reshape(1, d), w_in_b, cos_t, sin_t,
      conv_w, conv_b.reshape(1, lru_w), wg, bg, lam.reshape(1, lru_w))


def _gelu_tanh(x):
    return 0.5 * x * (1.0 + jnp.tanh(math.sqrt(2.0 / math.pi) * (x + 0.044715 * (x * x * x))))


def _lru_conv_gates(xr, cw_ref, cb_ref, wg_ref, bg_ref, xpad_ref):
    t, w = xr.shape
    xpad_ref[SUBLANES:SUBLANES + t, :] = xr
    u = cb_ref[...]
    for j in range(CONV_W):
        off = SUBLANES - (CONV_W - 1) + j
        u = u + cw_ref[j:j + 1, :] * xpad_ref[off:off + t, :]
    xpad_ref[0:SUBLANES, :] = xpad_ref[t:t + SUBLANES, :]
    gates = jnp.dot(u.astype(BF16), wg_ref[...], preferred_element_type=F32) + bg_ref[...]
    return u, gates


def _lru_coeffs(u, gates, lam_ref):
    w = u.shape[1]
    r = jax.nn.sigmoid(gates[:, :w])
    ig = jax.nn.sigmoid(gates[:, w:])
    neg_lam = -lam_ref[...]
    softplus = jnp.maximum(neg_lam, 0.0) + jnp.log1p(jnp.exp(-jnp.abs(neg_lam)))
    log_a = (-RG_C) * r * softplus
    a = jnp.exp(log_a)
    return a, jnp.sqrt(1.0 - a * a) * (ig * u)


def _lru_scan(a, bt, yr, h_ref):
    t, w = a.shape
    groups = t // SUBLANES
    a = a.reshape(groups, SUBLANES, w)
    bt = bt.reshape(groups, SUBLANES, w)
    sub = lax.broadcasted_iota(jnp.int32, a.shape, 1)
    shift = 1
    while shift < SUBLANES:
        keep = sub >= shift
        a_prev = jnp.where(keep, pltpu.roll(a, shift, axis=1), 1.0)
        b_prev = jnp.where(keep, pltpu.roll(bt, shift, axis=1), 0.0)
        bt = a * b_prev + bt
        a = a * a_prev
        shift *= 2
    carry = h_ref[...]
    rows = []
    for g in range(groups):
        hg = a[g] * carry + bt[g]
        rows.append(hg)
        carry = hg[SUBLANES - 1:SUBLANES, :]
    h_ref[...] = carry
    return jnp.concatenate(rows, axis=0) * _gelu_tanh(yr)


def _block_diag(wb):
    n, bw, _ = wb.shape
    eye = jnp.eye(n, dtype=wb.dtype)
    return jnp.einsum('nhk,nm->nhmk', wb, eye).reshape(n * bw, n * bw)


def _attn_kernel(qmin_ref, qmax_ref, kmin_ref, kmax_ref, qhmax_ref, khmin_ref,
                 q_ref, k_ref, v_ref, cq_ref, ck_ref, ckcol_ref, lq1_ref, lk1_ref, lq2_ref, lk2_ref,
                 g_ref, o_ref, m_ref, l_ref, acc_ref, *, tq, tk, nk, dh, lambda_init):
    b = pl.program_id(0)
    i = pl.program_id(2)
    m_ref[...] = jnp.full(m_ref.shape, -jnp.inf, F32)
    l_ref[...] = jnp.zeros(l_ref.shape, F32)
    acc_ref[...] = jnp.zeros(acc_ref.shape, F32)

    q = q_ref[...]
    lane = lax.broadcasted_iota(jnp.int32, q.shape, 1)
    qc = (jnp.where(lane < dh, q, jnp.zeros_like(q)), jnp.where(lane >= dh, q, jnp.zeros_like(q)))
    q_lo = qmin_ref[b, i]
    q_hi = qmax_ref[b, i]

    def process(j, mode, r0=0, rn=tq, k0=0, kn=tk):
        start = pl.multiple_of(j * tk, tk) + k0
        rows = slice(r0, r0 + rn)
        kb = k_ref[pl.ds(start, kn), :]
        vb = v_ref[pl.ds(start, kn), :]
        lhs = tuple(t[rows] for t in qc)
        if mode == "select":
            ck = ck_ref[:, pl.ds(start, kn)]
            visible = ck <= jnp.tile(cq_ref[rows, :], (1, kn // LANES))
        elif mode == "folded":
            c0 = kmin_ref[b, j]
            lane_k = lax.broadcasted_iota(jnp.int32, (kn, LANES), 1)
            k_chunk = jnp.where(ckcol_ref[pl.ds(start, kn), :] - c0 == lane_k, 1.0, 0.0)
            kb = jnp.concatenate([kb, k_chunk.astype(BF16)], axis=1)
            lane_q = lax.broadcasted_iota(jnp.int32, (rn, LANES), 1)
            q_bias = jnp.where(cq_ref[rows, :] - c0 < lane_q, -MASK_BIAS, 0.0).astype(BF16)
            lhs = tuple(jnp.concatenate([t, q_bias], axis=1) for t in lhs)
        for c in range(2):
            s = lax.dot_general(lhs[c], kb, (((1,), (1,)), ((), ())),
                                preferred_element_type=F32)
            if mode == "select":
                s = jnp.where(visible, s, MASK_VALUE)
            m_prev = m_ref[c, rows]
            m_new = jnp.maximum(m_prev, jnp.max(s, axis=-1, keepdims=True))
            alpha = jnp.exp2(m_prev - m_new)
            p = jnp.exp2(s - jnp.tile(m_new, (1, kn // LANES)))
            p_lanes = p[:, :LANES]
            for t in range(1, kn // LANES):
                p_lanes = p_lanes + p[:, t * LANES:(t + 1) * LANES]
            l_ref[c, rows] = alpha * l_ref[c, rows] + p_lanes
            acc_ref[c, rows] = alpha * acc_ref[c, rows] + jnp.dot(
                p.astype(BF16), vb, preferred_element_type=F32)
            m_ref[c, rows] = m_new

    def is_plain(j):
        return jnp.logical_and(kmin_ref[b, j] <= q_hi, kmax_ref[b, j] <= q_lo)

    def one_block(j, carry):
        k_lo = kmin_ref[b, j]
        k_hi = kmax_ref[b, j]
        needed = k_lo <= q_hi
        needs_mask = jnp.logical_and(needed, k_hi > q_lo)
        foldable = k_hi - k_lo < LANES
        fold = jnp.logical_and(needs_mask, foldable)
        corner_hidden = khmin_ref[b, 2 * j + 1] > qhmax_ref[b, 2 * i]

        @pl.when(jnp.logical_and(fold, corner_hidden))
        def _():
            process(j, "folded", 0, tq // 2, 0, tk // 2)
            process(j, "folded", tq // 2, tq // 2, 0, tk)

        @pl.when(jnp.logical_and(fold, jnp.logical_not(corner_hidden)))
        def _():
            process(j, "folded")

        @pl.when(jnp.logical_and(needs_mask, jnp.logical_not(foldable)))
        def _():
            process(j, "select")

        @pl.when(jnp.logical_and(needed, jnp.logical_not(needs_mask)))
        def _():
            process(j, "plain")

        return carry

    def two_blocks(jp, carry):
        j0 = 2 * jp
        both_plain = jnp.logical_and(is_plain(j0), is_plain(j0 + 1))

        @pl.when(both_plain)
        def _():
            process(j0, "plain", 0, tq, 0, 2 * tk)

        @pl.when(jnp.logical_not(both_plain))
        def _():
            lax.fori_loop(j0, j0 + 2, one_block, 0)

        return carry

    lax.fori_loop(0, nk // 2, two_blocks, 0)
    if nk % 2:
        one_block(nk - 1, 0)

    lam = (jnp.exp(jnp.sum(lq1_ref[...] * lk1_ref[...], keepdims=True))
           - jnp.exp(jnp.sum(lq2_ref[...] * lk2_ref[...], keepdims=True)) + lambda_init)
    l0 = jnp.sum(l_ref[0], axis=-1, keepdims=True)
    l1 = jnp.sum(l_ref[1], axis=-1, keepdims=True)
    o = acc_ref[0] / l0 - lam * (acc_ref[1] / l1)
    o_ref[...] = (_rmsnorm(o, g_ref[...]) * (1.0 - lambda_init)).astype(o_ref.dtype)


def _attention(q, k, v, positions, lq1, lk1, lq2, lk2, subln_g, lambda_init, *,
               dh, tq=1024, tk=1024):
    bsz, seq, aw = q.shape
    vd = 2 * dh
    heads = aw // vd
    nq, nk = seq // tq, seq // tk
    chunk = positions // CHUNK
    qmin = chunk.reshape(bsz, nq, tq).min(-1)
    qmax = chunk.reshape(bsz, nq, tq).max(-1)
    kmin = chunk.reshape(bsz, nk, tk).min(-1)
    kmax = chunk.reshape(bsz, nk, tk).max(-1)
    qhmax = chunk.reshape(bsz, 2 * nq, tq // 2).max(-1)
    khmin = chunk.reshape(bsz, 2 * nk, tk // 2).min(-1)
    cq = jnp.broadcast_to(chunk[:, :, None], (bsz, seq, LANES))
    ck = chunk.reshape(bsz, 1, seq)
    vec = lambda n: pl.BlockSpec((1, n), lambda b, h, i, *_: (0, 0))
    grid_spec = pltpu.PrefetchScalarGridSpec(
        num_scalar_prefetch=6,
        grid=(bsz, heads, nq),
        in_specs=[
            pl.BlockSpec((None, tq, vd), lambda b, h, i, *_: (b, i, h)),
            pl.BlockSpec((None, seq, vd), lambda b, h, i, *_: (b, 0, h)),
            pl.BlockSpec((None, seq, vd), lambda b, h, i, *_: (b, 0, h)),
            pl.BlockSpec((None, tq, LANES), lambda b, h, i, *_: (b, i, 0)),
            pl.BlockSpec((None, 1, seq), lambda b, h, i, *_: (b, 0, 0)),
            pl.BlockSpec((None, seq, LANES), lambda b, h, i, *_: (b, 0, 0)),
            vec(dh), vec(dh), vec(dh), vec(dh), vec(vd),
        ],
        out_specs=pl.BlockSpec((None, tq, vd), lambda b, h, i, *_: (b, i, h)),
        scratch_shapes=[pltpu.VMEM((2, tq, LANES), F32), pltpu.VMEM((2, tq, LANES), F32),
                        pltpu.VMEM((2, tq, vd), F32)],
    )
    return pl.pallas_call(
        functools.partial(_attn_kernel, tq=tq, tk=tk, nk=nk, dh=dh, lambda_init=lambda_init),
        grid_spec=grid_spec,
        out_shape=jax.ShapeDtypeStruct((bsz, seq, aw), BF16),
        compiler_params=_cparams(("arbitrary", "arbitrary", "arbitrary")),
        name="diff_attn",
    )(qmin, qmax, kmin, kmax, qhmax, khmin, q, k, v, cq, ck, cq,
      lq1.reshape(1, dh), lk1.reshape(1, dh), lq2.reshape(1, dh), lk2.reshape(1, dh),
      subln_g.reshape(1, vd))


def _outproj_router_kernel(lru_ref, att_ref, x_ref, g1_ref, sc_ref, sh_ref, ln_ref, wo_ref,
                           rt_ref, x1_ref, h2_ref, route_ref, cnt_ref, *, w):
    y = (jnp.dot(lru_ref[...], wo_ref[:w, :], preferred_element_type=F32)
         + jnp.dot(att_ref[...], wo_ref[w:, :], preferred_element_type=F32))
    x1 = x_ref[...] + g1_ref[...] * y
    x1_ref[...] = x1
    h2 = _rmsnorm(x1, ln_ref[...]) * (1.0 + sc_ref[...]) + sh_ref[...]
    h2_ref[...] = h2.astype(BF16)
    _route(h2, rt_ref, route_ref, cnt_ref)


def _route(h2, rt_ref, route_ref, cnt_ref):
    def split(v):
        hi = v.astype(BF16)
        return hi, (v - hi.astype(F32)).astype(BF16)

    h_hi, h_lo = split(h2)
    r_hi, r_lo = split(rt_ref[...])
    nt_dims = (((1,), (1,)), ((), ()))
    dot_nt = lambda a, b: lax.dot_general(a, b, nt_dims, preferred_element_type=F32)
    logits = dot_nt(r_hi, h_hi) + (dot_nt(r_hi, h_lo) + dot_nt(r_lo, h_hi))
    n_rows, tm = logits.shape
    row = lax.broadcasted_iota(jnp.int32, logits.shape, 0)
    lg = jnp.where(row < N_EXPERTS, logits, -jnp.inf)
    m1 = jnp.max(lg, axis=0, keepdims=True)
    i1 = jnp.min(jnp.where(lg == m1, row, n_rows), axis=0, keepdims=True)
    lg2 = jnp.where(row == i1, -jnp.inf, lg)
    m2 = jnp.max(lg2, axis=0, keepdims=True)
    i2 = jnp.min(jnp.where(lg2 == m2, row, n_rows), axis=0, keepdims=True)
    e2 = jnp.exp(m2 - m1)
    w1 = 1.0 / (1.0 + e2)
    w2 = e2 / (1.0 + e2)

    onehot = jnp.where(row == i1, 1.0, jnp.where(row == i2, 1.0, 0.0))
    tri = (lax.broadcasted_iota(jnp.int32, (tm, tm), 0)
           < lax.broadcasted_iota(jnp.int32, (tm, tm), 1))
    prefix = jnp.dot(onehot.astype(BF16), jnp.where(tri, 1.0, 0.0).astype(BF16),
                     preferred_element_type=F32)
    count = jnp.sum(onehot, axis=1, keepdims=True)
    seg_len = jnp.floor((count + (MOE_SEG - 1)) * (1.0 / MOE_SEG)) * MOE_SEG
    seg_off = jnp.zeros_like(seg_len)
    for e in range(N_EXPERTS - 1):
        seg_off = seg_off + jnp.where(row[:, :1] > e, seg_len[e:e + 1, :], 0.0)
    local = prefix + seg_off
    pos1 = jnp.sum(jnp.where(row == i1, local, 0.0), axis=0, keepdims=True)
    pos2 = jnp.sum(jnp.where(row == i2, local, 0.0), axis=0, keepdims=True)
    cnt_ref[...] = jnp.broadcast_to(count, cnt_ref.shape)
    fields = (i1.astype(F32), i2.astype(F32), pos1, pos2, w1, w2)
    field_row = lax.broadcasted_iota(jnp.int32, route_ref.shape, 0)
    route = jnp.zeros(route_ref.shape, F32)
    for n, val in enumerate(fields):
        route = jnp.where(field_row == n, val, route)
    route_ref[...] = route


def _outproj_router(lru, att, x, mod_l, ln_g, w_out_b, router):
    bsz, seq, d = x.shape
    w = lru.shape[-1]
    tm = MOE_TB
    nt = seq // tm
    e_rows = 2 * SUBLANES
    rt = jnp.zeros((e_rows, d), F32).at[:N_EXPERTS].set(router.T)
    row = lambda k: pl.BlockSpec((None, None, 1, d), lambda b, i: (b, k, 0, 0))
    tok = lambda n: pl.BlockSpec((None, tm, n), lambda b, i: (b, i, 0))
    return pl.pallas_call(
        functools.partial(_outproj_router_kernel, w=w),
        grid=(bsz, nt),
        in_specs=[tok(w), tok(w), tok(d), row(2), row(4), row(3),
                  pl.BlockSpec((1, d), lambda b, i: (0, 0)),
                  pl.BlockSpec((d, d), lambda b, i: (0, 0)),
                  pl.BlockSpec((e_rows, d), lambda b, i: (0, 0))],
        out_specs=[
            tok(d), tok(d),
            pl.BlockSpec((None, SUBLANES, tm), lambda b, i: (b * nt + i, 0, 0)),
            pl.BlockSpec((e_rows, LANES), lambda b, i: (b * nt + i, 0))],
        out_shape=[jax.ShapeDtypeStruct((bsz, seq, d), F32),
                   jax.ShapeDtypeStruct((bsz, seq, d), BF16),
                   jax.ShapeDtypeStruct((bsz * nt, SUBLANES, tm), F32),
                   jax.ShapeDtypeStruct((bsz * nt * e_rows, LANES), F32)],
        compiler_params=_cparams(("arbitrary", "arbitrary")),
        name="outproj_router",
    )(lru, att, x, mod_l, mod_l, mod_l, ln_g.reshape(1, d), w_out_b, rt)


def _ffn_kernel(*refs, w, final_norm):
    (lru_ref, att_ref, x_ref, g1_ref, sc_ref, sh_ref, ln_ref, wo_ref,
     wg_ref, wu_ref, wd_ref, g2_ref) = refs[:12]
    fg_ref = refs[12] if final_norm else None
    o_ref, x1_ref, h2_ref, acc_ref = refs[-4:]
    j = pl.program_id(2)

    @pl.when(j == 0)
    def _():
        y = (jnp.dot(lru_ref[...], wo_ref[:w, :], preferred_element_type=F32)
             + jnp.dot(att_ref[...], wo_ref[w:, :], preferred_element_type=F32))
        x1 = x_ref[...] + g1_ref[...] * y
        x1_ref[...] = x1
        h2 = _rmsnorm(x1, ln_ref[...]) * (1.0 + sc_ref[...]) + sh_ref[...]
        h2_ref[...] = h2.astype(BF16)
        acc_ref[...] = jnp.zeros_like(acc_ref)

    h = h2_ref[...]
    act = (jax.nn.silu(jnp.dot(h, wg_ref[...], preferred_element_type=F32))
           * jnp.dot(h, wu_ref[...], preferred_element_type=F32))
    acc_ref[...] += jnp.dot(act.astype(BF16), wd_ref[...], preferred_element_type=F32)

    @pl.when(j == pl.num_programs(2) - 1)
    def _():
        out = x1_ref[...] + g2_ref[...] * acc_ref[...]
        if final_norm:
            out = _rmsnorm(out, fg_ref[...])
        o_ref[...] = out


def _ffn(lru, att, x, mod_l, ln_g, w_out_b, wg, wu, wd, final_g=None, *, tm=512, tf=1536):
    bsz, seq, d = x.shape
    w = lru.shape[-1]
    ff = wg.shape[1]
    tf = min(tf, ff)
    final_norm = final_g is not None
    row = lambda k: pl.BlockSpec((None, None, 1, d), lambda b, i, j: (b, k, 0, 0))
    tok = lambda n: pl.BlockSpec((None, tm, n), lambda b, i, j: (b, i, 0))
    in_specs = [tok(w), tok(w), tok(d), row(2), row(4), row(3),
                pl.BlockSpec((1, d), lambda b, i, j: (0, 0)),
                pl.BlockSpec((d, d), lambda b, i, j: (0, 0)),
                pl.BlockSpec((d, tf), lambda b, i, j: (0, j)),
                pl.BlockSpec((d, tf), lambda b, i, j: (0, j)),
                pl.BlockSpec((tf, d), lambda b, i, j: (j, 0)),
                row(5)]
    args = [lru, att, x, mod_l, mod_l, mod_l, ln_g.reshape(1, d), w_out_b, wg, wu, wd, mod_l]
    if final_norm:
        in_specs.append(pl.BlockSpec((1, d), lambda b, i, j: (0, 0)))
        args.append(final_g.reshape(1, d))
    return pl.pallas_call(
        functools.partial(_ffn_kernel, w=w, final_norm=final_norm),
        grid=(bsz, seq // tm, ff // tf),
        in_specs=in_specs,
        out_specs=tok(d),
        out_shape=jax.ShapeDtypeStruct((bsz, seq, d), F32),
        scratch_shapes=[pltpu.VMEM((tm, d), F32), pltpu.VMEM((tm, d), BF16),
                        pltpu.VMEM((tm, d), F32)],
        compiler_params=_cparams(("arbitrary",) * 3),
        name="outproj_ffn",
    )(*args)


MOE_TB = 512
MOE_SEG = 16
MOE_TF = 512
MOE_LR = 2 * MOE_TB + N_EXPERTS * MOE_SEG


def _moe_plan(cnt, *, m_tok):
    i32 = jnp.int32
    nb = m_tok // MOE_TB
    n_e = N_EXPERTS
    rows = -(-(2 * m_tok + nb * n_e * MOE_SEG + n_e * MOE_TF) // MOE_TF) * MOE_TF
    ntf = rows // MOE_TF
    n = cnt.reshape(nb, -1, LANES)[:, :n_e, 0].astype(i32)
    seg_n = (n + MOE_SEG - 1) // MOE_SEG
    seg_src = jnp.cumsum(seg_n, axis=1) - seg_n
    used = jnp.sum(seg_n, axis=0)
    per_tile = MOE_TF // MOE_SEG
    gsz = (used + per_tile - 1) // per_tile * per_tile
    gend = jnp.cumsum(gsz)
    goff = gend - gsz
    seg_dst = goff[None, :] + jnp.cumsum(seg_n, axis=0) - seg_n
    total_tiles = gend[-1] // per_tile
    tile = jnp.arange(ntf, dtype=i32)
    f_valid = (tile < total_tiles).astype(i32)
    f_exp = jnp.minimum(
        jnp.sum(gend[None, :] <= (jnp.minimum(tile, total_tiles - 1) * per_tile)[:, None],
                axis=1).astype(i32), n_e - 1)
    tail_dst = jnp.concatenate([goff + used, gend[-1:]])
    tail_n = jnp.concatenate([gsz - used, rows // MOE_SEG - gend[-1:]])
    return dict(seg_n=seg_n.reshape(-1), seg_src=seg_src.reshape(-1),
                seg_dst=seg_dst.reshape(-1), blk_n=jnp.sum(seg_n, axis=1),
                tail_dst=tail_dst, tail_n=tail_n,
                f_exp=f_exp, f_valid=f_valid, rows=rows, ntf=ntf, nb=nb)


def _seg_rows(unit):
    return pl.ds(pl.multiple_of(unit * MOE_SEG, MOE_SEG), MOE_SEG)


def _wait_segments(sem, buf_ref, n):
    def body(_, carry):
        pltpu.make_async_copy(buf_ref.at[pl.ds(0, MOE_SEG)], buf_ref.at[pl.ds(0, MOE_SEG)],
                              sem).wait()
        return carry
    lax.fori_loop(0, n, body, 0)


def _moe_scatter_kernel(seg_n_ref, seg_src_ref, seg_dst_ref, tail_dst_ref, tail_n_ref,
                        h_ref, route_ref, xs_hbm, buf_ref, zero_ref, sem, tail_sem,
                        issued_ref):
    tb = pl.program_id(0)
    nb = pl.num_programs(0)
    slot = tb % 2
    buf = buf_ref.at[slot]

    @pl.when(tb >= 2)
    def _():
        _wait_segments(sem.at[slot], buf, issued_ref[slot])

    rows = lax.broadcasted_iota(jnp.int32, (MOE_LR, MOE_TB), 0)
    p1 = route_ref[2:3, :].astype(jnp.int32)
    p2 = route_ref[3:4, :].astype(jnp.int32)
    sel = jnp.where(p1 == rows, 1.0, jnp.where(p2 == rows, 1.0, 0.0))
    buf[...] = jnp.dot(sel.astype(BF16), h_ref[...],
                       preferred_element_type=F32).astype(buf_ref.dtype)

    issued = 0
    for e in range(N_EXPERTS):
        k = tb * N_EXPERTS + e
        n, src, dst = seg_n_ref[k], seg_src_ref[k], seg_dst_ref[k]

        def copy_seg(g, carry):
            pltpu.make_async_copy(buf.at[_seg_rows(src + g)], xs_hbm.at[_seg_rows(dst + g)],
                                  sem.at[slot]).start()
            return carry
        lax.fori_loop(0, n, copy_seg, 0)
        issued = issued + n
    issued_ref[slot] = issued

    @pl.when(tb == nb - 1)
    def _():
        zero_ref[...] = jnp.zeros_like(zero_ref)
        n_tail = 0
        for e in range(N_EXPERTS + 1):
            n, dst = tail_n_ref[e], tail_dst_ref[e]

            def zero_seg(g, carry):
                pltpu.make_async_copy(zero_ref, xs_hbm.at[_seg_rows(dst + g)], tail_sem).start()
                return carry
            lax.fori_loop(0, n, zero_seg, 0)
            n_tail = n_tail + n
        _wait_segments(tail_sem, zero_ref, n_tail)
        _wait_segments(sem.at[slot], buf, issued_ref[slot])

        @pl.when(nb >= 2)
        def _():
            _wait_segments(sem.at[1 - slot], buf, issued_ref[1 - slot])


def _moe_scatter(h2, plan, route_t):
    m_tok, d = h2.shape
    nb = plan['nb']
    grid_spec = pltpu.PrefetchScalarGridSpec(
        num_scalar_prefetch=5,
        grid=(nb,),
        in_specs=[pl.BlockSpec((MOE_TB, d), lambda t, *_: (t, 0)),
                  pl.BlockSpec((None, SUBLANES, MOE_TB), lambda t, *_: (t, 0, 0))],
        out_specs=pl.BlockSpec(memory_space=pl.ANY),
        scratch_shapes=[pltpu.VMEM((2, MOE_LR, d), BF16), pltpu.VMEM((MOE_SEG, d), BF16),
                        pltpu.SemaphoreType.DMA((2,)), pltpu.SemaphoreType.DMA(()),
                        pltpu.SMEM((2,), jnp.int32)],
    )
    return pl.pallas_call(
        _moe_scatter_kernel,
        grid_spec=grid_spec,
        out_shape=jax.ShapeDtypeStruct((plan['rows'], d), BF16),
        compiler_params=_cparams(("arbitrary",)),
        name="moe_scatter",
    )(plan['seg_n'], plan['seg_src'], plan['seg_dst'], plan['tail_dst'], plan['tail_n'],
      h2, route_t)


def _moe_ffn_kernel(exp_ref, valid_ref, xs_ref, wg_ref, wu_ref, wd_ref, o_ref, acc_ref):
    n = pl.program_id(0)
    j = pl.program_id(1)

    @pl.when(valid_ref[n] == 0)
    def _():
        o_ref[...] = jnp.zeros_like(o_ref)

    @pl.when(valid_ref[n] != 0)
    def _():
        @pl.when(j == 0)
        def _():
            acc_ref[...] = jnp.zeros_like(acc_ref)

        h = xs_ref[...]
        act = (jax.nn.silu(jnp.dot(h, wg_ref[...], preferred_element_type=F32))
               * jnp.dot(h, wu_ref[...], preferred_element_type=F32))
        acc_ref[...] += jnp.dot(act.astype(BF16), wd_ref[...], preferred_element_type=F32)

        @pl.when(j == pl.num_programs(1) - 1)
        def _():
            o_ref[...] = acc_ref[...].astype(o_ref.dtype)


def _moe_ffn(xs, wg, wu, wd, plan, *, tf=1536):
    rows, d = xs.shape
    ff = wg.shape[2]
    tf = min(tf, ff)
    nj = ff // tf

    def ff_tile(n, j, v):
        return j * v[n] + (nj - 1) * (1 - v[n])

    grid_spec = pltpu.PrefetchScalarGridSpec(
        num_scalar_prefetch=2,
        grid=(plan['ntf'], nj),
        in_specs=[pl.BlockSpec((MOE_TF, d), lambda n, j, e, v: (n, 0)),
                  pl.BlockSpec((None, d, tf), lambda n, j, e, v: (e[n], 0, ff_tile(n, j, v))),
                  pl.BlockSpec((None, d, tf), lambda n, j, e, v: (e[n], 0, ff_tile(n, j, v))),
                  pl.BlockSpec((None, tf, d), lambda n, j, e, v: (e[n], ff_tile(n, j, v), 0))],
        out_specs=pl.BlockSpec((MOE_TF, d), lambda n, j, e, v: (n, 0)),
        scratch_shapes=[pltpu.VMEM((MOE_TF, d), F32)],
    )
    return pl.pallas_call(
        _moe_ffn_kernel,
        grid_spec=grid_spec,
        out_shape=jax.ShapeDtypeStruct((rows, d), BF16),
        compiler_params=_cparams(("arbitrary", "arbitrary")),
        name="moe_ffn",
    )(plan['f_exp'], plan['f_valid'], xs, wg, wu, wd)


def _token_columns(route, field):
    pad = jnp.zeros_like(route)
    pick = jnp.where(lax.broadcasted_iota(jnp.int32, (2 * SUBLANES, LANES), 0) == field,
                     1.0, 0.0).astype(BF16)
    out = None
    rest = route
    for _ in range(3):
        piece = rest.astype(BF16)
        rest = rest - piece.astype(F32)
        term = lax.dot_general(jnp.concatenate([piece, pad.astype(BF16)], axis=0), pick,
                               (((0,), (0,)), ((), ())), preferred_element_type=F32)
        out = term if out is None else out + term
    return out


def _moe_combine_kernel(*refs, final_norm):
    if final_norm:
        (seg_n_ref, seg_src_ref, seg_dst_ref, blk_n_ref, ye_hbm, route_ref,
         x1_ref, g2_ref, fg_ref, o_ref, buf_ref, sem) = refs
    else:
        (seg_n_ref, seg_src_ref, seg_dst_ref, blk_n_ref, ye_hbm, route_ref,
         x1_ref, g2_ref, o_ref, buf_ref, sem) = refs
    tb = pl.program_id(0)
    nb = pl.num_programs(0)
    slot = tb % 2

    def fetch(block, into):
        for e in range(N_EXPERTS):
            k = block * N_EXPERTS + e
            n, src, dst = seg_n_ref[k], seg_src_ref[k], seg_dst_ref[k]

            def copy_seg(g, carry):
                pltpu.make_async_copy(ye_hbm.at[_seg_rows(dst + g)],
                                      buf_ref.at[into, _seg_rows(src + g)], sem.at[into]).start()
                return carry
            lax.fori_loop(0, n, copy_seg, 0)

    @pl.when(tb == 0)
    def _():
        fetch(tb, slot)

    @pl.when(tb + 1 < nb)
    def _():
        fetch(tb + 1, 1 - slot)

    buf = buf_ref.at[slot]
    _wait_segments(sem.at[slot], buf, blk_n_ref[tb])

    def clear(g, carry):
        buf[_seg_rows(g), :] = jnp.zeros((MOE_SEG, buf.shape[1]), buf.dtype)
        return carry
    lax.fori_loop(blk_n_ref[tb], MOE_LR // MOE_SEG, clear, 0)

    ye = buf[...]
    reps = MOE_LR // LANES
    cols = lax.broadcasted_iota(jnp.int32, (MOE_TB, MOE_LR), 1)

    route = route_ref[...]

    def unsort(field):
        pos = _token_columns(route, field).astype(jnp.int32)
        hit = jnp.tile(pos, (1, reps)) == cols
        return jnp.dot(jnp.where(hit, 1.0, 0.0).astype(BF16), ye, preferred_element_type=F32)

    lanes = x1_ref.shape[1] // LANES
    y = (jnp.tile(_token_columns(route, 4), (1, lanes)) * unsort(2)
         + jnp.tile(_token_columns(route, 5), (1, lanes)) * unsort(3))
    out = x1_ref[...] + g2_ref[...] * y
    if final_norm:
        out = _rmsnorm(out, fg_ref[...])
    o_ref[...] = out


def _moe_combine(ye, plan, route_t, x1, mod_l, final_g=None):
    bsz, seq, d = x1.shape
    m_tok = bsz * seq
    nb = plan['nb']
    blocks_per_seq = seq // MOE_TB
    final_norm = final_g is not None
    in_specs = [pl.BlockSpec(memory_space=pl.ANY),
                pl.BlockSpec((None, SUBLANES, MOE_TB), lambda t, *_: (t, 0, 0)),
                pl.BlockSpec((MOE_TB, d), lambda t, *_: (t, 0)),
                pl.BlockSpec((None, None, 1, d), lambda t, *_: (t // blocks_per_seq, 5, 0, 0))]
    args = [ye, route_t, x1.reshape(m_tok, d), mod_l]
    if final_norm:
        in_specs.append(pl.BlockSpec((1, d), lambda t, *_: (0, 0)))
        args.append(final_g.reshape(1, d))
    grid_spec = pltpu.PrefetchScalarGridSpec(
        num_scalar_prefetch=4,
        grid=(nb,),
        in_specs=in_specs,
        out_specs=pl.BlockSpec((MOE_TB, d), lambda t, *_: (t, 0)),
        scratch_shapes=[pltpu.VMEM((2, MOE_LR, d), BF16), pltpu.SemaphoreType.DMA((2,))],
    )
    out = pl.pallas_call(
        functools.partial(_moe_combine_kernel, final_norm=final_norm),
        grid_spec=grid_spec,
        out_shape=jax.ShapeDtypeStruct((m_tok, d), F32),
        compiler_params=_cparams(("arbitrary",)),
        name="moe_combine",
    )(plan['seg_n'], plan['seg_src'], plan['seg_dst'], plan['blk_n'], *args)
    return out.reshape(bsz, seq, d)


def _moe(h2, route_t, cnt, x1, mod_l, wg, wu, wd, final_g=None):
    bsz, seq, d = x1.shape
    m_tok = bsz * seq
    plan = _moe_plan(cnt, m_tok=m_tok)
    xs = _moe_scatter(h2.reshape(m_tok, d), plan, route_t)
    ye = _moe_ffn(xs, wg, wu, wd, plan)
    return _moe_combine(ye, plan, route_t, x1, mod_l, final_g)


def kernel(x, c, positions, ada_w, ada_b, ln1_g, ln2_g, w_in, conv_w, conv_b, gate_a_w, gate_a_b, gate_x_w, gate_x_b, lru_lambda, lam_q1, lam_k1, lam_q2, lam_k2, subln_g, w_out, ffn_w_gate, ffn_w_up, ffn_w_down, moe_router, moe_w_gate, moe_w_up, moe_w_down, final_g):
    depth = ada_w.shape[0]
    vd = subln_g.shape[-1]
    dh = vd // 2
    attn_w = DIFF_HEADS * vd

    mod = _modulation(c, ada_w, ada_b)
    cos_t, sin_t = _rope_tables(positions, dh)
    for l in range(depth):
        lambda_init = 0.8 - 0.6 * math.exp(-0.3 * l)
        mod_l = mod[l]
        lru, q, k, v = _inproj(x, mod_l, ln1_g[l], w_in[l].astype(BF16), cos_t, sin_t,
                               conv_w[l], conv_b[l], gate_a_w[l], gate_a_b[l], gate_x_w[l],
                               gate_x_b[l], lru_lambda[l], attn_w=attn_w, dh=dh)
        att = _attention(q, k, v, positions, lam_q1[l], lam_k1[l], lam_q2[l], lam_k2[l],
                         subln_g[l], lambda_init, dh=dh)
        fg = final_g if l == depth - 1 else None
        j = l // 2
        if l % 2 == 0:
            x = _ffn(lru, att, x, mod_l, ln2_g[l], w_out[l].astype(BF16),
                     ffn_w_gate[j].astype(BF16), ffn_w_up[j].astype(BF16),
                     ffn_w_down[j].astype(BF16), final_g=fg)
        else:
            x1, h2, route, cnt = _outproj_router(lru, att, x, mod_l, ln2_g[l],
                                                 w_out[l].astype(BF16), moe_router[j])
            x = _moe(h2, route, cnt, x1, mod_l, moe_w_gate[j].astype(BF16),
                     moe_w_up[j].astype(BF16), moe_w_down[j].astype(BF16), final_g=fg)
    return x
```

```python
import functools
import math

import jax
import jax.numpy as jnp
from jax import lax
from jax.experimental import pallas as pl
from jax.experimental.pallas import tpu as pltpu

F32 = jnp.float32
BF16 = jnp.bfloat16
HIGHEST = lax.Precision.HIGHEST

CHUNK = 64
CONV_W = 4
RG_C = 8.0
DIFF_HEADS = 4
ROPE_THETA = 10000.0
N_EXPERTS = 8
EPS = 1e-6
LANES = 128
SUBLANES = 8
VMEM_LIMIT = 56 * 1024 * 1024
MASK_VALUE = -0.5 * float(jnp.finfo(jnp.float32).max)
MASK_BIAS = 2.0 ** 100


def _cparams(sem):
    return pltpu.CompilerParams(dimension_semantics=sem, vmem_limit_bytes=VMEM_LIMIT)


def _rmsnorm(x, g):
    return x * lax.rsqrt(jnp.mean(x * x, axis=-1, keepdims=True) + EPS) * g


def _mod_kernel(c_ref, w_ref, b_ref, o_ref):
    c = c_ref[...]
    s = c * jax.nn.sigmoid(c)
    o_ref[...] = jnp.dot(s, w_ref[...], precision=HIGHEST,
                         preferred_element_type=F32) + b_ref[...]


def _modulation(c, ada_w, ada_b, tn=1024):
    depth, d, n = ada_w.shape
    bsz = c.shape[0]
    rows = -(-bsz // SUBLANES) * SUBLANES
    c_pad = jnp.zeros((rows, d), F32).at[:bsz].set(c)
    out = pl.pallas_call(
        _mod_kernel,
        grid=(depth, n // tn),
        in_specs=[
            pl.BlockSpec((rows, d), lambda l, j: (0, 0)),
            pl.BlockSpec((None, d, tn), lambda l, j: (l, 0, j)),
            pl.BlockSpec((None, 1, tn), lambda l, j: (l, 0, j)),
        ],
        out_specs=pl.BlockSpec((None, rows, tn), lambda l, j: (l, 0, j)),
        out_shape=jax.ShapeDtypeStruct((depth, rows, n), F32),
        compiler_params=_cparams(("arbitrary", "arbitrary")),
        name="adaln_mod",
    )(c_pad, ada_w, ada_b.reshape(depth, 1, n))
    return out[:, :bsz].reshape(depth, bsz, 6, 1, d)


def _rope_table_kernel(pos_ref, inv_ref, cos_ref, sin_ref, *, n_freq):
    ang = pos_ref[...].astype(F32) * inv_ref[...]
    groups = LANES // n_freq
    row = lax.broadcasted_iota(jnp.int32, (LANES, groups * LANES), 0)
    col = lax.broadcasted_iota(jnp.int32, (LANES, groups * LANES), 1)
    hit = row == (col // LANES) * n_freq + col % n_freq
    spread_cos = jnp.where(hit, 1.0, 0.0).astype(BF16)
    first_half = col % (2 * n_freq) < n_freq
    spread_sin = jnp.where(hit, jnp.where(first_half, -1.0, 1.0), 0.0).astype(BF16)

    def spread(t, e):
        out = None
        rest = t
        for _ in range(3):
            piece = rest.astype(BF16)
            rest = rest - piece.astype(F32)
            term = jnp.dot(piece, e, preferred_element_type=F32)
            out = term if out is None else out + term
        return out

    c = spread(jnp.cos(ang), spread_cos)
    s = spread(jnp.sin(ang), spread_sin)
    for g in range(groups):
        cos_ref[g] = c[:, g * LANES:(g + 1) * LANES]
        sin_ref[g] = s[:, g * LANES:(g + 1) * LANES]


def _rope_tables(positions, dh):
    n_freq = dh // 2
    groups = LANES // n_freq
    tok = positions.size
    rows = tok // groups
    inv = ROPE_THETA ** (-jnp.arange(0, dh, 2, dtype=F32) / dh)
    pos_x = jnp.repeat(positions.reshape(groups, rows).T, n_freq, axis=1)
    inv_x = jnp.tile(inv, groups).reshape(1, LANES)
    tr = min(rows, 1024)
    cos, sin = pl.pallas_call(
        functools.partial(_rope_table_kernel, n_freq=n_freq),
        grid=(rows // tr,),
        in_specs=[pl.BlockSpec((tr, LANES), lambda i: (i, 0)),
                  pl.BlockSpec((1, LANES), lambda i: (0, 0))],
        out_specs=[pl.BlockSpec((groups, tr, LANES), lambda i: (0, i, 0))] * 2,
        out_shape=[jax.ShapeDtypeStruct((groups, rows, LANES), F32)] * 2,
        compiler_params=_cparams(("arbitrary",)),
        name="rope_tables",
    )(pos_x, inv_x)
    return cos.reshape(tok, LANES), sin.reshape(tok, LANES)


def _inproj_kernel(x_ref, sc_ref, sh_ref, g_ref, w_ref, cos_ref, sin_ref,
                   cw_ref, cb_ref, wg_ref, bg_ref, lam_ref,
                   lru_ref, q_ref, k_ref, v_ref, xpad_ref, h_ref, *, lru_w, attn_w, dh):
    @pl.when(pl.program_id(1) == 0)
    def _():
        xpad_ref[0:SUBLANES, :] = jnp.zeros((SUBLANES, lru_w), F32)
        h_ref[...] = jnp.zeros_like(h_ref)

    h = _rmsnorm(x_ref[...], g_ref[...]) * (1.0 + sc_ref[...]) + sh_ref[...]
    hb = h.astype(BF16)
    lru2 = 2 * lru_w
    xy = jnp.dot(hb, w_ref[:, :lru2], preferred_element_type=F32)

    reps = attn_w // LANES
    cos = jnp.tile(cos_ref[...], (1, reps))
    sin = jnp.tile(sin_ref[...], (1, reps))
    lane = lax.broadcasted_iota(jnp.int32, cos.shape, 1)
    first_half = (lane % dh) < (dh // 2)

    def rope(t, cos_t, sin_t):
        fwd = pltpu.roll(t, attn_w - dh // 2, axis=1)
        bwd = pltpu.roll(t, dh // 2, axis=1)
        return t * cos_t + jnp.where(first_half, fwd, bwd) * sin_t

    q = jnp.dot(hb, w_ref[:, lru2:lru2 + attn_w], preferred_element_type=F32)
    q_scale = dh ** -0.5 * math.log2(math.e)
    q_ref[...] = rope(q, jnp.tile(cos_ref[...] * q_scale, (1, reps)),
                      jnp.tile(sin_ref[...] * q_scale, (1, reps))).astype(BF16)
    k = jnp.dot(hb, w_ref[:, lru2 + attn_w:lru2 + 2 * attn_w], preferred_element_type=F32)
    k_ref[...] = rope(k, cos, sin).astype(BF16)
    v = jnp.dot(hb, w_ref[:, lru2 + 2 * attn_w:], preferred_element_type=F32)
    v_ref[...] = v.astype(BF16)

    u, gates = _lru_conv_gates(xy[:, :lru_w], cw_ref, cb_ref, wg_ref, bg_ref, xpad_ref)
    a, bt = _lru_coeffs(u, gates, lam_ref)
    lru_ref[...] = _lru_scan(a, bt, xy[:, lru_w:], h_ref).astype(lru_ref.dtype)


def _inproj(x, mod_l, ln_g, w_in_b, cos_t, sin_t, conv_w, conv_b, wa, ba, wx, bx, lam, *,
            attn_w, dh, tm=1024):
    bsz, seq, d = x.shape
    nt = seq // tm
    d_in = w_in_b.shape[1]
    lru_w = conv_w.shape[-1]
    wg = jnp.concatenate([_block_diag(wa), _block_diag(wx)], axis=1).astype(BF16)
    bg = jnp.concatenate([ba, bx]).reshape(1, 2 * lru_w)
    row = lambda k: pl.BlockSpec((None, None, 1, d), lambda b, i: (b, k, 0, 0))
    tok = lambda w: pl.BlockSpec((None, tm, w), lambda b, i: (b, i, 0))
    const = lambda shape: pl.BlockSpec(shape, lambda b, i: (0,) * len(shape))
    return pl.pallas_call(
        functools.partial(_inproj_kernel, lru_w=lru_w, attn_w=attn_w, dh=dh),
        grid=(bsz, nt),
        in_specs=[
            tok(d), row(1), row(0), const((1, d)), const((d, d_in)),
            pl.BlockSpec((tm, LANES), lambda b, i: (b * nt + i, 0)),
            pl.BlockSpec((tm, LANES), lambda b, i: (b * nt + i, 0)),
            const((CONV_W, lru_w)), const((1, lru_w)), const((lru_w, 2 * lru_w)),
            const((1, 2 * lru_w)), const((1, lru_w)),
        ],
        out_specs=[tok(lru_w), tok(attn_w), tok(attn_w), tok(attn_w)],
        out_shape=[jax.ShapeDtypeStruct((bsz, seq, lru_w), BF16),
                   jax.ShapeDtypeStruct((bsz, seq, attn_w), BF16),
                   jax.ShapeDtypeStruct((bsz, seq, attn_w), BF16),
                   jax.ShapeDtypeStruct((bsz, seq, attn_w), BF16)],
        scratch_shapes=[pltpu.VMEM((tm + SUBLANES, lru_w), F32), pltpu.VMEM((1, lru_w), F32)],
        compiler_params=_cparams(("arbitrary", "arbitrary")),
        name="inproj_lru",
    )(x, mod_l, mod_l, ln_g.reshape(1, d), w_in_b, cos_t, sin_t,
      conv_w, conv_b.reshape(1, lru_w), wg, bg, lam.reshape(1, lru_w))


def _gelu_tanh(x):
    return 0.5 * x * (1.0 + jnp.tanh(math.sqrt(2.0 / math.pi) * (x + 0.044715 * (x * x * x))))


def _lru_conv_gates(xr, cw_ref, cb_ref, wg_ref, bg_ref, xpad_ref):
    t, w = xr.shape
    xpad_ref[SUBLANES:SUBLANES + t, :] = xr
    u = cb_ref[...]
    for j in range(CONV_W):
        off = SUBLANES - (CONV_W - 1) + j
        u = u + cw_ref[j:j + 1, :] * xpad_ref[off:off + t, :]
    xpad_ref[0:SUBLANES, :] = xpad_ref[t:t + SUBLANES, :]
    gates = jnp.dot(u.astype(BF16), wg_ref[...], preferred_element_type=F32) + bg_ref[...]
    return u, gates


def _lru_coeffs(u, gates, lam_ref):
    w = u.shape[1]
    r = jax.nn.sigmoid(gates[:, :w])
    ig = jax.nn.sigmoid(gates[:, w:])
    neg_lam = -lam_ref[...]
    softplus = jnp.maximum(neg_lam, 0.0) + jnp.log1p(jnp.exp(-jnp.abs(neg_lam)))
    log_a = (-RG_C) * r * softplus
    a = jnp.exp(log_a)
    return a, jnp.sqrt(1.0 - a * a) * (ig * u)


def _lru_scan(a, bt, yr, h_ref):
    t, w = a.shape
    groups = t // SUBLANES
    a = a.reshape(groups, SUBLANES, w)
    bt = bt.reshape(groups, SUBLANES, w)
    sub = lax.broadcasted_iota(jnp.int32, a.shape, 1)
    shift = 1
    while shift < SUBLANES:
        keep = sub >= shift
        a_prev = jnp.where(keep, pltpu.roll(a, shift, axis=1), 1.0)
        b_prev = jnp.where(keep, pltpu.roll(bt, shift, axis=1), 0.0)
        bt = a * b_prev + bt
        a = a * a_prev
        shift *= 2
    carry = h_ref[...]
    rows = []
    for g in range(groups):
        hg = a[g] * carry + bt[g]
        rows.append(hg)
        carry = hg[SUBLANES - 1:SUBLANES, :]
    h_ref[...] = carry
    return jnp.concatenate(rows, axis=0) * _gelu_tanh(yr)


def _block_diag(wb):
    n, bw, _ = wb.shape
    eye = jnp.eye(n, dtype=wb.dtype)
    return jnp.einsum('nhk,nm->nhmk', wb, eye).reshape(n * bw, n * bw)


def _attn_kernel(qmin_ref, qmax_ref, kmin_ref, kmax_ref, qhmax_ref, khmin_ref,
                 q_ref, k_ref, v_ref, cq_ref, ck_ref, ckcol_ref, lq1_ref, lk1_ref, lq2_ref, lk2_ref,
                 g_ref, o_ref, m_ref, l_ref, acc_ref, *, tq, tk, nk, dh, lambda_init):
    b = pl.program_id(0)
    i = pl.program_id(2)
    m_ref[...] = jnp.full(m_ref.shape, -jnp.inf, F32)
    l_ref[...] = jnp.zeros(l_ref.shape, F32)
    acc_ref[...] = jnp.zeros(acc_ref.shape, F32)

    q = q_ref[...]
    lane = lax.broadcasted_iota(jnp.int32, q.shape, 1)
    qc = (jnp.where(lane < dh, q, jnp.zeros_like(q)), jnp.where(lane >= dh, q, jnp.zeros_like(q)))
    q_lo = qmin_ref[b, i]
    q_hi = qmax_ref[b, i]

    def process(j, mode, r0=0, rn=tq, k0=0, kn=tk):
        start = pl.multiple_of(j * tk, tk) + k0
        rows = slice(r0, r0 + rn)
        kb = k_ref[pl.ds(start, kn), :]
        vb = v_ref[pl.ds(start, kn), :]
        lhs = tuple(t[rows] for t in qc)
        if mode == "select":
            ck = ck_ref[:, pl.ds(start, kn)]
            visible = ck <= jnp.tile(cq_ref[rows, :], (1, kn // LANES))
        elif mode == "folded":
            c0 = kmin_ref[b, j]
            lane_k = lax.broadcasted_iota(jnp.int32, (kn, LANES), 1)
            k_chunk = jnp.where(ckcol_ref[pl.ds(start, kn), :] - c0 == lane_k, 1.0, 0.0)
            kb = jnp.concatenate([kb, k_chunk.astype(BF16)], axis=1)
            lane_q = lax.broadcasted_iota(jnp.int32, (rn, LANES), 1)
            q_bias = jnp.where(cq_ref[rows, :] - c0 < lane_q, -MASK_BIAS, 0.0).astype(BF16)
            lhs = tuple(jnp.concatenate([t, q_bias], axis=1) for t in lhs)
        for c in range(2):
            s = lax.dot_general(lhs[c], kb, (((1,), (1,)), ((), ())),
                                preferred_element_type=F32)
            if mode == "select":
                s = jnp.where(visible, s, MASK_VALUE)
            m_prev = m_ref[c, rows]
            m_new = jnp.maximum(m_prev, jnp.max(s, axis=-1, keepdims=True))
            alpha = jnp.exp2(m_prev - m_new)
            p = jnp.exp2(s - jnp.tile(m_new, (1, kn // LANES)))
            p_lanes = p[:, :LANES]
            for t in range(1, kn // LANES):
                p_lanes = p_lanes + p[:, t * LANES:(t + 1) * LANES]
            l_ref[c, rows] = alpha * l_ref[c, rows] + p_lanes
            acc_ref[c, rows] = alpha * acc_ref[c, rows] + jnp.dot(
                p.astype(BF16), vb, preferred_element_type=F32)
            m_ref[c, rows] = m_new

    def is_plain(j):
        return jnp.logical_and(kmin_ref[b, j] <= q_hi, kmax_ref[b, j] <= q_lo)

    def one_block(j, carry):
        k_lo = kmin_ref[b, j]
        k_hi = kmax_ref[b, j]
        needed = k_lo <= q_hi
        needs_mask = jnp.logical_and(needed, k_hi > q_lo)
        foldable = k_hi - k_lo < LANES
        fold = jnp.logical_and(needs_mask, foldable)
        corner_hidden = khmin_ref[b, 2 * j + 1] > qhmax_ref[b, 2 * i]

        @pl.when(jnp.logical_and(fold, corner_hidden))
        def _():
            process(j, "folded", 0, tq // 2, 0, tk // 2)
            process(j, "folded", tq // 2, tq // 2, 0, tk)

        @pl.when(jnp.logical_and(fold, jnp.logical_not(corner_hidden)))
        def _():
            process(j, "folded")

        @pl.when(jnp.logical_and(needs_mask, jnp.logical_not(foldable)))
        def _():
            process(j, "select")

        @pl.when(jnp.logical_and(needed, jnp.logical_not(needs_mask)))
        def _():
            process(j, "plain")

        return carry

    def two_blocks(jp, carry):
        j0 = 2 * jp
        both_plain = jnp.logical_and(is_plain(j0), is_plain(j0 + 1))

        @pl.when(both_plain)
        def _():
            process(j0, "plain", 0, tq, 0, 2 * tk)

        @pl.when(jnp.logical_not(both_plain))
        def _():
            lax.fori_loop(j0, j0 + 2, one_block, 0)

        return carry

    lax.fori_loop(0, nk // 2, two_blocks, 0)
    if nk % 2:
        one_block(nk - 1, 0)

    lam = (jnp.exp(jnp.sum(lq1_ref[...] * lk1_ref[...], keepdims=True))
           - jnp.exp(jnp.sum(lq2_ref[...] * lk2_ref[...], keepdims=True)) + lambda_init)
    l0 = jnp.sum(l_ref[0], axis=-1, keepdims=True)
    l1 = jnp.sum(l_ref[1], axis=-1, keepdims=True)
    o = acc_ref[0] / l0 - lam * (acc_ref[1] / l1)
    o_ref[...] = (_rmsnorm(o, g_ref[...]) * (1.0 - lambda_init)).astype(o_ref.dtype)


def _attention(q, k, v, positions, lq1, lk1, lq2, lk2, subln_g, lambda_init, *,
               dh, tq=1024, tk=1024):
    bsz, seq, aw = q.shape
    vd = 2 * dh
    heads = aw // vd
    nq, nk = seq // tq, seq // tk
    chunk = positions // CHUNK
    qmin = chunk.reshape(bsz, nq, tq).min(-1)
    qmax = chunk.reshape(bsz, nq, tq).max(-1)
    kmin = chunk.reshape(bsz, nk, tk).min(-1)
    kmax = chunk.reshape(bsz, nk, tk).max(-1)
    qhmax = chunk.reshape(bsz, 2 * nq, tq // 2).max(-1)
    khmin = chunk.reshape(bsz, 2 * nk, tk // 2).min(-1)
    cq = jnp.broadcast_to(chunk[:, :, None], (bsz, seq, LANES))
    ck = chunk.reshape(bsz, 1, seq)
    vec = lambda n: pl.BlockSpec((1, n), lambda b, h, i, *_: (0, 0))
    grid_spec = pltpu.PrefetchScalarGridSpec(
        num_scalar_prefetch=6,
        grid=(bsz, heads, nq),
        in_specs=[
            pl.BlockSpec((None, tq, vd), lambda b, h, i, *_: (b, i, h)),
            pl.BlockSpec((None, seq, vd), lambda b, h, i, *_: (b, 0, h)),
            pl.BlockSpec((None, seq, vd), lambda b, h, i, *_: (b, 0, h)),
            pl.BlockSpec((None, tq, LANES), lambda b, h, i, *_: (b, i, 0)),
            pl.BlockSpec((None, 1, seq), lambda b, h, i, *_: (b, 0, 0)),
            pl.BlockSpec((None, seq, LANES), lambda b, h, i, *_: (b, 0, 0)),
            vec(dh), vec(dh), vec(dh), vec(dh), vec(vd),
        ],
        out_specs=pl.BlockSpec((None, tq, vd), lambda b, h, i, *_: (b, i, h)),
        scratch_shapes=[pltpu.VMEM((2, tq, LANES), F32), pltpu.VMEM((2, tq, LANES), F32),
                        pltpu.VMEM((2, tq, vd), F32)],
    )
    return pl.pallas_call(
        functools.partial(_attn_kernel, tq=tq, tk=tk, nk=nk, dh=dh, lambda_init=lambda_init),
        grid_spec=grid_spec,
        out_shape=jax.ShapeDtypeStruct((bsz, seq, aw), BF16),
        compiler_params=_cparams(("arbitrary", "arbitrary", "arbitrary")),
        name="diff_attn",
    )(qmin, qmax, kmin, kmax, qhmax, khmin, q, k, v, cq, ck, cq,
      lq1.reshape(1, dh), lk1.reshape(1, dh), lq2.reshape(1, dh), lk2.reshape(1, dh),
      subln_g.reshape(1, vd))


def _outproj_router_kernel(lru_ref, att_ref, x_ref, g1_ref, sc_ref, sh_ref, ln_ref, wo_ref,
                           rt_ref, x1_ref, h2_ref, route_ref, cnt_ref, *, w):
    y = (jnp.dot(lru_ref[...], wo_ref[:w, :], preferred_element_type=F32)
         + jnp.dot(att_ref[...], wo_ref[w:, :], preferred_element_type=F32))
    x1 = x_ref[...] + g1_ref[...] * y
    x1_ref[...] = x1
    h2 = _rmsnorm(x1, ln_ref[...]) * (1.0 + sc_ref[...]) + sh_ref[...]
    h2_ref[...] = h2.astype(BF16)
    _route(h2, rt_ref, route_ref, cnt_ref)


def _route(h2, rt_ref, route_ref, cnt_ref):
    def split(v):
        hi = v.astype(BF16)
        return hi, (v - hi.astype(F32)).astype(BF16)

    h_hi, h_lo = split(h2)
    r_hi, r_lo = split(rt_ref[...])
    nt_dims = (((1,), (1,)), ((), ()))
    dot_nt = lambda a, b: lax.dot_general(a, b, nt_dims, preferred_element_type=F32)
    logits = dot_nt(r_hi, h_hi) + (dot_nt(r_hi, h_lo) + dot_nt(r_lo, h_hi))
    n_rows, tm = logits.shape
    row = lax.broadcasted_iota(jnp.int32, logits.shape, 0)
    lg = jnp.where(row < N_EXPERTS, logits, -jnp.inf)
    m1 = jnp.max(lg, axis=0, keepdims=True)
    i1 = jnp.min(jnp.where(lg == m1, row, n_rows), axis=0, keepdims=True)
    lg2 = jnp.where(row == i1, -jnp.inf, lg)
    m2 = jnp.max(lg2, axis=0, keepdims=True)
    i2 = jnp.min(jnp.where(lg2 == m2, row, n_rows), axis=0, keepdims=True)
    e2 = jnp.exp(m2 - m1)
    w1 = 1.0 / (1.0 + e2)
    w2 = e2 / (1.0 + e2)

    onehot = jnp.where(row == i1, 1.0, jnp.where(row == i2, 1.0, 0.0))
    tri = (lax.broadcasted_iota(jnp.int32, (tm, tm), 0)
           < lax.broadcasted_iota(jnp.int32, (tm, tm), 1))
    prefix = jnp.dot(onehot.astype(BF16), jnp.where(tri, 1.0, 0.0).astype(BF16),
                     preferred_element_type=F32)
    count = jnp.sum(onehot, axis=1, keepdims=True)
    seg_len = jnp.floor((count + (MOE_SEG - 1)) * (1.0 / MOE_SEG)) * MOE_SEG
    seg_off = jnp.zeros_like(seg_len)
    for e in range(N_EXPERTS - 1):
        seg_off = seg_off + jnp.where(row[:, :1] > e, seg_len[e:e + 1, :], 0.0)
    local = prefix + seg_off
    pos1 = jnp.sum(jnp.where(row == i1, local, 0.0), axis=0, keepdims=True)
    pos2 = jnp.sum(jnp.where(row == i2, local, 0.0), axis=0, keepdims=True)
    cnt_ref[...] = jnp.broadcast_to(count, cnt_ref.shape)
    fields = (i1.astype(F32), i2.astype(F32), pos1, pos2, w1, w2)
    field_row = lax.broadcasted_iota(jnp.int32, route_ref.shape, 0)
    route = jnp.zeros(route_ref.shape, F32)
    for n, val in enumerate(fields):
        route = jnp.where(field_row == n, val, route)
    route_ref[...] = route


def _outproj_router(lru, att, x, mod_l, ln_g, w_out_b, router):
    bsz, seq, d = x.shape
    w = lru.shape[-1]
    tm = MOE_TB
    nt = seq // tm
    e_rows = 2 * SUBLANES
    rt = jnp.zeros((e_rows, d), F32).at[:N_EXPERTS].set(router.T)
    row = lambda k: pl.BlockSpec((None, None, 1, d), lambda b, i: (b, k, 0, 0))
    tok = lambda n: pl.BlockSpec((None, tm, n), lambda b, i: (b, i, 0))
    return pl.pallas_call(
        functools.partial(_outproj_router_kernel, w=w),
        grid=(bsz, nt),
        in_specs=[tok(w), tok(w), tok(d), row(2), row(4), row(3),
                  pl.BlockSpec((1, d), lambda b, i: (0, 0)),
                  pl.BlockSpec((d, d), lambda b, i: (0, 0)),
                  pl.BlockSpec((e_rows, d), lambda b, i: (0, 0))],
        out_specs=[
            tok(d), tok(d),
            pl.BlockSpec((None, SUBLANES, tm), lambda b, i: (b * nt + i, 0, 0)),
            pl.BlockSpec((e_rows, LANES), lambda b, i: (b * nt + i, 0))],
        out_shape=[jax.ShapeDtypeStruct((bsz, seq, d), F32),
                   jax.ShapeDtypeStruct((bsz, seq, d), BF16),
                   jax.ShapeDtypeStruct((bsz * nt, SUBLANES, tm), F32),
                   jax.ShapeDtypeStruct((bsz * nt * e_rows, LANES), F32)],
        compiler_params=_cparams(("arbitrary", "arbitrary")),
        name="outproj_router",
    )(lru, att, x, mod_l, mod_l, mod_l, ln_g.reshape(1, d), w_out_b, rt)


def _ffn_kernel(*refs, w, final_norm):
    (lru_ref, att_ref, x_ref, g1_ref, sc_ref, sh_ref, ln_ref, wo_ref,
     wg_ref, wu_ref, wd_ref, g2_ref) = refs[:12]
    fg_ref = refs[12] if final_norm else None
    o_ref, x1_ref, h2_ref, acc_ref = refs[-4:]
    j = pl.program_id(2)

    @pl.when(j == 0)
    def _():
        y = (jnp.dot(lru_ref[...], wo_ref[:w, :], preferred_element_type=F32)
             + jnp.dot(att_ref[...], wo_ref[w:, :], preferred_element_type=F32))
        x1 = x_ref[...] + g1_ref[...] * y
        x1_ref[...] = x1
        h2 = _rmsnorm(x1, ln_ref[...]) * (1.0 + sc_ref[...]) + sh_ref[...]
        h2_ref[...] = h2.astype(BF16)
        acc_ref[...] = jnp.zeros_like(acc_ref)

    h = h2_ref[...]
    act = (jax.nn.silu(jnp.dot(h, wg_ref[...], preferred_element_type=F32))
           * jnp.dot(h, wu_ref[...], preferred_element_type=F32))
    acc_ref[...] += jnp.dot(act.astype(BF16), wd_ref[...], preferred_element_type=F32)

    @pl.when(j == pl.num_programs(2) - 1)
    def _():
        out = x1_ref[...] + g2_ref[...] * acc_ref[...]
        if final_norm:
            out = _rmsnorm(out, fg_ref[...])
        o_ref[...] = out


def _ffn(lru, att, x, mod_l, ln_g, w_out_b, wg, wu, wd, final_g=None, *, tm=512, tf=1536):
    bsz, seq, d = x.shape
    w = lru.shape[-1]
    ff = wg.shape[1]
    tf = min(tf, ff)
    final_norm = final_g is not None
    row = lambda k: pl.BlockSpec((None, None, 1, d), lambda b, i, j: (b, k, 0, 0))
    tok = lambda n: pl.BlockSpec((None, tm, n), lambda b, i, j: (b, i, 0))
    in_specs = [tok(w), tok(w), tok(d), row(2), row(4), row(3),
                pl.BlockSpec((1, d), lambda b, i, j: (0, 0)),
                pl.BlockSpec((d, d), lambda b, i, j: (0, 0)),
                pl.BlockSpec((d, tf), lambda b, i, j: (0, j)),
                pl.BlockSpec((d, tf), lambda b, i, j: (0, j)),
                pl.BlockSpec((tf, d), lambda b, i, j: (j, 0)),
                row(5)]
    args = [lru, att, x, mod_l, mod_l, mod_l, ln_g.reshape(1, d), w_out_b, wg, wu, wd, mod_l]
    if final_norm:
        in_specs.append(pl.BlockSpec((1, d), lambda b, i, j: (0, 0)))
        args.append(final_g.reshape(1, d))
    return pl.pallas_call(
        functools.partial(_ffn_kernel, w=w, final_norm=final_norm),
        grid=(bsz, seq // tm, ff // tf),
        in_specs=in_specs,
        out_specs=tok(d),
        out_shape=jax.ShapeDtypeStruct((bsz, seq, d), F32),
        scratch_shapes=[pltpu.VMEM((tm, d), F32), pltpu.VMEM((tm, d), BF16),
                        pltpu.VMEM((tm, d), F32)],
        compiler_params=_cparams(("arbitrary",) * 3),
        name="outproj_ffn",
    )(*args)


MOE_TB = 512
MOE_SEG = 16
MOE_TF = 512
MOE_LR = 2 * MOE_TB + N_EXPERTS * MOE_SEG


def _moe_plan(cnt, *, m_tok):
    i32 = jnp.int32
    nb = m_tok // MOE_TB
    n_e = N_EXPERTS
    rows = -(-(2 * m_tok + nb * n_e * MOE_SEG + n_e * MOE_TF) // MOE_TF) * MOE_TF
    ntf = rows // MOE_TF
    n = cnt.reshape(nb, -1, LANES)[:, :n_e, 0].astype(i32)
    seg_n = (n + MOE_SEG - 1) // MOE_SEG
    seg_src = jnp.cumsum(seg_n, axis=1) - seg_n
    used = jnp.sum(seg_n, axis=0)
    per_tile = MOE_TF // MOE_SEG
    gsz = (used + per_tile - 1) // per_tile * per_tile
    gend = jnp.cumsum(gsz)
    goff = gend - gsz
    seg_dst = goff[None, :] + jnp.cumsum(seg_n, axis=0) - seg_n
    total_tiles = gend[-1] // per_tile
    tile = jnp.arange(ntf, dtype=i32)
    f_valid = (tile < total_tiles).astype(i32)
    f_exp = jnp.minimum(
        jnp.sum(gend[None, :] <= (jnp.minimum(tile, total_tiles - 1) * per_tile)[:, None],
                axis=1).astype(i32), n_e - 1)
    tail_dst = jnp.concatenate([goff + used, gend[-1:]])
    tail_n = jnp.concatenate([gsz - used, rows // MOE_SEG - gend[-1:]])
    return dict(seg_n=seg_n.reshape(-1), seg_src=seg_src.reshape(-1),
                seg_dst=seg_dst.reshape(-1), blk_n=jnp.sum(seg_n, axis=1),
                tail_dst=tail_dst, tail_n=tail_n,
                f_exp=f_exp, f_valid=f_valid, rows=rows, ntf=ntf, nb=nb)


def _seg_rows(unit):
    return pl.ds(pl.multiple_of(unit * MOE_SEG, MOE_SEG), MOE_SEG)


def _wait_segments(sem, buf_ref, n):
    def body(_, carry):
        pltpu.make_async_copy(buf_ref.at[pl.ds(0, MOE_SEG)], buf_ref.at[pl.ds(0, MOE_SEG)],
                              sem).wait()
        return carry
    lax.fori_loop(0, n, body, 0)


def _moe_scatter_kernel(seg_n_ref, seg_src_ref, seg_dst_ref, tail_dst_ref, tail_n_ref,
                        h_ref, route_ref, xs_hbm, buf_ref, zero_ref, sem, tail_sem,
                        issued_ref):
    tb = pl.program_id(0)
    nb = pl.num_programs(0)
    slot = tb % 2
    buf = buf_ref.at[slot]

    @pl.when(tb >= 2)
    def _():
        _wait_segments(sem.at[slot], buf, issued_ref[slot])

    rows = lax.broadcasted_iota(jnp.int32, (MOE_LR, MOE_TB), 0)
    p1 = route_ref[2:3, :].astype(jnp.int32)
    p2 = route_ref[3:4, :].astype(jnp.int32)
    sel = jnp.where(p1 == rows, 1.0, jnp.where(p2 == rows, 1.0, 0.0))
    buf[...] = jnp.dot(sel.astype(BF16), h_ref[...],
                       preferred_element_type=F32).astype(buf_ref.dtype)

    issued = 0
    for e in range(N_EXPERTS):
        k = tb * N_EXPERTS + e
        n, src, dst = seg_n_ref[k], seg_src_ref[k], seg_dst_ref[k]

        def copy_seg(g, carry):
            pltpu.make_async_copy(buf.at[_seg_rows(src + g)], xs_hbm.at[_seg_rows(dst + g)],
                                  sem.at[slot]).start()
            return carry
        lax.fori_loop(0, n, copy_seg, 0)
        issued = issued + n
    issued_ref[slot] = issued

    @pl.when(tb == nb - 1)
    def _():
        zero_ref[...] = jnp.zeros_like(zero_ref)
        n_tail = 0
        for e in range(N_EXPERTS + 1):
            n, dst = tail_n_ref[e], tail_dst_ref[e]

            def zero_seg(g, carry):
                pltpu.make_async_copy(zero_ref, xs_hbm.at[_seg_rows(dst + g)], tail_sem).start()
                return carry
            lax.fori_loop(0, n, zero_seg, 0)
            n_tail = n_tail + n
        _wait_segments(tail_sem, zero_ref, n_tail)
        _wait_segments(sem.at[slot], buf, issued_ref[slot])

        @pl.when(nb >= 2)
        def _():
            _wait_segments(sem.at[1 - slot], buf, issued_ref[1 - slot])


def _moe_scatter(h2, plan, route_t):
    m_tok, d = h2.shape
    nb = plan['nb']
    grid_spec = pltpu.PrefetchScalarGridSpec(
        num_scalar_prefetch=5,
        grid=(nb,),
        in_specs=[pl.BlockSpec((MOE_TB, d), lambda t, *_: (t, 0)),
                  pl.BlockSpec((None, SUBLANES, MOE_TB), lambda t, *_: (t, 0, 0))],
        out_specs=pl.BlockSpec(memory_space=pl.ANY),
        scratch_shapes=[pltpu.VMEM((2, MOE_LR, d), BF16), pltpu.VMEM((MOE_SEG, d), BF16),
                        pltpu.SemaphoreType.DMA((2,)), pltpu.SemaphoreType.DMA(()),
                        pltpu.SMEM((2,), jnp.int32)],
    )
    return pl.pallas_call(
        _moe_scatter_kernel,
        grid_spec=grid_spec,
        out_shape=jax.ShapeDtypeStruct((plan['rows'], d), BF16),
        compiler_params=_cparams(("arbitrary",)),
        name="moe_scatter",
    )(plan['seg_n'], plan['seg_src'], plan['seg_dst'], plan['tail_dst'], plan['tail_n'],
      h2, route_t)


def _moe_ffn_kernel(exp_ref, valid_ref, xs_ref, wg_ref, wu_ref, wd_ref, o_ref, acc_ref):
    n = pl.program_id(0)
    j = pl.program_id(1)

    @pl.when(valid_ref[n] == 0)
    def _():
        o_ref[...] = jnp.zeros_like(o_ref)

    @pl.when(valid_ref[n] != 0)
    def _():
        @pl.when(j == 0)
        def _():
            acc_ref[...] = jnp.zeros_like(acc_ref)

        h = xs_ref[...]
        act = (jax.nn.silu(jnp.dot(h, wg_ref[...], preferred_element_type=F32))
               * jnp.dot(h, wu_ref[...], preferred_element_type=F32))
        acc_ref[...] += jnp.dot(act.astype(BF16), wd_ref[...], preferred_element_type=F32)

        @pl.when(j == pl.num_programs(1) - 1)
        def _():
            o_ref[...] = acc_ref[...].astype(o_ref.dtype)


def _moe_ffn(xs, wg, wu, wd, plan, *, tf=1536):
    rows, d = xs.shape
    ff = wg.shape[2]
    tf = min(tf, ff)
    nj = ff // tf

    def ff_tile(n, j, v):
        return j * v[n] + (nj - 1) * (1 - v[n])

    grid_spec = pltpu.PrefetchScalarGridSpec(
        num_scalar_prefetch=2,
        grid=(plan['ntf'], nj),
        in_specs=[pl.BlockSpec((MOE_TF, d), lambda n, j, e, v: (n, 0)),
                  pl.BlockSpec((None, d, tf), lambda n, j, e, v: (e[n], 0, ff_tile(n, j, v))),
                  pl.BlockSpec((None, d, tf), lambda n, j, e, v: (e[n], 0, ff_tile(n, j, v))),
                  pl.BlockSpec((None, tf, d), lambda n, j, e, v: (e[n], ff_tile(n, j, v), 0))],
        out_specs=pl.BlockSpec((MOE_TF, d), lambda n, j, e, v: (n, 0)),
        scratch_shapes=[pltpu.VMEM((MOE_TF, d), F32)],
    )
    return pl.pallas_call(
        _moe_ffn_kernel,
        grid_spec=grid_spec,
        out_shape=jax.ShapeDtypeStruct((rows, d), BF16),
        compiler_params=_cparams(("arbitrary", "arbitrary")),
        name="moe_ffn",
    )(plan['f_exp'], plan['f_valid'], xs, wg, wu, wd)


def _token_columns(route, field):
    pad = jnp.zeros_like(route)
    pick = jnp.where(lax.broadcasted_iota(jnp.int32, (2 * SUBLANES, LANES), 0) == field,
                     1.0, 0.0).astype(BF16)
    out = None
    rest = route
    for _ in range(3):
        piece = rest.astype(BF16)
        rest = rest - piece.astype(F32)
        term = lax.dot_general(jnp.concatenate([piece, pad.astype(BF16)], axis=0), pick,
                               (((0,), (0,)), ((), ())), preferred_element_type=F32)
        out = term if out is None else out + term
    return out


def _moe_combine_kernel(*refs, final_norm):
    if final_norm:
        (seg_n_ref, seg_src_ref, seg_dst_ref, blk_n_ref, ye_hbm, route_ref,
         x1_ref, g2_ref, fg_ref, o_ref, buf_ref, sem) = refs
    else:
        (seg_n_ref, seg_src_ref, seg_dst_ref, blk_n_ref, ye_hbm, route_ref,
         x1_ref, g2_ref, o_ref, buf_ref, sem) = refs
    tb = pl.program_id(0)
    nb = pl.num_programs(0)
    slot = tb % 2

    def fetch(block, into):
        for e in range(N_EXPERTS):
            k = block * N_EXPERTS + e
            n, src, dst = seg_n_ref[k], seg_src_ref[k], seg_dst_ref[k]

            def copy_seg(g, carry):
                pltpu.make_async_copy(ye_hbm.at[_seg_rows(dst + g)],
                                      buf_ref.at[into, _seg_rows(src + g)], sem.at[into]).start()
                return carry
            lax.fori_loop(0, n, copy_seg, 0)

    @pl.when(tb == 0)
    def _():
        fetch(tb, slot)

    @pl.when(tb + 1 < nb)
    def _():
        fetch(tb + 1, 1 - slot)

    buf = buf_ref.at[slot]
    _wait_segments(sem.at[slot], buf, blk_n_ref[tb])

    def clear(g, carry):
        buf[_seg_rows(g), :] = jnp.zeros((MOE_SEG, buf.shape[1]), buf.dtype)
        return carry
    lax.fori_loop(blk_n_ref[tb], MOE_LR // MOE_SEG, clear, 0)

    ye = buf[...]
    reps = MOE_LR // LANES
    cols = lax.broadcasted_iota(jnp.int32, (MOE_TB, MOE_LR), 1)

    route = route_ref[...]

    def unsort(field):
        pos = _token_columns(route, field).astype(jnp.int32)
        hit = jnp.tile(pos, (1, reps)) == cols
        return jnp.dot(jnp.where(hit, 1.0, 0.0).astype(BF16), ye, preferred_element_type=F32)

    lanes = x1_ref.shape[1] // LANES
    y = (jnp.tile(_token_columns(route, 4), (1, lanes)) * unsort(2)
         + jnp.tile(_token_columns(route, 5), (1, lanes)) * unsort(3))
    out = x1_ref[...] + g2_ref[...] * y
    if final_norm:
        out = _rmsnorm(out, fg_ref[...])
    o_ref[...] = out


def _moe_combine(ye, plan, route_t, x1, mod_l, final_g=None):
    bsz, seq, d = x1.shape
    m_tok = bsz * seq
    nb = plan['nb']
    blocks_per_seq = seq // MOE_TB
    final_norm = final_g is not None
    in_specs = [pl.BlockSpec(memory_space=pl.ANY),
                pl.BlockSpec((None, SUBLANES, MOE_TB), lambda t, *_: (t, 0, 0)),
                pl.BlockSpec((MOE_TB, d), lambda t, *_: (t, 0)),
                pl.BlockSpec((None, None, 1, d), lambda t, *_: (t // blocks_per_seq, 5, 0, 0))]
    args = [ye, route_t, x1.reshape(m_tok, d), mod_l]
    if final_norm:
        in_specs.append(pl.BlockSpec((1, d), lambda t, *_: (0, 0)))
        args.append(final_g.reshape(1, d))
    grid_spec = pltpu.PrefetchScalarGridSpec(
        num_scalar_prefetch=4,
        grid=(nb,),
        in_specs=in_specs,
        out_specs=pl.BlockSpec((MOE_TB, d), lambda t, *_: (t, 0)),
        scratch_shapes=[pltpu.VMEM((2, MOE_LR, d), BF16), pltpu.SemaphoreType.DMA((2,))],
    )
    out = pl.pallas_call(
        functools.partial(_moe_combine_kernel, final_norm=final_norm),
        grid_spec=grid_spec,
        out_shape=jax.ShapeDtypeStruct((m_tok, d), F32),
        compiler_params=_cparams(("arbitrary",)),
        name="moe_combine",
    )(plan['seg_n'], plan['seg_src'], plan['seg_dst'], plan['blk_n'], *args)
    return out.reshape(bsz, seq, d)


def _moe(h2, route_t, cnt, x1, mod_l, wg, wu, wd, final_g=None):
    bsz, seq, d = x1.shape
    m_tok = bsz * seq
    plan = _moe_plan(cnt, m_tok=m_tok)
    xs = _moe_scatter(h2.reshape(m_tok, d), plan, route_t)
    ye = _moe_ffn(xs, wg, wu, wd, plan)
    return _moe_combine(ye, plan, route_t, x1, mod_l, final_g)


def kernel(x, c, positions, ada_w, ada_b, ln1_g, ln2_g, w_in, conv_w, conv_b, gate_a_w, gate_a_b, gate_x_w, gate_x_b, lru_lambda, lam_q1, lam_k1, lam_q2, lam_k2, subln_g, w_out, ffn_w_gate, ffn_w_up, ffn_w_down, moe_router, moe_w_gate, moe_w_up, moe_w_down, final_g):
    depth = ada_w.shape[0]
    vd = subln_g.shape[-1]
    dh = vd // 2
    attn_w = DIFF_HEADS * vd

    mod = _modulation(c, ada_w, ada_b)
    cos_t, sin_t = _rope_tables(positions, dh)
    for l in range(depth):
        lambda_init = 0.8 - 0.6 * math.exp(-0.3 * l)
        mod_l = mod[l]
        lru, q, k, v = _inproj(x, mod_l, ln1_g[l], w_in[l].astype(BF16), cos_t, sin_t,
                               conv_w[l], conv_b[l], gate_a_w[l], gate_a_b[l], gate_x_w[l],
                               gate_x_b[l], lru_lambda[l], attn_w=attn_w, dh=dh)
        att = _attention(q, k, v, positions, lam_q1[l], lam_k1[l], lam_q2[l], lam_k2[l],
                         subln_g[l], lambda_init, dh=dh)
        fg = final_g if l == depth - 1 else None
        j = l // 2
        if l % 2 == 0:
            x = _ffn(lru, att, x, mod_l, ln2_g[l], w_out[l].astype(BF16),
                     ffn_w_gate[j].astype(BF16), ffn_w_up[j].astype(BF16),
                     ffn_w_down[j].astype(BF16), final_g=fg)
        else:
            x1, h2, route, cnt = _outproj_router(lru, att, x, mod_l, ln2_g[l],
                                                 w_out[l].astype(BF16), moe_router[j])
            x = _moe(h2, route, cnt, x1, mod_l, moe_w_gate[j].astype(BF16),
                     moe_w_up[j].astype(BF16), moe_w_down[j].astype(BF16), final_g=fg)
    return x
```
